```python
import math
import jax, jax.numpy as jnp
from jax import lax
import numpy as np

D_MODEL = 2048
BATCH = 2
SEQ = 8192
DEPTH = 2
DEC_BATCH = 16
DEC_SEQ = 64
PAST_LEN = 2048

CHUNK = 64
CONV_W = 4
EPS = 1e-6
D_RNN = D_MODEL // 2
H_A = 8
BLK_A = D_RNN // H_A
RG_C = 8.0
H_B = 8
DH_B = D_MODEL // (4 * H_B)
DV_B = 2 * DH_B
D_ATT = H_B * DV_B
Q_BLOCK = 128
IN_EVEN = 2 * D_RNN + 3 * D_ATT
D_SSD = D_MODEL // 2
P_C = 64
H_C = D_SSD // P_C
N_C = 128
G_C = 2
D_XBC = D_SSD + 2 * G_C * N_C
SSD_CHUNK = CHUNK
D_HG = D_MODEL // 2
H_D = 8
DK_D = D_HG // H_D
DV_D = DK_D
HG_CHUNK = 16
IN_ODD = D_SSD + D_XBC + H_C + 3 * D_HG
D_FF = ((8 * D_MODEL // 3 + 255) // 256) * 256

kernel_name = 'hybrid_stream_encoder_step'


def rms_norm(x, w):
    xf = x.astype(jnp.float32)
    y = xf * lax.rsqrt(jnp.mean(xf * xf, axis=-1, keepdims=True) + EPS)
    return (y * w.astype(jnp.float32)).astype(x.dtype)


def causal_conv(x, buf, w, b):
    L = x.shape[1]
    xp = jnp.concatenate([buf.astype(x.dtype), x], axis=1)
    y = b.astype(x.dtype)
    for tap in range(CONV_W):
        y = y + xp[:, tap:tap + L] * w[tap]
    return y, xp[:, L:]


def swiglu(x, w_gate, w_up, w_down):
    return (jax.nn.silu(x @ w_gate) * (x @ w_up)) @ w_down


def rglru(x, h0, w_r, b_r, w_i, b_i, lam):
    bn, L, _ = x.shape
    xb = x.reshape(bn, L, H_A, BLK_A)
    r = jax.nn.sigmoid(jnp.einsum('blhi,hij->blhj', xb, w_r).reshape(bn, L, D_RNN).astype(jnp.float32) + b_r.astype(jnp.float32))
    gi = jax.nn.sigmoid(jnp.einsum('blhi,hij->blhj', xb, w_i).reshape(bn, L, D_RNN).astype(jnp.float32) + b_i.astype(jnp.float32))
    log_a = -RG_C * r * jax.nn.softplus(-lam.astype(jnp.float32))
    a = jnp.exp(log_a)
    u = jnp.sqrt(-jnp.expm1(2.0 * log_a)) * (gi * x.astype(jnp.float32))
    u = u.at[:, 0].add(a[:, 0] * h0.astype(jnp.float32))

    def combine(e1, e2):
        a1, b1 = e1
        a2, b2 = e2
        return a1 * a2, a2 * b1 + b2

    _, h = lax.associative_scan(combine, (a, u), axis=1)
    return h.astype(x.dtype), h[:, -1].astype(x.dtype)


def diff_attention(q, k, v, pos0, lam):
    bn, Lq = q.shape[:2]
    Lk = k.shape[1]
    qb = Q_BLOCK if Lq % Q_BLOCK == 0 else Lq
    nblk = Lq // qb
    q_blocks = q.reshape(bn, nblk, qb, H_B, 2, DH_B).swapaxes(0, 1)
    qpos = (pos0 + jnp.arange(Lq)).reshape(nblk, qb)
    kchunk = jnp.arange(Lk) // CHUNK
    scale = DH_B ** -0.5

    def block(args):
        qblk, qp = args
        s = jnp.einsum('bqhcd,bkhcd->bhcqk', qblk, k).astype(jnp.float32) * scale
        mask = kchunk[None, :] <= (qp // CHUNK)[:, None]
        p = jax.nn.softmax(jnp.where(mask, s, -jnp.inf), axis=-1)
        wgt = p[:, :, 0] - lam * p[:, :, 1]
        return jnp.einsum('bhqk,bkhv->bqhv', wgt.astype(v.dtype), v)

    o = lax.map(block, (q_blocks, qpos))
    return o.swapaxes(0, 1).reshape(bn, Lq, H_B, DV_B)


def even_mixer(x, layer, k_cache, v_cache, conv_buf, h0, p):
    (w_in, conv_w, conv_b, w_r, b_r, w_i, b_i, lam_p, lq1, lk1, lq2, lk2, subln_w, w_out) = p
    bn, L, _ = x.shape
    proj = x @ w_in
    xa, ga, q, k, v = jnp.split(proj, [D_RNN, 2 * D_RNN, 2 * D_RNN + D_ATT, 2 * D_RNN + 2 * D_ATT], axis=-1)
    xc, new_conv = causal_conv(xa, conv_buf, conv_w, conv_b)
    h, h_last = rglru(xc, h0, w_r, b_r, w_i, b_i, lam_p)
    ya = h * jax.nn.gelu(ga)
    q = q.reshape(bn, L, H_B, 2, DH_B)
    k = k.reshape(bn, L, H_B, DV_B)
    v = v.reshape(bn, L, H_B, DV_B)
    if k_cache is None:
        pos0, k_all, v_all = 0, k, v
    else:
        pos0 = k_cache.shape[1]
        k_all = jnp.concatenate([k_cache.astype(k.dtype), k], axis=1)
        v_all = jnp.concatenate([v_cache.astype(v.dtype), v], axis=1)
    lambda_init = 0.8 - 0.6 * math.exp(-0.3 * layer)
    lam = jnp.exp(jnp.sum(lq1.astype(jnp.float32) * lk1.astype(jnp.float32))) - jnp.exp(jnp.sum(lq2.astype(jnp.float32) * lk2.astype(jnp.float32))) + lambda_init
    yb = diff_attention(q, k_all.reshape(bn, -1, H_B, 2, DH_B), v_all, pos0, lam)
    yb = rms_norm(yb, subln_w) * (1.0 - lambda_init)
    y = jnp.concatenate([ya, yb.reshape(bn, L, D_ATT)], axis=-1) @ w_out
    return y, k, v, new_conv, h_last


def ssd_scan(x, dt, a, bm, cm, s0):
    bn, L = x.shape[:2]
    Q = SSD_CHUNK if L % SSD_CHUNK == 0 else L
    nc = L // Q
    E = H_C // G_C
    xr = x.astype(jnp.float32).reshape(bn, nc, Q, G_C, E, P_C)
    dtr = dt.reshape(bn, nc, Q, G_C, E)
    br = bm.astype(jnp.float32).reshape(bn, nc, Q, G_C, N_C)
    cr = cm.astype(jnp.float32).reshape(bn, nc, Q, G_C, N_C)
    acum = jnp.cumsum(dtr * a.reshape(G_C, E), axis=2)
    causal = jnp.tril(jnp.ones((Q, Q), bool))[:, :, None, None]
    decay = jnp.exp(jnp.where(causal, acum[:, :, :, None] - acum[:, :, None, :], -jnp.inf))
    cb = jnp.einsum('bcign,bcjgn->bcijg', cr, br)
    y_diag = jnp.einsum('bcijge,bcjgep->bcigep', cb[..., None] * decay, dtr[..., None] * xr)
    w_state = jnp.exp(acum[:, :, -1:] - acum) * dtr
    chunk_states = jnp.einsum('bcjgn,bcjgep->bcgepn', br, w_state[..., None] * xr)
    chunk_decay = jnp.exp(acum[:, :, -1])

    def step(s, inp):
        st, dec = inp
        return dec[..., None, None] * s + st, s

    s_init = s0.astype(jnp.float32).reshape(bn, G_C, E, P_C, N_C)
    s_last, s_prev = lax.scan(step, s_init, (chunk_states.swapaxes(0, 1), chunk_decay.swapaxes(0, 1)))
    s_prev = s_prev.swapaxes(0, 1)
    y_off = jnp.einsum('bcign,bcgepn->bcigep', cr, s_prev) * jnp.exp(acum)[..., None]
    y = (y_diag + y_off).reshape(bn, L, H_C, P_C)
    return y, s_last.reshape(bn, H_C, P_C, N_C).astype(x.dtype)


def hgrn2(q, f, i, s0, lb):
    bn, L, _ = q.shape
    Q = HG_CHUNK if L % HG_CHUNK == 0 else L
    nc = L // Q
    g = lb + (1.0 - lb) * jax.nn.sigmoid(f.astype(jnp.float32))
    logg = jnp.log(g)
    kk = 1.0 - g
    qq = jax.nn.silu(q.astype(jnp.float32))

    def blocks(t, d):
        return t.reshape(bn, nc, Q, H_D, d).swapaxes(0, 1)

    qc, kc, ic, bc = blocks(qq, DK_D), blocks(kk, DK_D), blocks(i.astype(jnp.float32), DV_D), jnp.cumsum(blocks(logg, DK_D), axis=2)
    causal = jnp.tril(jnp.ones((Q, Q), bool))[:, :, None, None]

    def step(S, inp):
        qb, kb, ib, bb = inp
        inter = jnp.einsum('bihk,bhkv->bihv', qb * jnp.exp(bb), S)
        rel = jnp.exp(jnp.where(causal, bb[:, :, None] - bb[:, None, :], -jnp.inf))
        att = jnp.einsum('bijhk,bjhk->bhij', qb[:, :, None] * rel, kb)
        intra = jnp.einsum('bhij,bjhv->bihv', att, ib)
        blast = bb[:, -1]
        S_new = jnp.exp(blast)[..., None] * S + jnp.einsum('bjhk,bjhv->bhkv', kb * jnp.exp(blast[:, None] - bb), ib)
        return S_new, inter + intra

    s_last, o = lax.scan(step, s0.astype(jnp.float32), (qc, kc, ic, bc))
    o = o.swapaxes(0, 1).reshape(bn, L, H_D, DV_D)
    return o.astype(q.dtype), s_last.astype(q.dtype)


def odd_mixer(x, layer, conv_buf, s0, hg0, p):
    (w_in, conv_w, conv_b, dt_bias, a_log, d_skip, ssd_norm_w, hg_lb, hg_norm_w, w_out) = p
    bn, L, _ = x.shape
    proj = x @ w_in
    o1 = D_SSD
    o2 = o1 + D_XBC
    o3 = o2 + H_C
    o4 = o3 + D_HG
    o5 = o4 + D_HG
    z, xbc, dt, qd, fd, idn = jnp.split(proj, [o1, o2, o3, o4, o5], axis=-1)
    xbc, new_conv = causal_conv(xbc, conv_buf, conv_w, conv_b)
    xbc = jax.nn.silu(xbc)
    xs = xbc[..., :D_SSD].reshape(bn, L, H_C, P_C)
    bm = xbc[..., D_SSD:D_SSD + G_C * N_C].reshape(bn, L, G_C, N_C)
    cm = xbc[..., D_SSD + G_C * N_C:].reshape(bn, L, G_C, N_C)
    dt = jax.nn.softplus(dt.astype(jnp.float32) + dt_bias.astype(jnp.float32))
    a = -jnp.exp(a_log.astype(jnp.float32))
    ys, s_last = ssd_scan(xs, dt, a, bm, cm, s0)
    ys = ys + d_skip.astype(jnp.float32)[:, None] * xs.astype(jnp.float32)
    ys = (ys.reshape(bn, L, D_SSD) * jax.nn.silu(z.astype(jnp.float32))).astype(x.dtype)
    ys = rms_norm(ys.reshape(bn, L, G_C, D_SSD // G_C), ssd_norm_w.reshape(G_C, D_SSD // G_C)).reshape(bn, L, D_SSD)
    lb_all = jnp.cumsum(jax.nn.softmax(hg_lb.astype(jnp.float32), axis=0), axis=0)
    lb = lb_all[layer] - lb_all[0]
    yh, hg_last = hgrn2(qd, fd, idn, hg0, lb)
    yh = rms_norm(yh, hg_norm_w).reshape(bn, L, D_HG)
    y = jnp.concatenate([ys, yh], axis=-1) @ w_out
    return y, new_conv, s_last, hg_last


def trunk(x, k_cache, v_cache, rg_conv, rg_h, ssd_conv, ssd_s, hg_s, norm_w, even_p, odd_p, ffn_p):
    w_gate, w_up, w_down = ffn_p
    for layer in range(DEPTH):
        h = rms_norm(x, norm_w[layer, 0])
        if layer % 2 == 0:
            m, k_new, v_new, rg_conv_new, rg_h_new = even_mixer(h, layer, k_cache, v_cache, rg_conv, rg_h, even_p)
        else:
            m, ssd_conv_new, ssd_s_new, hg_s_new = odd_mixer(h, layer, ssd_conv, ssd_s, hg_s, odd_p)
        x = x + rms_norm(m, norm_w[layer, 1])
        h = rms_norm(x, norm_w[layer, 2])
        x = x + rms_norm(swiglu(h, w_gate[layer], w_up[layer], w_down[layer]), norm_w[layer, 3])
    return x, k_new, v_new, rg_conv_new, rg_h_new, ssd_conv_new, ssd_s_new, hg_s_new


def setup_inputs(seed: int = 0) -> dict:
    key = jax.random.key(seed)
    ks = iter(jax.random.split(key, 48))

    def nrm(shape, s=1.0):
        return jax.random.normal(next(ks), shape, jnp.float32) * s

    u_a = jax.random.uniform(next(ks), (D_RNN,), jnp.float32, 0.9, 0.999)
    dt0 = jnp.exp(jax.random.uniform(next(ks), (H_C,), jnp.float32, math.log(1e-3), math.log(1e-1)))
    a0 = jax.random.uniform(next(ks), (H_C,), jnp.float32, 1.0, 16.0)
    return {
        'x_prompt': nrm((BATCH, SEQ, D_MODEL)),
        'x_sample': nrm((DEC_BATCH, DEC_SEQ, D_MODEL)),
        'cache_diff_k': nrm((DEC_BATCH, PAST_LEN, H_B, DV_B)),
        'cache_diff_v': nrm((DEC_BATCH, PAST_LEN, H_B, DV_B)),
        'state_rglru_conv': nrm((DEC_BATCH, CONV_W - 1, D_RNN)),
        'state_rglru_h': nrm((DEC_BATCH, D_RNN), 0.5),
        'state_ssd_conv': nrm((DEC_BATCH, CONV_W - 1, D_XBC)),
        'state_ssd': nrm((DEC_BATCH, H_C, P_C, N_C), 0.1),
        'state_hgrn': nrm((DEC_BATCH, H_D, DK_D, DV_D), 0.5),
        'norm_w': 1.0 + nrm((DEPTH, 4, D_MODEL), 0.02),
        'l0_w_in': nrm((D_MODEL, IN_EVEN), D_MODEL ** -0.5),
        'l0_conv_w': nrm((CONV_W, D_RNN), CONV_W ** -0.5),
        'l0_conv_b': nrm((D_RNN,), 0.02),
        'l0_rg_w_r': nrm((H_A, BLK_A, BLK_A), BLK_A ** -0.5),
        'l0_rg_b_r': nrm((D_RNN,), 0.02),
        'l0_rg_w_i': nrm((H_A, BLK_A, BLK_A), BLK_A ** -0.5),
        'l0_rg_b_i': nrm((D_RNN,), 0.02),
        'l0_rg_lambda': jnp.log(u_a) - jnp.log1p(-u_a),
        'l0_lq1': nrm((DH_B,), 0.1),
        'l0_lk1': nrm((DH_B,), 0.1),
        'l0_lq2': nrm((DH_B,), 0.1),
        'l0_lk2': nrm((DH_B,), 0.1),
        'l0_subln_w': 1.0 + nrm((DV_B,), 0.02),
        'l0_w_out': nrm((D_MODEL, D_MODEL), D_MODEL ** -0.5),
        'l1_w_in': nrm((D_MODEL, IN_ODD), D_MODEL ** -0.5),
        'l1_conv_w': nrm((CONV_W, D_XBC), CONV_W ** -0.5),
        'l1_conv_b': nrm((D_XBC,), 0.02),
        'l1_dt_bias': dt0 + jnp.log(-jnp.expm1(-dt0)),
        'l1_a_log': jnp.log(a0),
        'l1_d_skip': 1.0 + nrm((H_C,), 0.02),
        'l1_ssd_norm_w': 1.0 + nrm((D_SSD,), 0.02),
        'l1_hg_lower_bound': nrm((DEPTH, D_HG)),
        'l1_hg_norm_w': 1.0 + nrm((DV_D,), 0.02),
        'l1_w_out': nrm((D_MODEL, D_MODEL), D_MODEL ** -0.5),
        'ffn_w_gate': nrm((DEPTH, D_MODEL, D_FF), D_MODEL ** -0.5),
        'ffn_w_up': nrm((DEPTH, D_MODEL, D_FF), D_MODEL ** -0.5),
        'ffn_w_down': nrm((DEPTH, D_FF, D_MODEL), D_FF ** -0.5),
    }


def reference(x_prompt, x_sample, cache_diff_k, cache_diff_v, state_rglru_conv, state_rglru_h, state_ssd_conv, state_ssd, state_hgrn,
              norm_w, l0_w_in, l0_conv_w, l0_conv_b, l0_rg_w_r, l0_rg_b_r, l0_rg_w_i, l0_rg_b_i, l0_rg_lambda,
              l0_lq1, l0_lk1, l0_lq2, l0_lk2, l0_subln_w, l0_w_out,
              l1_w_in, l1_conv_w, l1_conv_b, l1_dt_bias, l1_a_log, l1_d_skip, l1_ssd_norm_w, l1_hg_lower_bound, l1_hg_norm_w, l1_w_out,
              ffn_w_gate, ffn_w_up, ffn_w_down):
    even_p = (l0_w_in, l0_conv_w, l0_conv_b, l0_rg_w_r, l0_rg_b_r, l0_rg_w_i, l0_rg_b_i, l0_rg_lambda,
              l0_lq1, l0_lk1, l0_lq2, l0_lk2, l0_subln_w, l0_w_out)
    odd_p = (l1_w_in, l1_conv_w, l1_conv_b, l1_dt_bias, l1_a_log, l1_d_skip, l1_ssd_norm_w, l1_hg_lower_bound, l1_hg_norm_w, l1_w_out)
    ffn_p = (ffn_w_gate, ffn_w_up, ffn_w_down)
    bp = x_prompt.shape[0]
    dtp = x_prompt.dtype
    (y_prompt, prompt_k, prompt_v, prompt_rglru_conv, prompt_rglru_h,
     prompt_ssd_conv, prompt_ssd, prompt_hgrn) = trunk(
        x_prompt, None, None,
        jnp.zeros((bp, CONV_W - 1, D_RNN), dtp), jnp.zeros((bp, D_RNN), dtp),
        jnp.zeros((bp, CONV_W - 1, D_XBC), dtp), jnp.zeros((bp, H_C, P_C, N_C), dtp),
        jnp.zeros((bp, H_D, DK_D, DV_D), dtp),
        norm_w, even_p, odd_p, ffn_p)
    (y_sample, sample_k, sample_v, sample_rglru_conv, sample_rglru_h,
     sample_ssd_conv, sample_ssd, sample_hgrn) = trunk(
        x_sample, cache_diff_k, cache_diff_v, state_rglru_conv, state_rglru_h,
        state_ssd_conv, state_ssd, state_hgrn,
        norm_w, even_p, odd_p, ffn_p)
    return (y_prompt, y_sample,
            prompt_k, prompt_v, prompt_rglru_conv, prompt_rglru_h, prompt_ssd_conv, prompt_ssd, prompt_hgrn,
            sample_k, sample_v, sample_rglru_conv, sample_rglru_h, sample_ssd_conv, sample_ssd, sample_hgrn)
```

```python
import functools
import math

import numpy as np
import jax
import jax.numpy as jnp
from jax import lax
from jax.experimental import pallas as pl
from jax.experimental.pallas import tpu as pltpu

F32 = jnp.float32
BF16 = jnp.bfloat16

EPS = 1e-6
CHUNK = 64
CONV_W = 4
RG_C = 8.0
LANES = 128
SUBLANES = 8
HG_SUB = 16
NEG = -1e30
VMEM_LIMIT_BYTES = 56 * 1024 * 1024

_NT = (((1,), (1,)), ((), ()))
_TN = (((0,), (0,)), ((), ()))


def _cparams(*sem):
    return pltpu.CompilerParams(dimension_semantics=sem, vmem_limit_bytes=VMEM_LIMIT_BYTES)


def _sigmoid(x):
    return 1.0 / (1.0 + jnp.exp(-x))


def _silu(x):
    return x * _sigmoid(x)


def _softplus(x):
    return jnp.maximum(x, 0.0) + jnp.log1p(jnp.exp(-jnp.abs(x)))


def _neg_expm1(x):
    return -jnp.tanh(0.5 * x) * (jnp.exp(x) + 1.0)


def _gelu_tanh(x):
    c = math.sqrt(2.0 / math.pi)
    return x * (0.5 * (1.0 + jnp.tanh(c * (x + 0.044715 * (x * x * x)))))


def _rms(x, w):
    return x * lax.rsqrt(jnp.mean(x * x, axis=-1, keepdims=True) + EPS) * w


def _dot(a, b):
    return jnp.dot(a, b, preferred_element_type=F32)


def _cumsum_rows(x):
    n = x.shape[0]
    row = lax.broadcasted_iota(jnp.int32, x.shape, 0)
    s = 1
    while s < n:
        x = x + jnp.where(row >= s, pltpu.roll(x, s, axis=0), 0.0)
        s *= 2
    return x


def _norm_matmul_kernel(x_ref, nw_ref, w_ref, o_ref, xn_ref):
    @pl.when(pl.program_id(1) == 0)
    def _():
        xn_ref[...] = _rms(x_ref[...], nw_ref[...]).astype(BF16)

    o_ref[...] = _dot(xn_ref[...], w_ref[...])


def _norm_matmul(x, nw, w, tn):
    m, d = x.shape
    n = w.shape[1]
    tm = min(512, m)
    return pl.pallas_call(
        _norm_matmul_kernel,
        grid=(m // tm, n // tn),
        in_specs=[
            pl.BlockSpec((tm, d), lambda i, j: (i, 0)),
            pl.BlockSpec((1, d), lambda i, j: (0, 0)),
            pl.BlockSpec((d, tn), lambda i, j: (0, j)),
        ],
        out_specs=pl.BlockSpec((tm, tn), lambda i, j: (i, j)),
        out_shape=jax.ShapeDtypeStruct((m, n), F32),
        scratch_shapes=[pltpu.VMEM((tm, d), BF16)],
        compiler_params=_cparams("parallel", "arbitrary"),
        name="norm_in_proj",
    )(x, nw.reshape(1, d), w)


def _out_proj_kernel(ya_ref, yb_ref, wa_ref, wb_ref, x_ref, nw_ref, o_ref):
    m = _dot(ya_ref[...], wa_ref[...]) + _dot(yb_ref[...], wb_ref[...])
    o_ref[...] = x_ref[...] + _rms(m, nw_ref[...])


def _out_proj(ya, yb, w_out, x, nw):
    m, d = x.shape
    da, db = ya.shape[1], yb.shape[1]
    tm = min(512, m)
    return pl.pallas_call(
        _out_proj_kernel,
        grid=(m // tm,),
        in_specs=[
            pl.BlockSpec((tm, da), lambda i: (i, 0)),
            pl.BlockSpec((tm, db), lambda i: (i, 0)),
            pl.BlockSpec((da, d), lambda i: (0, 0)),
            pl.BlockSpec((db, d), lambda i: (0, 0)),
            pl.BlockSpec((tm, d), lambda i: (i, 0)),
            pl.BlockSpec((1, d), lambda i: (0, 0)),
        ],
        out_specs=pl.BlockSpec((tm, d), lambda i: (i, 0)),
        out_shape=jax.ShapeDtypeStruct((m, d), F32),
        compiler_params=_cparams("parallel"),
        name="out_proj",
    )(ya, yb, w_out[:da], w_out[da:], x, nw.reshape(1, d))


def _ffn_kernel(x_ref, nwa_ref, nwb_ref, wg_ref, wu_ref, wd_ref, o_ref, hn_ref, acc_ref):
    f = pl.program_id(1)

    @pl.when(f == 0)
    def _():
        hn_ref[...] = _rms(x_ref[...], nwa_ref[...]).astype(BF16)
        acc_ref[...] = jnp.zeros_like(acc_ref)

    hn = hn_ref[...]
    g = _dot(hn, wg_ref[...])
    u = _dot(hn, wu_ref[...])
    a = (_silu(g) * u).astype(BF16)
    acc_ref[...] += _dot(a, wd_ref[...])

    @pl.when(f == pl.num_programs(1) - 1)
    def _():
        o_ref[...] = x_ref[...] + _rms(acc_ref[...], nwb_ref[...])


def _ffn(x, nw_pre, nw_post, wg, wu, wd):
    m, d = x.shape
    dff = wg.shape[1]
    tm = min(512, m)
    tf = 512
    return pl.pallas_call(
        _ffn_kernel,
        grid=(m // tm, dff // tf),
        in_specs=[
            pl.BlockSpec((tm, d), lambda i, f: (i, 0)),
            pl.BlockSpec((1, d), lambda i, f: (0, 0)),
            pl.BlockSpec((1, d), lambda i, f: (0, 0)),
            pl.BlockSpec((d, tf), lambda i, f: (0, f)),
            pl.BlockSpec((d, tf), lambda i, f: (0, f)),
            pl.BlockSpec((tf, d), lambda i, f: (f, 0)),
        ],
        out_specs=pl.BlockSpec((tm, d), lambda i, f: (i, 0)),
        out_shape=jax.ShapeDtypeStruct((m, d), F32),
        scratch_shapes=[pltpu.VMEM((tm, d), BF16), pltpu.VMEM((tm, d), F32)],
        compiler_params=_cparams("parallel", "arbitrary"),
        name="ffn",
    )(x, nw_pre.reshape(1, d), nw_post.reshape(1, d), wg, wu, wd)


def _conv_tile(xs_ref, x_tile, cw_ref, cb_ref, tl):
    xs_ref[8:8 + tl, :] = x_tile
    y = cb_ref[...]
    for tap in range(CONV_W):
        y = y + xs_ref[5 + tap:5 + tap + tl, :] * cw_ref[tap:tap + 1, :]
    xs_ref[5:8, :] = xs_ref[tl + 5:tl + 8, :]
    return y


def _rglru_kernel(xa_ref, ga_ref, cbuf_ref, h0_ref, cw_ref, cb_ref, wr_ref, br_ref, wi_ref, bi_ref,
                  lam_ref, ya_ref, hlast_ref, xs_ref, a_ref, u_ref, h_ref, *, tl, n_blk):
    t = pl.program_id(1)

    @pl.when(t == 0)
    def _():
        xs_ref[5:8, :] = cbuf_ref[...]
        h_ref[...] = h0_ref[...]

    xc = _conv_tile(xs_ref, xa_ref[...], cw_ref, cb_ref, tl)
    xcb = xc.astype(BF16)
    r_pre = jnp.concatenate(
        [_dot(xcb[:, hb * LANES:(hb + 1) * LANES], wr_ref[hb]) for hb in range(n_blk)], axis=1)
    i_pre = jnp.concatenate(
        [_dot(xcb[:, hb * LANES:(hb + 1) * LANES], wi_ref[hb]) for hb in range(n_blk)], axis=1)
    r = _sigmoid(r_pre + br_ref[...])
    gi = _sigmoid(i_pre + bi_ref[...])
    log_a = -RG_C * r * _softplus(-lam_ref[...])
    a_ref[...] = jnp.exp(log_a)
    u_ref[...] = jnp.sqrt(_neg_expm1(2.0 * log_a)) * (gi * xc)

    row = lax.broadcasted_iota(jnp.int32, (SUBLANES, a_ref.shape[1]), 0)

    def group(gidx, h_prev):
        r0 = pl.multiple_of(gidx * SUBLANES, SUBLANES)
        ag = a_ref[pl.ds(r0, SUBLANES), :]
        ug = u_ref[pl.ds(r0, SUBLANES), :]
        s = 1
        while s < SUBLANES:
            a_sh = jnp.where(row >= s, pltpu.roll(ag, s, axis=0), 1.0)
            u_sh = jnp.where(row >= s, pltpu.roll(ug, s, axis=0), 0.0)
            ug = ag * u_sh + ug
            ag = ag * a_sh
            s *= 2
        hg = ag * h_prev + ug
        u_ref[pl.ds(r0, SUBLANES), :] = hg
        return hg[SUBLANES - 1:SUBLANES, :]

    h_last = lax.fori_loop(0, tl // SUBLANES, group, h_ref[...])
    h_ref[...] = h_last
    hlast_ref[...] = h_last
    ya_ref[...] = (u_ref[...] * _gelu_tanh(ga_ref[...])).astype(BF16)


def _rglru(proj, conv_buf, h0, conv_w, conv_b, w_r, b_r, w_i, b_i, lam, d_rnn):
    bn, l, _ = proj.shape
    tl = min(256, l)
    n_blk = w_r.shape[0]
    vec = lambda v: v.reshape(1, d_rnn)
    const2 = lambda b, t: (0, 0)
    kern = functools.partial(_rglru_kernel, tl=tl, n_blk=n_blk)
    ya, h_last = pl.pallas_call(
        kern,
        grid=(bn, l // tl),
        in_specs=[
            pl.BlockSpec((None, tl, d_rnn), lambda b, t: (b, t, 0)),
            pl.BlockSpec((None, tl, d_rnn), lambda b, t: (b, t, 1)),
            pl.BlockSpec((None, CONV_W - 1, d_rnn), lambda b, t: (b, 0, 0)),
            pl.BlockSpec((None, 1, d_rnn), lambda b, t: (b, 0, 0)),
            pl.BlockSpec((CONV_W, d_rnn), const2),
            pl.BlockSpec((1, d_rnn), const2),
            pl.BlockSpec(w_r.shape, lambda b, t: (0, 0, 0)),
            pl.BlockSpec((1, d_rnn), const2),
            pl.BlockSpec(w_i.shape, lambda b, t: (0, 0, 0)),
            pl.BlockSpec((1, d_rnn), const2),
            pl.BlockSpec((1, d_rnn), const2),
        ],
        out_specs=[
            pl.BlockSpec((None, tl, d_rnn), lambda b, t: (b, t, 0)),
            pl.BlockSpec((None, 1, d_rnn), lambda b, t: (b, 0, 0)),
        ],
        out_shape=[
            jax.ShapeDtypeStruct((bn, l, d_rnn), BF16),
            jax.ShapeDtypeStruct((bn, 1, d_rnn), F32),
        ],
        scratch_shapes=[
            pltpu.VMEM((tl + 8, d_rnn), F32),
            pltpu.VMEM((tl, d_rnn), F32),
            pltpu.VMEM((tl, d_rnn), F32),
            pltpu.VMEM((1, d_rnn), F32),
        ],
        compiler_params=_cparams("parallel", "arbitrary"),
        name="rglru",
    )(proj, proj, conv_buf, h0.reshape(bn, 1, d_rnn), conv_w, vec(conv_b), w_r, vec(b_r), w_i, vec(b_i),
      vec(lam))
    return ya, h_last.reshape(bn, d_rnn)


def _attn_kernel(qi_ref, kind_ref, kc_idx_ref, kn_idx_ref, *refs, has_cache, tq, dh, pos0, scale, lam_init):
    if has_cache:
        (q_ref, kc_ref, vc_ref, kn_ref, vn_ref, lq1_ref, lk1_ref, lq2_ref, lk2_ref, sw_ref, o_ref,
         q1_ref, q2_ref, m_ref, l_ref, acc_ref) = refs
    else:
        (q_ref, kn_ref, vn_ref, lq1_ref, lk1_ref, lq2_ref, lk2_ref, sw_ref, o_ref,
         q1_ref, q2_ref, m_ref, l_ref, acc_ref) = refs
    s_id = pl.program_id(2)
    kind = kind_ref[s_id]
    is_first = (kind & 1) == 1
    is_cache = (kind & 2) == 2
    is_diag = (kind & 4) == 4

    @pl.when(is_first)
    def _():
        q = q_ref[...] * scale
        lane = lax.broadcasted_iota(jnp.int32, q.shape, 1)
        q1_ref[...] = jnp.where(lane < dh, q, 0.0).astype(BF16)
        q2_ref[...] = jnp.where(lane >= dh, q, 0.0).astype(BF16)
        m_ref[...] = jnp.full_like(m_ref, NEG)
        l_ref[...] = jnp.zeros_like(l_ref)
        acc_ref[...] = jnp.zeros_like(acc_ref)

    def update(k, v, mask):
        kb = k.astype(BF16)
        vb = v.astype(BF16)
        for c, qc_ref in enumerate((q1_ref, q2_ref)):
            s = lax.dot_general(qc_ref[...], kb, _NT, preferred_element_type=F32)
            if mask is not None:
                s = jnp.where(mask, s, NEG)
            m_prev = m_ref[c]
            m_new = jnp.maximum(m_prev, jnp.max(s, axis=-1, keepdims=True))
            alpha = jnp.exp(m_prev - m_new)
            p = jnp.exp(s - m_new)
            l_ref[c] = alpha * l_ref[c] + jnp.sum(p, axis=-1, keepdims=True)
            acc_ref[c] = alpha * acc_ref[c] + _dot(p.astype(BF16), vb)
            m_ref[c] = m_new

    if has_cache:
        @pl.when(is_cache)
        def _():
            update(kc_ref[...], vc_ref[...], None)

    @pl.when(jnp.logical_and(jnp.logical_not(is_cache), jnp.logical_not(is_diag)))
    def _():
        update(kn_ref[...], vn_ref[...], None)

    @pl.when(is_diag)
    def _():
        base = pos0 + qi_ref[s_id] * tq
        qpos = base + lax.broadcasted_iota(jnp.int32, (tq, tq), 0)
        kpos = base + lax.broadcasted_iota(jnp.int32, (tq, tq), 1)
        shift = CHUNK.bit_length() - 1
        mask = jnp.right_shift(kpos, shift) <= jnp.right_shift(qpos, shift)
        update(kn_ref[...], vn_ref[...], mask)
        lam = (jnp.exp(jnp.sum(lq1_ref[...] * lk1_ref[...], axis=-1, keepdims=True))
               - jnp.exp(jnp.sum(lq2_ref[...] * lk2_ref[...], axis=-1, keepdims=True)) + lam_init)
        o = acc_ref[0] / l_ref[0] - lam * (acc_ref[1] / l_ref[1])
        o_ref[...] = (_rms(o, sw_ref[...]) * (1.0 - lam_init)).astype(BF16)


def _diff_attention(proj, q_col, k_col, v_col, k_cache, v_cache, lq1, lk1, lq2, lk2, subln_w, n_heads, layer):
    bn, l, _ = proj.shape
    dv = LANES
    dh = dv // 2
    has_cache = k_cache is not None
    tq = min(512, l)
    nq = l // tq
    assert tq % CHUNK == 0 and l % tq == 0
    if has_cache:
        pos0 = k_cache.shape[1]
        assert pos0 % CHUNK == 0
        tkc = min(512, pos0)
        assert pos0 % tkc == 0
        n_cache = pos0 // tkc
        kc = k_cache.reshape(bn, pos0, n_heads * dv)
        vc = v_cache.reshape(bn, pos0, n_heads * dv)
    else:
        pos0, n_cache, tkc = 0, 0, 0
    qi, kind, kc_idx, kn_idx = [], [], [], []
    for i in range(nq):
        for j in range(n_cache + i + 1):
            qi.append(i)
            first = 1 if j == 0 else 0
            if j < n_cache:
                kind.append(first | 2)
                kc_idx.append(j)
                kn_idx.append(kn_idx[-1] if kn_idx else 0)
            else:
                jn = j - n_cache
                kind.append(first | (4 if jn == i else 0))
                kc_idx.append(max(n_cache - 1, 0))
                kn_idx.append(jn)
    n_steps = len(qi)
    tables = [jnp.asarray(np.asarray(t, np.int32)) for t in (qi, kind, kc_idx, kn_idx)]
    lam_init = 0.8 - 0.6 * math.exp(-0.3 * layer)
    kern = functools.partial(_attn_kernel, has_cache=has_cache, tq=tq, dh=dh, pos0=pos0,
                             scale=dh ** -0.5, lam_init=lam_init)
    small = lambda n: pl.BlockSpec((1, n), lambda b, h, s, *_: (0, 0))
    in_specs = [pl.BlockSpec((None, tq, dv), lambda b, h, s, qi, kd, kc_i, kn_i: (b, qi[s], q_col + h))]
    args = [proj]
    if has_cache:
        in_specs += [
            pl.BlockSpec((None, tkc, dv), lambda b, h, s, qi, kd, kc_i, kn_i: (b, kc_i[s], h)),
            pl.BlockSpec((None, tkc, dv), lambda b, h, s, qi, kd, kc_i, kn_i: (b, kc_i[s], h)),
        ]
        args += [kc, vc]
    in_specs += [
        pl.BlockSpec((None, tq, dv), lambda b, h, s, qi, kd, kc_i, kn_i: (b, kn_i[s], k_col + h)),
        pl.BlockSpec((None, tq, dv), lambda b, h, s, qi, kd, kc_i, kn_i: (b, kn_i[s], v_col + h)),
        small(dh), small(dh), small(dh), small(dh), small(dv),
    ]
    args += [proj, proj, lq1.reshape(1, dh), lk1.reshape(1, dh), lq2.reshape(1, dh), lk2.reshape(1, dh),
             subln_w.reshape(1, dv)]
    return pl.pallas_call(
        kern,
        grid_spec=pltpu.PrefetchScalarGridSpec(
            num_scalar_prefetch=4,
            grid=(bn, n_heads, n_steps),
            in_specs=in_specs,
            out_specs=pl.BlockSpec((None, tq, dv), lambda b, h, s, qi, kd, kc_i, kn_i: (b, qi[s], h)),
            scratch_shapes=[
                pltpu.VMEM((tq, dv), BF16),
                pltpu.VMEM((tq, dv), BF16),
                pltpu.VMEM((2, tq, 1), F32),
                pltpu.VMEM((2, tq, 1), F32),
                pltpu.VMEM((2, tq, dv), F32),
            ],
        ),
        out_shape=jax.ShapeDtypeStruct((bn, l, n_heads * dv), BF16),
        compiler_params=_cparams("parallel", "parallel", "arbitrary"),
        name="diff_attention",
    )(*tables, *args)


def _ssd_kernel(z_ref, x_ref, bc_ref, dt_ref, cbx_ref, cbbc_ref, s0_ref, cwx_ref, cbiasx_ref, cwbc_ref,
                cbiasbc_ref, dtb_ref, alog_ref, dskip_ref, nw_ref, expand_ref,
                y_ref, slast_ref, xsx_ref, xsbc_ref, s_ref, *, n_grp, hpg, p_dim, n_dim):
    t = pl.program_id(1)
    q = CHUNK
    d_ssd = n_grp * hpg * p_dim
    gw = hpg * p_dim

    @pl.when(t == 0)
    def _():
        xsx_ref[5:8, :] = cbx_ref[...]
        xsbc_ref[5:8, :] = cbbc_ref[...]
        s_ref[...] = s0_ref[...]

    xs = _silu(_conv_tile(xsx_ref, x_ref[...], cwx_ref, cbiasx_ref, q))
    bcs = _silu(_conv_tile(xsbc_ref, bc_ref[...], cwbc_ref, cbiasbc_ref, q))

    dt = _softplus(dt_ref[...] + dtb_ref[...])
    a = -jnp.exp(alog_ref[...])
    acum = _cumsum_rows(dt * a)
    a_last = acum[q - 1:q, :]
    wst = jnp.exp(a_last - acum) * dt
    eac = jnp.exp(acum)
    acum_t = jnp.transpose(acum)

    stacked = jnp.concatenate([dt, wst, eac], axis=0)
    expd = jnp.dot(stacked, expand_ref[...], precision=lax.Precision.HIGHEST, preferred_element_type=F32)
    dt_e, wst_e, eac_e = expd[0:q], expd[q:2 * q], expd[2 * q:3 * q]

    xdt = (xs * dt_e).astype(BF16)
    xw = (xs * wst_e).astype(BF16)
    row = lax.broadcasted_iota(jnp.int32, (q, q), 0)
    col = lax.broadcasted_iota(jnp.int32, (q, q), 1)
    causal = col <= row
    lane = lax.broadcasted_iota(jnp.int32, (q, LANES), 1)
    heads_per_tile = LANES // p_dim

    y_parts = []
    for g in range(n_grp):
        bg = bcs[:, g * n_dim:(g + 1) * n_dim]
        cg = bcs[:, (n_grp + g) * n_dim:(n_grp + g + 1) * n_dim].astype(BF16)
        cb = lax.dot_general(cg, bg.astype(BF16), _NT, preferred_element_type=F32)
        s_prev = s_ref[g]
        y_off = _dot(cg, s_prev.astype(BF16)) * eac_e[:, g * gw:(g + 1) * gw]
        y_diag = []
        for tile in range(gw // LANES):
            c0 = g * gw + tile * LANES
            x_tile = xdt[:, c0:c0 + LANES]
            acc = None
            for hh in range(heads_per_tile):
                h = (c0 // p_dim) + hh
                dec = jnp.exp(jnp.where(causal, acum[:, h:h + 1] - acum_t[h:h + 1, :], -jnp.inf))
                yh = _dot((cb * dec).astype(BF16), x_tile)
                sel = jnp.logical_and(lane >= hh * p_dim, lane < (hh + 1) * p_dim)
                acc = jnp.where(sel, yh, 0.0) if acc is None else jnp.where(sel, yh, acc)
            y_diag.append(acc)
        y_parts.append(jnp.concatenate(y_diag, axis=1) + y_off)
        bg_t = jnp.transpose(bg).astype(BF16)
        st = _dot(bg_t, xw[:, g * gw:(g + 1) * gw])
        s_ref[g] = eac_e[q - 1:q, g * gw:(g + 1) * gw] * s_prev + st
    y = jnp.concatenate(y_parts, axis=1)
    y = (y + dskip_ref[...] * xs) * _silu(z_ref[...])
    nw = nw_ref[...]
    y_ref[...] = jnp.concatenate(
        [_rms(y[:, g * gw:(g + 1) * gw], nw[:, g * gw:(g + 1) * gw]) for g in range(n_grp)],
        axis=1).astype(BF16)
    slast_ref[...] = s_ref[...]


def _ssd(proj, cols, conv_buf, s0, conv_w, conv_b, dt_bias, a_log, d_skip, norm_w, n_grp, n_dim, p_dim):
    bn, l, _ = proj.shape
    n_heads = dt_bias.shape[0]
    hpg = n_heads // n_grp
    d_ssd = n_heads * p_dim
    d_bc = 2 * n_grp * n_dim
    gw = hpg * p_dim
    q = CHUNK
    assert l % q == 0 and n_heads <= LANES and LANES % p_dim == 0
    z_col, x_col, bc_col, dt_col = cols
    pad = lambda v: jnp.pad(v.astype(F32), (0, LANES - n_heads)).reshape(1, LANES)
    expand = (jnp.arange(LANES)[:, None] == (jnp.arange(d_ssd)[None, :] // p_dim)).astype(F32)
    s0_t = s0.reshape(bn, n_grp, hpg, p_dim, n_dim).transpose(0, 1, 4, 2, 3).reshape(bn, n_grp, n_dim, gw)
    const2 = lambda b, t: (0, 0)
    kern = functools.partial(_ssd_kernel, n_grp=n_grp, hpg=hpg, p_dim=p_dim, n_dim=n_dim)
    y, s_last = pl.pallas_call(
        kern,
        grid=(bn, l // q),
        in_specs=[
            pl.BlockSpec((None, q, d_ssd), lambda b, t: (b, t, z_col)),
            pl.BlockSpec((None, q, d_ssd), lambda b, t: (b, t, x_col)),
            pl.BlockSpec((None, q, d_bc), lambda b, t: (b, t, bc_col)),
            pl.BlockSpec((None, q, LANES), lambda b, t: (b, t, dt_col)),
            pl.BlockSpec((None, CONV_W - 1, d_ssd), lambda b, t: (b, 0, 0)),
            pl.BlockSpec((None, CONV_W - 1, d_bc), lambda b, t: (b, 0, 0)),
            pl.BlockSpec((None, n_grp, n_dim, gw), lambda b, t: (b, 0, 0, 0)),
            pl.BlockSpec((CONV_W, d_ssd), const2),
            pl.BlockSpec((1, d_ssd), const2),
            pl.BlockSpec((CONV_W, d_bc), const2),
            pl.BlockSpec((1, d_bc), const2),
            pl.BlockSpec((1, LANES), const2),
            pl.BlockSpec((1, LANES), const2),
            pl.BlockSpec((1, d_ssd), const2),
            pl.BlockSpec((1, d_ssd), const2),
            pl.BlockSpec((LANES, d_ssd), const2),
        ],
        out_specs=[
            pl.BlockSpec((None, q, d_ssd), lambda b, t: (b, t, 0)),
            pl.BlockSpec((None, n_grp, n_dim, gw), lambda b, t: (b, 0, 0, 0)),
        ],
        out_shape=[
            jax.ShapeDtypeStruct((bn, l, d_ssd), BF16),
            jax.ShapeDtypeStruct((bn, n_grp, n_dim, gw), F32),
        ],
        scratch_shapes=[
            pltpu.VMEM((q + 8, d_ssd), F32),
            pltpu.VMEM((q + 8, d_bc), F32),
            pltpu.VMEM((n_grp, n_dim, gw), F32),
        ],
        compiler_params=_cparams("parallel", "arbitrary"),
        name="ssd",
    )(proj, proj, proj, proj, conv_buf[:, :, :d_ssd], conv_buf[:, :, d_ssd:], s0_t,
      conv_w[:, :d_ssd], conv_b[:d_ssd].reshape(1, d_ssd), conv_w[:, d_ssd:], conv_b[d_ssd:].reshape(1, d_bc),
      pad(dt_bias), pad(a_log), jnp.repeat(d_skip.astype(F32), p_dim).reshape(1, d_ssd),
      norm_w.reshape(1, d_ssd), expand)
    s_last = s_last.reshape(bn, n_grp, n_dim, hpg, p_dim).transpose(0, 1, 3, 4, 2).reshape(bn, n_heads, p_dim, n_dim)
    return y, s_last


def _hgrn_kernel(q_ref, f_ref, i_ref, lb_ref, nw_ref, s0_ref, y_ref, slast_ref, s_ref, *, n_heads):
    t = pl.program_id(1)
    c = CHUNK
    nsb = c // HG_SUB

    @pl.when(t == 0)
    def _():
        s_ref[...] = s0_ref[...]

    lbp = lb_ref[...]
    e = jnp.exp(lbp - jnp.max(lbp, axis=0, keepdims=True))
    lb = e[1:2, :] / jnp.sum(e, axis=0, keepdims=True)
    g = lb + (1.0 - lb) * _sigmoid(f_ref[...])
    logg = jnp.log(g)
    kk = 1.0 - g
    qq = _silu(q_ref[...])
    b = _cumsum_rows(logg)
    iv = i_ref[...]

    row = lax.broadcasted_iota(jnp.int32, (c, LANES), 0)
    blk = jnp.right_shift(row, HG_SUB.bit_length() - 1)
    r2 = lax.broadcasted_iota(jnp.int32, (c, c), 0)
    c2 = lax.broadcasted_iota(jnp.int32, (c, c), 1)
    causal = c2 <= r2
    nw = nw_ref[...]

    outs = []
    for h in range(n_heads):
        sl = slice(h * LANES, (h + 1) * LANES)
        bh, qh, kh, ih = b[:, sl], qq[:, sl], kk[:, sl], iv[:, sl]
        qsegs, ksegs = [], []
        for a in range(1, nsb):
            m = bh[a * HG_SUB - 1:a * HG_SUB, :]
            qsegs.append(jnp.where(blk == a, qh * jnp.exp(bh - m), 0.0))
            ksegs.append(jnp.where(blk < a, kh * jnp.exp(m - bh), 0.0))
        for d in range(nsb):
            m = bh[d * HG_SUB + HG_SUB // 2 - 1:d * HG_SUB + HG_SUB // 2, :]
            qsegs.append(jnp.where(blk == d, qh * jnp.exp(bh - m), 0.0))
            ksegs.append(jnp.where(blk == d, kh * jnp.exp(m - bh), 0.0))
        qcat = jnp.concatenate(qsegs, axis=1).astype(BF16)
        kcat = jnp.concatenate(ksegs, axis=1).astype(BF16)
        att = lax.dot_general(qcat, kcat, _NT, preferred_element_type=F32)
        att = jnp.where(causal, att, 0.0)
        ihb = ih.astype(BF16)
        s_t = s_ref[h]
        o = _dot(att.astype(BF16), ihb)
        o = o + lax.dot_general((qh * jnp.exp(bh)).astype(BF16), s_t.astype(BF16), _NT,
                                preferred_element_type=F32)
        b_last = bh[c - 1:c, :]
        kd = (kh * jnp.exp(b_last - bh)).astype(BF16)
        s_ref[h] = jnp.exp(b_last) * s_t + _dot(jnp.transpose(ih).astype(BF16), kd)
        outs.append(_rms(o, nw))
    y_ref[...] = jnp.concatenate(outs, axis=1).astype(BF16)
    slast_ref[...] = s_ref[...]


def _hgrn(proj, cols, s0, lb_param, norm_w, n_heads):
    bn, l, _ = proj.shape
    d = n_heads * LANES
    c = CHUNK
    assert l % c == 0
    q_col, f_col, i_col = cols
    depth = lb_param.shape[0]
    assert depth == 2
    s0_t = jnp.swapaxes(s0, -1, -2)
    kern = functools.partial(_hgrn_kernel, n_heads=n_heads)
    y, s_last = pl.pallas_call(
        kern,
        grid=(bn, l // c),
        in_specs=[
            pl.BlockSpec((None, c, d), lambda b, t: (b, t, q_col)),
            pl.BlockSpec((None, c, d), lambda b, t: (b, t, f_col)),
            pl.BlockSpec((None, c, d), lambda b, t: (b, t, i_col)),
            pl.BlockSpec((depth, d), lambda b, t: (0, 0)),
            pl.BlockSpec((1, LANES), lambda b, t: (0, 0)),
            pl.BlockSpec((None, n_heads, LANES, LANES), lambda b, t: (b, 0, 0, 0)),
        ],
        out_specs=[
            pl.BlockSpec((None, c, d), lambda b, t: (b, t, 0)),
            pl.BlockSpec((None, n_heads, LANES, LANES), lambda b, t: (b, 0, 0, 0)),
        ],
        out_shape=[
            jax.ShapeDtypeStruct((bn, l, d), BF16),
            jax.ShapeDtypeStruct((bn, n_heads, LANES, LANES), F32),
        ],
        scratch_shapes=[pltpu.VMEM((n_heads, LANES, LANES), F32)],
        compiler_params=_cparams("parallel", "arbitrary"),
        name="hgrn2",
    )(proj, proj, proj, lb_param, norm_w.reshape(1, LANES), s0_t)
    return y, jnp.swapaxes(s_last, -1, -2)


def kernel(x_prompt, x_sample, cache_diff_k, cache_diff_v, state_rglru_conv, state_rglru_h, state_ssd_conv,
           state_ssd, state_hgrn, norm_w, l0_w_in, l0_conv_w, l0_conv_b, l0_rg_w_r, l0_rg_b_r, l0_rg_w_i,
           l0_rg_b_i, l0_rg_lambda, l0_lq1, l0_lk1, l0_lq2, l0_lk2, l0_subln_w, l0_w_out, l1_w_in, l1_conv_w,
           l1_conv_b, l1_dt_bias, l1_a_log, l1_d_skip, l1_ssd_norm_w, l1_hg_lower_bound, l1_hg_norm_w, l1_w_out,
           ffn_w_gate, ffn_w_up, ffn_w_down):
    d_model = x_prompt.shape[-1]
    d_rnn = l0_conv_w.shape[1]
    n_heads_b, dv_b = cache_diff_k.shape[2], cache_diff_k.shape[3]
    d_att = n_heads_b * dv_b
    n_heads_c, p_c, n_c = state_ssd.shape[1:]
    d_ssd = n_heads_c * p_c
    d_xbc = l1_conv_w.shape[1]
    d_bc = d_xbc - d_ssd
    n_grp = d_bc // (2 * n_c)
    n_heads_d, dk_d, dv_d = state_hgrn.shape[1:]
    d_hg = n_heads_d * dk_d
    assert dv_b == LANES and dk_d == LANES and dv_d == LANES and d_rnn == d_att == d_ssd == d_hg

    w_in0 = l0_w_in.astype(BF16)
    o1 = d_ssd
    o2 = o1 + d_xbc
    o3 = o2 + n_heads_c
    w1 = l1_w_in
    w_in1 = jnp.concatenate(
        [w1[:, :o1], w1[:, o1:o1 + d_ssd], w1[:, o3:], w1[:, o1 + d_ssd:o2],
         jnp.pad(w1[:, o2:o3], ((0, 0), (0, LANES - n_heads_c)))], axis=1).astype(BF16)
    w_out0 = l0_w_out.astype(BF16)
    w_out1 = l1_w_out.astype(BF16)
    wg = ffn_w_gate.astype(BF16)
    wu = ffn_w_up.astype(BF16)
    wd = ffn_w_down.astype(BF16)
    w_r = l0_rg_w_r.astype(BF16)
    w_i = l0_rg_w_i.astype(BF16)
    tn0 = 1024
    tn1 = 1152
    assert w_in0.shape[1] % tn0 == 0 and w_in1.shape[1] % tn1 == 0
    ssd_cols = (0, 1, (2 * d_ssd + 3 * d_hg) // d_bc, (2 * d_ssd + 3 * d_hg + d_bc) // LANES)
    hg_cols = (2, 3, 4)

    def trunk(x, k_cache, v_cache, rg_conv, rg_h, ssd_conv, ssd_s, hg_s):
        bn, l, _ = x.shape
        m = bn * l
        x0 = x.reshape(m, d_model)
        proj0 = _norm_matmul(x0, norm_w[0, 0], w_in0, tn0).reshape(bn, l, -1)
        ya, rg_h_new = _rglru(proj0, rg_conv, rg_h, l0_conv_w, l0_conv_b, w_r, l0_rg_b_r, w_i, l0_rg_b_i,
                              l0_rg_lambda, d_rnn)
        qb = 2 * d_rnn // LANES
        yb = _diff_attention(proj0, qb, qb + n_heads_b, qb + 2 * n_heads_b, k_cache, v_cache,
                             l0_lq1, l0_lk1, l0_lq2, l0_lk2, l0_subln_w, n_heads_b, 0)
        k_new = proj0[:, :, 2 * d_rnn + d_att:2 * d_rnn + 2 * d_att].reshape(bn, l, n_heads_b, dv_b)
        v_new = proj0[:, :, 2 * d_rnn + 2 * d_att:].reshape(bn, l, n_heads_b, dv_b)
        rg_conv_new = proj0[:, l - (CONV_W - 1):, :d_rnn]
        x1 = _out_proj(ya.reshape(m, d_rnn), yb.reshape(m, d_att), w_out0, x0, norm_w[0, 1])
        x2 = _ffn(x1, norm_w[0, 2], norm_w[0, 3], wg[0], wu[0], wd[0])
        proj1 = _norm_matmul(x2, norm_w[1, 0], w_in1, tn1).reshape(bn, l, -1)
        ys, ssd_s_new = _ssd(proj1, ssd_cols, ssd_conv, ssd_s, l1_conv_w, l1_conv_b, l1_dt_bias, l1_a_log,
                             l1_d_skip, l1_ssd_norm_w, n_grp, n_c, p_c)
        yh, hg_s_new = _hgrn(proj1, hg_cols, hg_s, l1_hg_lower_bound, l1_hg_norm_w, n_heads_d)
        tail = proj1[:, l - (CONV_W - 1):, :]
        ssd_conv_new = jnp.concatenate(
            [tail[:, :, d_ssd:2 * d_ssd], tail[:, :, 2 * d_ssd + 3 * d_hg:2 * d_ssd + 3 * d_hg + d_bc]], axis=-1)
        x3 = _out_proj(ys.reshape(m, d_ssd), yh.reshape(m, d_hg), w_out1, x2, norm_w[1, 1])
        x4 = _ffn(x3, norm_w[1, 2], norm_w[1, 3], wg[1], wu[1], wd[1])
        return (x4.reshape(bn, l, d_model), k_new, v_new, rg_conv_new, rg_h_new, ssd_conv_new, ssd_s_new, hg_s_new)

    bp = x_prompt.shape[0]
    zeros = lambda *s: jnp.zeros(s, F32)
    outs_p = trunk(x_prompt, None, None, zeros(bp, CONV_W - 1, d_rnn), zeros(bp, d_rnn),
                   zeros(bp, CONV_W - 1, d_xbc), zeros(bp, n_heads_c, p_c, n_c), zeros(bp, n_heads_d, dk_d, dv_d))
    outs_s = trunk(x_sample, cache_diff_k, cache_diff_v, state_rglru_conv, state_rglru_h, state_ssd_conv,
                   state_ssd, state_hgrn)
    return (outs_p[0], outs_s[0]) + tuple(outs_p[1:]) + tuple(outs_s[1:])
```

```python
import functools
import math

import jax
import jax.numpy as jnp
from jax import lax
from jax.experimental import pallas as pl
from jax.experimental.pallas import tpu as pltpu

F32 = jnp.float32
BF16 = jnp.bfloat16

EPS = 1e-6
CHUNK = 64
CONV_W = 4
RG_C = 8.0
LANES = 128
SUBLANES = 8
HG_SUB = 16
NEG = -1e30
VMEM_LIMIT_BYTES = 56 * 1024 * 1024

_NT = (((1,), (1,)), ((), ()))
_TN = (((0,), (0,)), ((), ()))


def _cparams(*sem):
    return pltpu.CompilerParams(dimension_semantics=sem, vmem_limit_bytes=VMEM_LIMIT_BYTES)


def _sigmoid(x):
    return 1.0 / (1.0 + jnp.exp(-x))


def _silu(x):
    return x * _sigmoid(x)


def _softplus(x):
    return jnp.maximum(x, 0.0) + jnp.log1p(jnp.exp(-jnp.abs(x)))


def _neg_expm1(x):
    return -jnp.tanh(0.5 * x) * (jnp.exp(x) + 1.0)


def _gelu_tanh(x):
    c = math.sqrt(2.0 / math.pi)
    return x * (0.5 * (1.0 + jnp.tanh(c * (x + 0.044715 * (x * x * x)))))


def _rms(x, w):
    return x * lax.rsqrt(jnp.mean(x * x, axis=-1, keepdims=True) + EPS) * w


def _dot(a, b):
    return jnp.dot(a, b, preferred_element_type=F32)


def _cumsum_rows(x):
    n = x.shape[0]
    row = lax.broadcasted_iota(jnp.int32, x.shape, 0)
    s = 1
    while s < n:
        x = x + jnp.where(row >= s, pltpu.roll(x, s, axis=0), 0.0)
        s *= 2
    return x


def _norm_matmul_kernel(x_ref, nw_ref, w_ref, *rest, bounds):
    o_refs, xn_ref = rest[:-1], rest[-1]
    j = pl.program_id(1)

    @pl.when(j == 0)
    def _():
        xn_ref[...] = _rms(x_ref[...], nw_ref[...]).astype(BF16)

    for k, o_ref in enumerate(o_refs):
        @pl.when(jnp.logical_and(j >= bounds[k], j < bounds[k + 1]))
        def _(o_ref=o_ref):
            o_ref[...] = _dot(xn_ref[...], w_ref[...])


def _norm_matmul(x, nw, w, tn, splits=None):
    m, d = x.shape
    n = w.shape[1]
    splits = [n] if splits is None else list(splits)
    assert sum(splits) == n and all(s % tn == 0 for s in splits)
    bounds = [0]
    for s in splits:
        bounds.append(bounds[-1] + s // tn)
    tm = min(512, m)

    def out_map(k):
        lo, hi = bounds[k], bounds[k + 1]
        return lambda i, j: (i, jnp.clip(j, lo, hi - 1) - lo)

    outs = pl.pallas_call(
        functools.partial(_norm_matmul_kernel, bounds=tuple(bounds)),
        grid=(m // tm, n // tn),
        in_specs=[
            pl.BlockSpec((tm, d), lambda i, j: (i, 0)),
            pl.BlockSpec((1, d), lambda i, j: (0, 0)),
            pl.BlockSpec((d, tn), lambda i, j: (0, j)),
        ],
        out_specs=[pl.BlockSpec((tm, tn), out_map(k)) for k in range(len(splits))],
        out_shape=[jax.ShapeDtypeStruct((m, s), F32) for s in splits],
        scratch_shapes=[pltpu.VMEM((tm, d), BF16)],
        compiler_params=_cparams("parallel", "arbitrary"),
        name="norm_in_proj",
    )(x, nw.reshape(1, d), w)
    return outs


def _out_proj_kernel(ya_ref, yb_ref, wa_ref, wb_ref, x_ref, nw_ref, o_ref):
    m = _dot(ya_ref[...], wa_ref[...]) + _dot(yb_ref[...], wb_ref[...])
    o_ref[...] = x_ref[...] + _rms(m, nw_ref[...])


def _out_proj(ya, yb, w_out, x, nw):
    m, d = x.shape
    da, db = ya.shape[1], yb.shape[1]
    tm = min(512, m)
    return pl.pallas_call(
        _out_proj_kernel,
        grid=(m // tm,),
        in_specs=[
            pl.BlockSpec((tm, da), lambda i: (i, 0)),
            pl.BlockSpec((tm, db), lambda i: (i, 0)),
            pl.BlockSpec((da, d), lambda i: (0, 0)),
            pl.BlockSpec((db, d), lambda i: (0, 0)),
            pl.BlockSpec((tm, d), lambda i: (i, 0)),
            pl.BlockSpec((1, d), lambda i: (0, 0)),
        ],
        out_specs=pl.BlockSpec((tm, d), lambda i: (i, 0)),
        out_shape=jax.ShapeDtypeStruct((m, d), F32),
        compiler_params=_cparams("parallel"),
        name="out_proj",
    )(ya, yb, w_out[:da], w_out[da:], x, nw.reshape(1, d))


def _ffn_kernel(x_ref, nwa_ref, nwb_ref, wg_ref, wu_ref, wd_ref, o_ref, hn_ref, acc_ref):
    f = pl.program_id(1)

    @pl.when(f == 0)
    def _():
        hn_ref[...] = _rms(x_ref[...], nwa_ref[...]).astype(BF16)
        acc_ref[...] = jnp.zeros_like(acc_ref)

    hn = hn_ref[...]
    g = _dot(hn, wg_ref[...])
    u = _dot(hn, wu_ref[...])
    a = (_silu(g) * u).astype(BF16)
    acc_ref[...] += _dot(a, wd_ref[...])

    @pl.when(f == pl.num_programs(1) - 1)
    def _():
        o_ref[...] = x_ref[...] + _rms(acc_ref[...], nwb_ref[...])


def _ffn(x, nw_pre, nw_post, wg, wu, wd):
    m, d = x.shape
    dff = wg.shape[1]
    tm = min(512, m)
    tf = 512
    return pl.pallas_call(
        _ffn_kernel,
        grid=(m // tm, dff // tf),
        in_specs=[
            pl.BlockSpec((tm, d), lambda i, f: (i, 0)),
            pl.BlockSpec((1, d), lambda i, f: (0, 0)),
            pl.BlockSpec((1, d), lambda i, f: (0, 0)),
            pl.BlockSpec((d, tf), lambda i, f: (0, f)),
            pl.BlockSpec((d, tf), lambda i, f: (0, f)),
            pl.BlockSpec((tf, d), lambda i, f: (f, 0)),
        ],
        out_specs=pl.BlockSpec((tm, d), lambda i, f: (i, 0)),
        out_shape=jax.ShapeDtypeStruct((m, d), F32),
        scratch_shapes=[pltpu.VMEM((tm, d), BF16), pltpu.VMEM((tm, d), F32)],
        compiler_params=_cparams("parallel", "arbitrary"),
        name="ffn",
    )(x, nw_pre.reshape(1, d), nw_post.reshape(1, d), wg, wu, wd)


def _conv_tile(xs_ref, x_tile, cw_ref, cb_ref, tl):
    xs_ref[8:8 + tl, :] = x_tile
    y = cb_ref[...]
    for tap in range(CONV_W):
        y = y + xs_ref[5 + tap:5 + tap + tl, :] * cw_ref[tap:tap + 1, :]
    xs_ref[5:8, :] = xs_ref[tl + 5:tl + 8, :]
    return y


def _rglru_kernel(xa_ref, ga_ref, cbuf_ref, h0_ref, cw_ref, cb_ref, wr_ref, br_ref, wi_ref, bi_ref,
                  lam_ref, ya_ref, hlast_ref, xs_ref, a_ref, u_ref, h_ref, *, tl, n_blk):
    t = pl.program_id(1)

    @pl.when(t == 0)
    def _():
        xs_ref[5:8, :] = cbuf_ref[...]
        h_ref[...] = h0_ref[...]

    xc = _conv_tile(xs_ref, xa_ref[...], cw_ref, cb_ref, tl)
    xcb = xc.astype(BF16)
    r_pre = jnp.concatenate(
        [_dot(xcb[:, hb * LANES:(hb + 1) * LANES], wr_ref[hb]) for hb in range(n_blk)], axis=1)
    i_pre = jnp.concatenate(
        [_dot(xcb[:, hb * LANES:(hb + 1) * LANES], wi_ref[hb]) for hb in range(n_blk)], axis=1)
    r = _sigmoid(r_pre + br_ref[...])
    gi = _sigmoid(i_pre + bi_ref[...])
    log_a = -RG_C * r * _softplus(-lam_ref[...])
    a_ref[...] = jnp.exp(log_a)
    u_ref[...] = jnp.sqrt(_neg_expm1(2.0 * log_a)) * (gi * xc)

    row = lax.broadcasted_iota(jnp.int32, (SUBLANES, a_ref.shape[1]), 0)

    def group(gidx, h_prev):
        r0 = pl.multiple_of(gidx * SUBLANES, SUBLANES)
        ag = a_ref[pl.ds(r0, SUBLANES), :]
        ug = u_ref[pl.ds(r0, SUBLANES), :]
        s = 1
        while s < SUBLANES:
            a_sh = jnp.where(row >= s, pltpu.roll(ag, s, axis=0), 1.0)
            u_sh = jnp.where(row >= s, pltpu.roll(ug, s, axis=0), 0.0)
            ug = ag * u_sh + ug
            ag = ag * a_sh
            s *= 2
        hg = ag * h_prev + ug
        u_ref[pl.ds(r0, SUBLANES), :] = hg
        return hg[SUBLANES - 1:SUBLANES, :]

    h_last = lax.fori_loop(0, tl // SUBLANES, group, h_ref[...])
    h_ref[...] = h_last
    hlast_ref[...] = h_last
    ya_ref[...] = (u_ref[...] * _gelu_tanh(ga_ref[...])).astype(BF16)


def _rglru(proj, conv_buf, h0, conv_w, conv_b, w_r, b_r, w_i, b_i, lam, d_rnn):
    bn, l, _ = proj.shape
    tl = min(256, l)
    n_blk = w_r.shape[0]
    vec = lambda v: v.reshape(1, d_rnn)
    const2 = lambda b, t: (0, 0)
    kern = functools.partial(_rglru_kernel, tl=tl, n_blk=n_blk)
    ya, h_last = pl.pallas_call(
        kern,
        grid=(bn, l // tl),
        in_specs=[
            pl.BlockSpec((None, tl, d_rnn), lambda b, t: (b, t, 0)),
            pl.BlockSpec((None, tl, d_rnn), lambda b, t: (b, t, 1)),
            pl.BlockSpec((None, CONV_W - 1, d_rnn), lambda b, t: (b, 0, 0)),
            pl.BlockSpec((None, 1, d_rnn), lambda b, t: (b, 0, 0)),
            pl.BlockSpec((CONV_W, d_rnn), const2),
            pl.BlockSpec((1, d_rnn), const2),
            pl.BlockSpec(w_r.shape, lambda b, t: (0, 0, 0)),
            pl.BlockSpec((1, d_rnn), const2),
            pl.BlockSpec(w_i.shape, lambda b, t: (0, 0, 0)),
            pl.BlockSpec((1, d_rnn), const2),
            pl.BlockSpec((1, d_rnn), const2),
        ],
        out_specs=[
            pl.BlockSpec((None, tl, d_rnn), lambda b, t: (b, t, 0)),
            pl.BlockSpec((None, 1, d_rnn), lambda b, t: (b, 0, 0)),
        ],
        out_shape=[
            jax.ShapeDtypeStruct((bn, l, d_rnn), BF16),
            jax.ShapeDtypeStruct((bn, 1, d_rnn), F32),
        ],
        scratch_shapes=[
            pltpu.VMEM((tl + 8, d_rnn), F32),
            pltpu.VMEM((tl, d_rnn), F32),
            pltpu.VMEM((tl, d_rnn), F32),
            pltpu.VMEM((1, d_rnn), F32),
        ],
        compiler_params=_cparams("parallel", "arbitrary"),
        name="rglru",
    )(proj, proj, conv_buf, h0.reshape(bn, 1, d_rnn), conv_w, vec(conv_b), w_r, vec(b_r), w_i, vec(b_i),
      vec(lam))
    return ya, h_last.reshape(bn, d_rnn)


def _chunk_mask(kpos, qpos):
    shift = CHUNK.bit_length() - 1
    return jnp.right_shift(kpos, shift) <= jnp.right_shift(qpos, shift)


def _diff_lambda(lq1_ref, lk1_ref, lq2_ref, lk2_ref, lam_init):
    return (jnp.exp(jnp.sum(lq1_ref[...] * lk1_ref[...], axis=-1, keepdims=True))
            - jnp.exp(jnp.sum(lq2_ref[...] * lk2_ref[...], axis=-1, keepdims=True)) + lam_init)


def _attn_prompt_kernel(q_ref, k_ref, v_ref, lq1_ref, lk1_ref, lq2_ref, lk2_ref, sw_ref, o_ref,
                        kb_ref, vt_ref, m_ref, l_ref, acc_ref, *, t, nblk, dh, scale, lam_init):
    i = pl.program_id(2)

    @pl.when(i == 0)
    def _():
        for jb in range(nblk):
            kb_ref[jb] = k_ref[jb * t:(jb + 1) * t, :].astype(BF16)
            vt_ref[jb] = jnp.transpose(v_ref[jb * t:(jb + 1) * t, :]).astype(BF16)

    qt = jnp.transpose(q_ref[...] * scale)
    sub = lax.broadcasted_iota(jnp.int32, qt.shape, 0)
    qt2 = jnp.concatenate([jnp.where(sub < dh, qt, 0.0), jnp.where(sub >= dh, qt, 0.0)],
                          axis=1).astype(BF16)
    m_ref[...] = jnp.full_like(m_ref, NEG)
    l_ref[...] = jnp.zeros_like(l_ref)
    acc_ref[...] = jnp.zeros_like(acc_ref)

    def step(j, mask):
        s = _dot(kb_ref[j], qt2)
        if mask is not None:
            s = jnp.where(mask, s, NEG)
        m_prev = m_ref[...]
        m_new = jnp.maximum(m_prev, jnp.max(s, axis=0, keepdims=True))
        alpha = jnp.exp(m_prev - m_new)
        p = jnp.exp(s - m_new)
        l_ref[...] = alpha * l_ref[...] + jnp.sum(p, axis=0, keepdims=True)
        acc_ref[...] = alpha * acc_ref[...] + _dot(vt_ref[j], p.astype(BF16))
        m_ref[...] = m_new

    def body(j, carry):
        step(j, None)
        return carry

    lax.fori_loop(0, i, body, 0)
    base = i * t
    kpos = base + lax.broadcasted_iota(jnp.int32, (t, t), 0)
    qpos = base + lax.broadcasted_iota(jnp.int32, (t, t), 1)
    mask = _chunk_mask(kpos, qpos)
    step(i, jnp.concatenate([mask, mask], axis=1))

    lam = _diff_lambda(lq1_ref, lk1_ref, lq2_ref, lk2_ref, lam_init)
    acc = acc_ref[...]
    l = l_ref[...]
    o_t = acc[:, :t] / l[:, :t] - lam * (acc[:, t:] / l[:, t:])
    o_ref[...] = (_rms(jnp.transpose(o_t), sw_ref[...]) * (1.0 - lam_init)).astype(BF16)


def _attn_cached_kernel(q_ref, kc_ref, vc_ref, kn_ref, vn_ref, lq1_ref, lk1_ref, lq2_ref, lk2_ref, sw_ref,
                        o_ref, q2_ref, m_ref, l_ref, acc_ref, *, n_heads, lq, n_cache, pos0, dh, scale,
                        lam_init):
    j = pl.program_id(1)
    hsl = lambda h: slice(h * LANES, (h + 1) * LANES)

    @pl.when(j == 0)
    def _():
        lane = lax.broadcasted_iota(jnp.int32, (lq, LANES), 1)
        for h in range(n_heads):
            q = q_ref[:, hsl(h)] * scale
            q2_ref[h] = jnp.concatenate([jnp.where(lane < dh, q, 0.0), jnp.where(lane >= dh, q, 0.0)],
                                        axis=0).astype(BF16)
        m_ref[...] = jnp.full_like(m_ref, NEG)
        l_ref[...] = jnp.zeros_like(l_ref)
        acc_ref[...] = jnp.zeros_like(acc_ref)

    def update(k_ref, v_ref, mask):
        for h in range(n_heads):
            kb = k_ref[:, hsl(h)].astype(BF16)
            vb = v_ref[:, hsl(h)].astype(BF16)
            s = lax.dot_general(q2_ref[h], kb, _NT, preferred_element_type=F32)
            if mask is not None:
                s = jnp.where(mask, s, NEG)
            m_prev = m_ref[h]
            m_new = jnp.maximum(m_prev, jnp.max(s, axis=-1, keepdims=True))
            alpha = jnp.exp(m_prev - m_new)
            p = jnp.exp(s - m_new)
            l_ref[h] = alpha * l_ref[h] + jnp.sum(p, axis=-1, keepdims=True)
            acc_ref[h] = alpha * acc_ref[h] + _dot(p.astype(BF16), vb)
            m_ref[h] = m_new

    @pl.when(j < n_cache)
    def _():
        update(kc_ref, vc_ref, None)

    @pl.when(j == n_cache)
    def _():
        row = lax.broadcasted_iota(jnp.int32, (2 * lq, lq), 0)
        qpos = pos0 + jnp.where(row >= lq, row - lq, row)
        kpos = pos0 + lax.broadcasted_iota(jnp.int32, (2 * lq, lq), 1)
        update(kn_ref, vn_ref, _chunk_mask(kpos, qpos))
        lam = _diff_lambda(lq1_ref, lk1_ref, lq2_ref, lk2_ref, lam_init)
        outs = []
        for h in range(n_heads):
            acc = acc_ref[h]
            l = l_ref[h]
            o = acc[:lq] / l[:lq] - lam * (acc[lq:] / l[lq:])
            outs.append(_rms(o, sw_ref[...]) * (1.0 - lam_init))
        o_ref[...] = jnp.concatenate(outs, axis=1).astype(BF16)


def _diff_attention(qsrc, q_col, k, v, k_cache, v_cache, lq1, lk1, lq2, lk2, subln_w, n_heads, layer):
    bn, l, _ = k.shape
    dv = LANES
    dh = dv // 2
    d_att = n_heads * dv
    lam_init = 0.8 - 0.6 * math.exp(-0.3 * layer)
    scale = dh ** -0.5
    params = [lq1.reshape(1, dh), lk1.reshape(1, dh), lq2.reshape(1, dh), lk2.reshape(1, dh),
              subln_w.reshape(1, dv)]
    if k_cache is None:
        t = min(512, l)
        nblk = l // t
        assert t % CHUNK == 0 and l % t == 0
        small = lambda n: pl.BlockSpec((1, n), lambda b, h, i: (0, 0))
        kern = functools.partial(_attn_prompt_kernel, t=t, nblk=nblk, dh=dh, scale=scale, lam_init=lam_init)
        return pl.pallas_call(
            kern,
            grid=(bn, n_heads, nblk),
            in_specs=[
                pl.BlockSpec((None, t, dv), lambda b, h, i: (b, i, q_col + h)),
                pl.BlockSpec((None, l, dv), lambda b, h, i: (b, 0, h)),
                pl.BlockSpec((None, l, dv), lambda b, h, i: (b, 0, h)),
                small(dh), small(dh), small(dh), small(dh), small(dv),
            ],
            out_specs=pl.BlockSpec((None, t, dv), lambda b, h, i: (b, i, h)),
            out_shape=jax.ShapeDtypeStruct((bn, l, d_att), BF16),
            scratch_shapes=[
                pltpu.VMEM((nblk, t, dv), BF16),
                pltpu.VMEM((nblk, dv, t), BF16),
                pltpu.VMEM((1, 2 * t), F32),
                pltpu.VMEM((1, 2 * t), F32),
                pltpu.VMEM((dv, 2 * t), F32),
            ],
            compiler_params=_cparams("parallel", "parallel", "arbitrary"),
            name="diff_attention_prompt",
        )(qsrc, k, v, *params)

    pos0 = k_cache.shape[1]
    tkc = min(512, pos0)
    assert pos0 % CHUNK == 0 and pos0 % tkc == 0 and l % SUBLANES == 0
    n_cache = pos0 // tkc
    kc = k_cache.reshape(bn, pos0, d_att)
    vc = v_cache.reshape(bn, pos0, d_att)
    small = lambda n: pl.BlockSpec((1, n), lambda b, j: (0, 0))
    kern = functools.partial(_attn_cached_kernel, n_heads=n_heads, lq=l, n_cache=n_cache, pos0=pos0, dh=dh,
                             scale=scale, lam_init=lam_init)
    cache_spec = pl.BlockSpec((None, tkc, d_att), lambda b, j: (b, jnp.minimum(j, n_cache - 1), 0))
    new_spec = pl.BlockSpec((None, l, d_att), lambda b, j: (b, 0, 0))
    return pl.pallas_call(
        kern,
        grid=(bn, n_cache + 1),
        in_specs=[
            pl.BlockSpec((None, l, d_att), lambda b, j: (b, 0, q_col * dv // d_att)),
            cache_spec, cache_spec, new_spec, new_spec,
            small(dh), small(dh), small(dh), small(dh), small(dv),
        ],
        out_specs=pl.BlockSpec((None, l, d_att), lambda b, j: (b, 0, 0)),
        out_shape=jax.ShapeDtypeStruct((bn, l, d_att), BF16),
        scratch_shapes=[
            pltpu.VMEM((n_heads, 2 * l, dv), BF16),
            pltpu.VMEM((n_heads, 2 * l, 1), F32),
            pltpu.VMEM((n_heads, 2 * l, 1), F32),
            pltpu.VMEM((n_heads, 2 * l, dv), F32),
        ],
        compiler_params=_cparams("parallel", "arbitrary"),
        name="diff_attention_cached",
    )(qsrc, kc, vc, k, v, *params)


def _ssd_kernel(z_ref, x_ref, bc_ref, dt_ref, cbx_ref, cbbc_ref, s0_ref, cwx_ref, cbiasx_ref, cwbc_ref,
                cbiasbc_ref, dtb_ref, alog_ref, dskip_ref, nw_ref, expand_ref,
                y_ref, slast_ref, xsx_ref, xsbc_ref, s_ref, *, n_grp, hpg, p_dim, n_dim):
    t = pl.program_id(1)
    q = CHUNK
    d_ssd = n_grp * hpg * p_dim
    gw = hpg * p_dim

    @pl.when(t == 0)
    def _():
        xsx_ref[5:8, :] = cbx_ref[...]
        xsbc_ref[5:8, :] = cbbc_ref[...]
        s_ref[...] = s0_ref[...]

    xs = _silu(_conv_tile(xsx_ref, x_ref[...], cwx_ref, cbiasx_ref, q))
    bcs = _silu(_conv_tile(xsbc_ref, bc_ref[...], cwbc_ref, cbiasbc_ref, q))

    dt = _softplus(dt_ref[...] + dtb_ref[...])
    a = -jnp.exp(alog_ref[...])
    acum = _cumsum_rows(dt * a)
    a_last = acum[q - 1:q, :]
    wst = jnp.exp(a_last - acum) * dt
    eac = jnp.exp(acum)
    acum_t = jnp.transpose(acum)

    stacked = jnp.concatenate([dt, wst, eac], axis=0)
    expd = jnp.dot(stacked, expand_ref[...], precision=lax.Precision.HIGHEST, preferred_element_type=F32)
    dt_e, wst_e, eac_e = expd[0:q], expd[q:2 * q], expd[2 * q:3 * q]

    xdt = (xs * dt_e).astype(BF16)
    xw = (xs * wst_e).astype(BF16)
    row = lax.broadcasted_iota(jnp.int32, (q, q), 0)
    col = lax.broadcasted_iota(jnp.int32, (q, q), 1)
    causal = col <= row
    lane = lax.broadcasted_iota(jnp.int32, (q, LANES), 1)
    heads_per_tile = LANES // p_dim

    y_parts = []
    for g in range(n_grp):
        bg = bcs[:, g * n_dim:(g + 1) * n_dim]
        cg = bcs[:, (n_grp + g) * n_dim:(n_grp + g + 1) * n_dim].astype(BF16)
        cb = lax.dot_general(cg, bg.astype(BF16), _NT, preferred_element_type=F32)
        s_prev = s_ref[g]
        y_off = _dot(cg, s_prev.astype(BF16)) * eac_e[:, g * gw:(g + 1) * gw]
        y_diag = []
        for tile in range(gw // LANES):
            c0 = g * gw + tile * LANES
            x_tile = xdt[:, c0:c0 + LANES]
            acc = None
            for hh in range(heads_per_tile):
                h = (c0 // p_dim) + hh
                dec = jnp.exp(jnp.where(causal, acum[:, h:h + 1] - acum_t[h:h + 1, :], -jnp.inf))
                yh = _dot((cb * dec).astype(BF16), x_tile)
                sel = jnp.logical_and(lane >= hh * p_dim, lane < (hh + 1) * p_dim)
                acc = jnp.where(sel, yh, 0.0) if acc is None else jnp.where(sel, yh, acc)
            y_diag.append(acc)
        y_parts.append(jnp.concatenate(y_diag, axis=1) + y_off)
        bg_t = jnp.transpose(bg).astype(BF16)
        st = _dot(bg_t, xw[:, g * gw:(g + 1) * gw])
        s_ref[g] = eac_e[q - 1:q, g * gw:(g + 1) * gw] * s_prev + st
    y = jnp.concatenate(y_parts, axis=1)
    y = (y + dskip_ref[...] * xs) * _silu(z_ref[...])
    nw = nw_ref[...]
    y_ref[...] = jnp.concatenate(
        [_rms(y[:, g * gw:(g + 1) * gw], nw[:, g * gw:(g + 1) * gw]) for g in range(n_grp)],
        axis=1).astype(BF16)
    slast_ref[...] = s_ref[...]


def _ssd(proj, cols, conv_buf, s0, conv_w, conv_b, dt_bias, a_log, d_skip, norm_w, n_grp, n_dim, p_dim):
    bn, l, _ = proj.shape
    n_heads = dt_bias.shape[0]
    hpg = n_heads // n_grp
    d_ssd = n_heads * p_dim
    d_bc = 2 * n_grp * n_dim
    gw = hpg * p_dim
    q = CHUNK
    assert l % q == 0 and n_heads <= LANES and LANES % p_dim == 0
    z_col, x_col, bc_col, dt_col = cols
    pad = lambda v: jnp.pad(v.astype(F32), (0, LANES - n_heads)).reshape(1, LANES)
    expand = (jnp.arange(LANES)[:, None] == (jnp.arange(d_ssd)[None, :] // p_dim)).astype(F32)
    s0_t = s0.reshape(bn, n_grp, hpg, p_dim, n_dim).transpose(0, 1, 4, 2, 3).reshape(bn, n_grp, n_dim, gw)
    const2 = lambda b, t: (0, 0)
    kern = functools.partial(_ssd_kernel, n_grp=n_grp, hpg=hpg, p_dim=p_dim, n_dim=n_dim)
    y, s_last = pl.pallas_call(
        kern,
        grid=(bn, l // q),
        in_specs=[
            pl.BlockSpec((None, q, d_ssd), lambda b, t: (b, t, z_col)),
            pl.BlockSpec((None, q, d_ssd), lambda b, t: (b, t, x_col)),
            pl.BlockSpec((None, q, d_bc), lambda b, t: (b, t, bc_col)),
            pl.BlockSpec((None, q, LANES), lambda b, t: (b, t, dt_col)),
            pl.BlockSpec((None, CONV_W - 1, d_ssd), lambda b, t: (b, 0, 0)),
            pl.BlockSpec((None, CONV_W - 1, d_bc), lambda b, t: (b, 0, 0)),
            pl.BlockSpec((None, n_grp, n_dim, gw), lambda b, t: (b, 0, 0, 0)),
            pl.BlockSpec((CONV_W, d_ssd), const2),
            pl.BlockSpec((1, d_ssd), const2),
            pl.BlockSpec((CONV_W, d_bc), const2),
            pl.BlockSpec((1, d_bc), const2),
            pl.BlockSpec((1, LANES), const2),
            pl.BlockSpec((1, LANES), const2),
            pl.BlockSpec((1, d_ssd), const2),
            pl.BlockSpec((1, d_ssd), const2),
            pl.BlockSpec((LANES, d_ssd), const2),
        ],
        out_specs=[
            pl.BlockSpec((None, q, d_ssd), lambda b, t: (b, t, 0)),
            pl.BlockSpec((None, n_grp, n_dim, gw), lambda b, t: (b, 0, 0, 0)),
        ],
        out_shape=[
            jax.ShapeDtypeStruct((bn, l, d_ssd), BF16),
            jax.ShapeDtypeStruct((bn, n_grp, n_dim, gw), F32),
        ],
        scratch_shapes=[
            pltpu.VMEM((q + 8, d_ssd), F32),
            pltpu.VMEM((q + 8, d_bc), F32),
            pltpu.VMEM((n_grp, n_dim, gw), F32),
        ],
        compiler_params=_cparams("parallel", "arbitrary"),
        name="ssd",
    )(proj, proj, proj, proj, conv_buf[:, :, :d_ssd], conv_buf[:, :, d_ssd:], s0_t,
      conv_w[:, :d_ssd], conv_b[:d_ssd].reshape(1, d_ssd), conv_w[:, d_ssd:], conv_b[d_ssd:].reshape(1, d_bc),
      pad(dt_bias), pad(a_log), jnp.repeat(d_skip.astype(F32), p_dim).reshape(1, d_ssd),
      norm_w.reshape(1, d_ssd), expand)
    s_last = s_last.reshape(bn, n_grp, n_dim, hpg, p_dim).transpose(0, 1, 3, 4, 2).reshape(bn, n_heads, p_dim, n_dim)
    return y, s_last


def _hgrn_kernel(q_ref, f_ref, i_ref, lb_ref, nw_ref, s0_ref, y_ref, slast_ref, s_ref, *, n_heads):
    t = pl.program_id(1)
    c = CHUNK
    nsb = c // HG_SUB

    @pl.when(t == 0)
    def _():
        s_ref[...] = s0_ref[...]

    lbp = lb_ref[...]
    e = jnp.exp(lbp - jnp.max(lbp, axis=0, keepdims=True))
    lb = e[1:2, :] / jnp.sum(e, axis=0, keepdims=True)
    g = lb + (1.0 - lb) * _sigmoid(f_ref[...])
    logg = jnp.log(g)
    kk = 1.0 - g
    qq = _silu(q_ref[...])
    b = _cumsum_rows(logg)
    iv = i_ref[...]

    row = lax.broadcasted_iota(jnp.int32, (c, LANES), 0)
    blk = jnp.right_shift(row, HG_SUB.bit_length() - 1)
    r2 = lax.broadcasted_iota(jnp.int32, (c, c), 0)
    c2 = lax.broadcasted_iota(jnp.int32, (c, c), 1)
    causal = c2 <= r2
    nw = nw_ref[...]

    outs = []
    for h in range(n_heads):
        sl = slice(h * LANES, (h + 1) * LANES)
        bh, qh, kh, ih = b[:, sl], qq[:, sl], kk[:, sl], iv[:, sl]
        qsegs, ksegs = [], []
        for a in range(1, nsb):
            m = bh[a * HG_SUB - 1:a * HG_SUB, :]
            qsegs.append(jnp.where(blk == a, qh * jnp.exp(bh - m), 0.0))
            ksegs.append(jnp.where(blk < a, kh * jnp.exp(m - bh), 0.0))
        for d in range(nsb):
            m = bh[d * HG_SUB + HG_SUB // 2 - 1:d * HG_SUB + HG_SUB // 2, :]
            qsegs.append(jnp.where(blk == d, qh * jnp.exp(bh - m), 0.0))
            ksegs.append(jnp.where(blk == d, kh * jnp.exp(m - bh), 0.0))
        qcat = jnp.concatenate(qsegs, axis=1).astype(BF16)
        kcat = jnp.concatenate(ksegs, axis=1).astype(BF16)
        att = lax.dot_general(qcat, kcat, _NT, preferred_element_type=F32)
        att = jnp.where(causal, att, 0.0)
        ihb = ih.astype(BF16)
        s_t = s_ref[h]
        o = _dot(att.astype(BF16), ihb)
        o = o + lax.dot_general((qh * jnp.exp(bh)).astype(BF16), s_t.astype(BF16), _NT,
                                preferred_element_type=F32)
        b_last = bh[c - 1:c, :]
        kd = (kh * jnp.exp(b_last - bh)).astype(BF16)
        s_ref[h] = jnp.exp(b_last) * s_t + _dot(jnp.transpose(ih).astype(BF16), kd)
        outs.append(_rms(o, nw))
    y_ref[...] = jnp.concatenate(outs, axis=1).astype(BF16)
    slast_ref[...] = s_ref[...]


def _hgrn(proj, cols, s0, lb_param, norm_w, n_heads):
    bn, l, _ = proj.shape
    d = n_heads * LANES
    c = CHUNK
    assert l % c == 0
    q_col, f_col, i_col = cols
    depth = lb_param.shape[0]
    assert depth == 2
    s0_t = jnp.swapaxes(s0, -1, -2)
    kern = functools.partial(_hgrn_kernel, n_heads=n_heads)
    y, s_last = pl.pallas_call(
        kern,
        grid=(bn, l // c),
        in_specs=[
            pl.BlockSpec((None, c, d), lambda b, t: (b, t, q_col)),
            pl.BlockSpec((None, c, d), lambda b, t: (b, t, f_col)),
            pl.BlockSpec((None, c, d), lambda b, t: (b, t, i_col)),
            pl.BlockSpec((depth, d), lambda b, t: (0, 0)),
            pl.BlockSpec((1, LANES), lambda b, t: (0, 0)),
            pl.BlockSpec((None, n_heads, LANES, LANES), lambda b, t: (b, 0, 0, 0)),
        ],
        out_specs=[
            pl.BlockSpec((None, c, d), lambda b, t: (b, t, 0)),
            pl.BlockSpec((None, n_heads, LANES, LANES), lambda b, t: (b, 0, 0, 0)),
        ],
        out_shape=[
            jax.ShapeDtypeStruct((bn, l, d), BF16),
            jax.ShapeDtypeStruct((bn, n_heads, LANES, LANES), F32),
        ],
        scratch_shapes=[pltpu.VMEM((n_heads, LANES, LANES), F32)],
        compiler_params=_cparams("parallel", "arbitrary"),
        name="hgrn2",
    )(proj, proj, proj, lb_param, norm_w.reshape(1, LANES), s0_t)
    return y, jnp.swapaxes(s_last, -1, -2)


def kernel(x_prompt, x_sample, cache_diff_k, cache_diff_v, state_rglru_conv, state_rglru_h, state_ssd_conv,
           state_ssd, state_hgrn, norm_w, l0_w_in, l0_conv_w, l0_conv_b, l0_rg_w_r, l0_rg_b_r, l0_rg_w_i,
           l0_rg_b_i, l0_rg_lambda, l0_lq1, l0_lk1, l0_lq2, l0_lk2, l0_subln_w, l0_w_out, l1_w_in, l1_conv_w,
           l1_conv_b, l1_dt_bias, l1_a_log, l1_d_skip, l1_ssd_norm_w, l1_hg_lower_bound, l1_hg_norm_w, l1_w_out,
           ffn_w_gate, ffn_w_up, ffn_w_down):
    d_model = x_prompt.shape[-1]
    d_rnn = l0_conv_w.shape[1]
    n_heads_b, dv_b = cache_diff_k.shape[2], cache_diff_k.shape[3]
    d_att = n_heads_b * dv_b
    n_heads_c, p_c, n_c = state_ssd.shape[1:]
    d_ssd = n_heads_c * p_c
    d_xbc = l1_conv_w.shape[1]
    d_bc = d_xbc - d_ssd
    n_grp = d_bc // (2 * n_c)
    n_heads_d, dk_d, dv_d = state_hgrn.shape[1:]
    d_hg = n_heads_d * dk_d
    assert dv_b == LANES and dk_d == LANES and dv_d == LANES and d_rnn == d_att == d_ssd == d_hg

    w_in0 = l0_w_in.astype(BF16)
    o1 = d_ssd
    o2 = o1 + d_xbc
    o3 = o2 + n_heads_c
    w1 = l1_w_in
    w_in1 = jnp.concatenate(
        [w1[:, :o1], w1[:, o1:o1 + d_ssd], w1[:, o3:], w1[:, o1 + d_ssd:o2],
         jnp.pad(w1[:, o2:o3], ((0, 0), (0, LANES - n_heads_c)))], axis=1).astype(BF16)
    w_out0 = l0_w_out.astype(BF16)
    w_out1 = l1_w_out.astype(BF16)
    wg = ffn_w_gate.astype(BF16)
    wu = ffn_w_up.astype(BF16)
    wd = ffn_w_down.astype(BF16)
    w_r = l0_rg_w_r.astype(BF16)
    w_i = l0_rg_w_i.astype(BF16)
    tn0 = 1024
    tn1 = 1152
    assert w_in0.shape[1] % tn0 == 0 and w_in1.shape[1] % tn1 == 0
    ssd_cols = (0, 1, (2 * d_ssd + 3 * d_hg) // d_bc, (2 * d_ssd + 3 * d_hg + d_bc) // LANES)
    hg_cols = (2, 3, 4)

    def trunk(x, k_cache, v_cache, rg_conv, rg_h, ssd_conv, ssd_s, hg_s):
        bn, l, _ = x.shape
        m = bn * l
        x0 = x.reshape(m, d_model)
        proj0, k_new, v_new = _norm_matmul(x0, norm_w[0, 0], w_in0, tn0, [2 * d_rnn + d_att, d_att, d_att])
        proj0 = proj0.reshape(bn, l, -1)
        k_new = k_new.reshape(bn, l, d_att)
        v_new = v_new.reshape(bn, l, d_att)
        ya, rg_h_new = _rglru(proj0, rg_conv, rg_h, l0_conv_w, l0_conv_b, w_r, l0_rg_b_r, w_i, l0_rg_b_i,
                              l0_rg_lambda, d_rnn)
        yb = _diff_attention(proj0, 2 * d_rnn // LANES, k_new, v_new, k_cache, v_cache,
                             l0_lq1, l0_lk1, l0_lq2, l0_lk2, l0_subln_w, n_heads_b, 0)
        k_new = k_new.reshape(bn, l, n_heads_b, dv_b)
        v_new = v_new.reshape(bn, l, n_heads_b, dv_b)
        rg_conv_new = proj0[:, l - (CONV_W - 1):, :d_rnn]
        x1 = _out_proj(ya.reshape(m, d_rnn), yb.reshape(m, d_att), w_out0, x0, norm_w[0, 1])
        x2 = _ffn(x1, norm_w[0, 2], norm_w[0, 3], wg[0], wu[0], wd[0])
        proj1 = _norm_matmul(x2, norm_w[1, 0], w_in1, tn1)[0].reshape(bn, l, -1)
        ys, ssd_s_new = _ssd(proj1, ssd_cols, ssd_conv, ssd_s, l1_conv_w, l1_conv_b, l1_dt_bias, l1_a_log,
                             l1_d_skip, l1_ssd_norm_w, n_grp, n_c, p_c)
        yh, hg_s_new = _hgrn(proj1, hg_cols, hg_s, l1_hg_lower_bound, l1_hg_norm_w, n_heads_d)
        tail = proj1[:, l - (CONV_W - 1):, :]
        ssd_conv_new = jnp.concatenate(
            [tail[:, :, d_ssd:2 * d_ssd], tail[:, :, 2 * d_ssd + 3 * d_hg:2 * d_ssd + 3 * d_hg + d_bc]], axis=-1)
        x3 = _out_proj(ys.reshape(m, d_ssd), yh.reshape(m, d_hg), w_out1, x2, norm_w[1, 1])
        x4 = _ffn(x3, norm_w[1, 2], norm_w[1, 3], wg[1], wu[1], wd[1])
        return (x4.reshape(bn, l, d_model), k_new, v_new, rg_conv_new, rg_h_new, ssd_conv_new, ssd_s_new, hg_s_new)

    bp = x_prompt.shape[0]
    zeros = lambda *s: jnp.zeros(s, F32)
    outs_p = trunk(x_prompt, None, None, zeros(bp, CONV_W - 1, d_rnn), zeros(bp, d_rnn),
                   zeros(bp, CONV_W - 1, d_xbc), zeros(bp, n_heads_c, p_c, n_c), zeros(bp, n_heads_d, dk_d, dv_d))
    outs_s = trunk(x_sample, cache_diff_k, cache_diff_v, state_rglru_conv, state_rglru_h, state_ssd_conv,
                   state_ssd, state_hgrn)
    return (outs_p[0], outs_s[0]) + tuple(outs_p[1:]) + tuple(outs_s[1:])
```

```python
import functools
import math

import jax
import jax.numpy as jnp
from jax import lax
from jax.experimental import pallas as pl
from jax.experimental.pallas import tpu as pltpu

F32 = jnp.float32
BF16 = jnp.bfloat16

EPS = 1e-6
CHUNK = 64
CONV_W = 4
RG_C = 8.0
LANES = 128
SUBLANES = 8
HG_SUB = 16
NEG = -1e30
VMEM_LIMIT_BYTES = 56 * 1024 * 1024

_NT = (((1,), (1,)), ((), ()))
_TN = (((0,), (0,)), ((), ()))


def _cparams(*sem):
    return pltpu.CompilerParams(dimension_semantics=sem, vmem_limit_bytes=VMEM_LIMIT_BYTES)


def _sigmoid(x):
    return 1.0 / (1.0 + jnp.exp(-x))


def _silu(x):
    return x * _sigmoid(x)


def _softplus(x):
    return jnp.maximum(x, 0.0) + jnp.log1p(jnp.exp(-jnp.abs(x)))


def _neg_expm1(x):
    return -jnp.tanh(0.5 * x) * (jnp.exp(x) + 1.0)


def _gelu_tanh(x):
    c = math.sqrt(2.0 / math.pi)
    return x * (0.5 * (1.0 + jnp.tanh(c * (x + 0.044715 * (x * x * x)))))


def _rms(x, w):
    return x * lax.rsqrt(jnp.mean(x * x, axis=-1, keepdims=True) + EPS) * w


def _dot(a, b):
    return jnp.dot(a, b, preferred_element_type=F32)


def _cumsum_rows(x):
    n = x.shape[0]
    row = lax.broadcasted_iota(jnp.int32, x.shape, 0)
    s = 1
    while s < n:
        x = x + jnp.where(row >= s, pltpu.roll(x, s, axis=0), 0.0)
        s *= 2
    return x


def _norm_matmul_kernel(x_ref, nw_ref, w_ref, o_ref, xn_ref):
    @pl.when(pl.program_id(1) == 0)
    def _():
        xn_ref[...] = _rms(x_ref[...], nw_ref[...]).astype(BF16)

    o_ref[...] = _dot(xn_ref[...], w_ref[...])


def _norm_matmul(x, nw, w, tn):
    m, d = x.shape
    n = w.shape[1]
    assert n % tn == 0
    tm = min(1024, m)
    return pl.pallas_call(
        _norm_matmul_kernel,
        grid=(m // tm, n // tn),
        in_specs=[
            pl.BlockSpec((tm, d), lambda i, j: (i, 0)),
            pl.BlockSpec((1, d), lambda i, j: (0, 0)),
            pl.BlockSpec((d, tn), lambda i, j: (0, j)),
        ],
        out_specs=[pl.BlockSpec((tm, tn), lambda i, j: (i, j)), pl.BlockSpec((tm, d), lambda i, j: (i, 0))],
        out_shape=[jax.ShapeDtypeStruct((m, n), F32), jax.ShapeDtypeStruct((m, d), BF16)],
        compiler_params=_cparams("parallel", "arbitrary"),
        name="norm_in_proj",
    )(x, nw.reshape(1, d), w)


def _head_proj_kernel(xn_ref, w_ref, *o_refs):
    n_out = len(o_refs) // 2
    j = pl.program_id(1)
    for k in range(n_out):
        @pl.when(j == k)
        def _(k=k):
            o_ref, ob_ref = o_refs[k], o_refs[n_out + k]
            res = _dot(xn_ref[...], w_ref[...])
            for h in range(o_ref.shape[1]):
                o_ref[:, h, :] = res[:, h * LANES:(h + 1) * LANES]
            ob_ref[...] = res.astype(BF16)


def _head_proj(xn, w, n_heads):
    m, d = xn.shape
    width = n_heads * LANES
    n_out = w.shape[1] // width
    tm = min(1024, m)
    return pl.pallas_call(
        _head_proj_kernel,
        grid=(m // tm, n_out),
        in_specs=[
            pl.BlockSpec((tm, d), lambda i, j: (i, 0)),
            pl.BlockSpec((d, width), lambda i, j: (0, j)),
        ],
        out_specs=([pl.BlockSpec((tm, n_heads, LANES), lambda i, j: (i, 0, 0)) for _ in range(n_out)]
                   + [pl.BlockSpec((tm, width), lambda i, j: (i, 0)) for _ in range(n_out)]),
        out_shape=([jax.ShapeDtypeStruct((m, n_heads, LANES), F32) for _ in range(n_out)]
                   + [jax.ShapeDtypeStruct((m, width), BF16) for _ in range(n_out)]),
        compiler_params=_cparams("parallel", "arbitrary"),
        name="head_proj",
    )(xn, w)


def _out_proj_kernel(ya_ref, yb_ref, wa_ref, wb_ref, x_ref, nw_ref, o_ref):
    m = _dot(ya_ref[...], wa_ref[...]) + _dot(yb_ref[...], wb_ref[...])
    o_ref[...] = x_ref[...] + _rms(m, nw_ref[...])


def _out_proj(ya, yb, w_out, x, nw):
    m, d = x.shape
    da, db = ya.shape[1], yb.shape[1]
    tm = min(512, m)
    return pl.pallas_call(
        _out_proj_kernel,
        grid=(m // tm,),
        in_specs=[
            pl.BlockSpec((tm, da), lambda i: (i, 0)),
            pl.BlockSpec((tm, db), lambda i: (i, 0)),
            pl.BlockSpec((da, d), lambda i: (0, 0)),
            pl.BlockSpec((db, d), lambda i: (0, 0)),
            pl.BlockSpec((tm, d), lambda i: (i, 0)),
            pl.BlockSpec((1, d), lambda i: (0, 0)),
        ],
        out_specs=pl.BlockSpec((tm, d), lambda i: (i, 0)),
        out_shape=jax.ShapeDtypeStruct((m, d), F32),
        compiler_params=_cparams("parallel"),
        name="out_proj",
    )(ya, yb, w_out[:da], w_out[da:], x, nw.reshape(1, d))


def _ffn_kernel(x_ref, nwa_ref, nwb_ref, wg_ref, wu_ref, wd_ref, o_ref, hn_ref, acc_ref):
    f = pl.program_id(1)

    @pl.when(f == 0)
    def _():
        hn_ref[...] = _rms(x_ref[...], nwa_ref[...]).astype(BF16)
        acc_ref[...] = jnp.zeros_like(acc_ref)

    hn = hn_ref[...]
    g = _dot(hn, wg_ref[...])
    u = _dot(hn, wu_ref[...])
    a = (_silu(g) * u).astype(BF16)
    acc_ref[...] += _dot(a, wd_ref[...])

    @pl.when(f == pl.num_programs(1) - 1)
    def _():
        o_ref[...] = x_ref[...] + _rms(acc_ref[...], nwb_ref[...])


def _ffn(x, nw_pre, nw_post, wg, wu, wd):
    m, d = x.shape
    dff = wg.shape[1]
    tm = min(512, m)
    tf = 512
    return pl.pallas_call(
        _ffn_kernel,
        grid=(m // tm, dff // tf),
        in_specs=[
            pl.BlockSpec((tm, d), lambda i, f: (i, 0)),
            pl.BlockSpec((1, d), lambda i, f: (0, 0)),
            pl.BlockSpec((1, d), lambda i, f: (0, 0)),
            pl.BlockSpec((d, tf), lambda i, f: (0, f)),
            pl.BlockSpec((d, tf), lambda i, f: (0, f)),
            pl.BlockSpec((tf, d), lambda i, f: (f, 0)),
        ],
        out_specs=pl.BlockSpec((tm, d), lambda i, f: (i, 0)),
        out_shape=jax.ShapeDtypeStruct((m, d), F32),
        scratch_shapes=[pltpu.VMEM((tm, d), BF16), pltpu.VMEM((tm, d), F32)],
        compiler_params=_cparams("parallel", "arbitrary"),
        name="ffn",
    )(x, nw_pre.reshape(1, d), nw_post.reshape(1, d), wg, wu, wd)


def _conv_tile(xs_ref, x_tile, cw_ref, cb_ref, tl):
    xs_ref[8:8 + tl, :] = x_tile
    y = cb_ref[...]
    for tap in range(CONV_W):
        y = y + xs_ref[5 + tap:5 + tap + tl, :] * cw_ref[tap:tap + 1, :]
    xs_ref[5:8, :] = xs_ref[tl + 5:tl + 8, :]
    return y


def _rglru_kernel(xa_ref, ga_ref, cbuf_ref, h0_ref, cw_ref, cb_ref, wr_ref, br_ref, wi_ref, bi_ref,
                  lam_ref, ya_ref, hlast_ref, xs_ref, a_ref, u_ref, h_ref, *, tl, n_blk):
    t = pl.program_id(1)

    @pl.when(t == 0)
    def _():
        xs_ref[5:8, :] = cbuf_ref[...]
        h_ref[...] = h0_ref[...]

    xc = _conv_tile(xs_ref, xa_ref[...], cw_ref, cb_ref, tl)
    xcb = xc.astype(BF16)
    r_pre = jnp.concatenate(
        [_dot(xcb[:, hb * LANES:(hb + 1) * LANES], wr_ref[hb]) for hb in range(n_blk)], axis=1)
    i_pre = jnp.concatenate(
        [_dot(xcb[:, hb * LANES:(hb + 1) * LANES], wi_ref[hb]) for hb in range(n_blk)], axis=1)
    r = _sigmoid(r_pre + br_ref[...])
    gi = _sigmoid(i_pre + bi_ref[...])
    log_a = -RG_C * r * _softplus(-lam_ref[...])
    a_ref[...] = jnp.exp(log_a)
    u_ref[...] = jnp.sqrt(_neg_expm1(2.0 * log_a)) * (gi * xc)

    row = lax.broadcasted_iota(jnp.int32, (SUBLANES, a_ref.shape[1]), 0)

    def group(gidx, h_prev):
        r0 = pl.multiple_of(gidx * SUBLANES, SUBLANES)
        ag = a_ref[pl.ds(r0, SUBLANES), :]
        ug = u_ref[pl.ds(r0, SUBLANES), :]
        s = 1
        while s < SUBLANES:
            a_sh = jnp.where(row >= s, pltpu.roll(ag, s, axis=0), 1.0)
            u_sh = jnp.where(row >= s, pltpu.roll(ug, s, axis=0), 0.0)
            ug = ag * u_sh + ug
            ag = ag * a_sh
            s *= 2
        hg = ag * h_prev + ug
        u_ref[pl.ds(r0, SUBLANES), :] = hg
        return hg[SUBLANES - 1:SUBLANES, :]

    h_last = lax.fori_loop(0, tl // SUBLANES, group, h_ref[...])
    h_ref[...] = h_last
    hlast_ref[...] = h_last
    ya_ref[...] = (u_ref[...] * _gelu_tanh(ga_ref[...])).astype(BF16)


def _rglru(proj, conv_buf, h0, conv_w, conv_b, w_r, b_r, w_i, b_i, lam, d_rnn):
    bn, l, _ = proj.shape
    tl = min(256, l)
    n_blk = w_r.shape[0]
    vec = lambda v: v.reshape(1, d_rnn)
    const2 = lambda b, t: (0, 0)
    kern = functools.partial(_rglru_kernel, tl=tl, n_blk=n_blk)
    ya, h_last = pl.pallas_call(
        kern,
        grid=(bn, l // tl),
        in_specs=[
            pl.BlockSpec((None, tl, d_rnn), lambda b, t: (b, t, 0)),
            pl.BlockSpec((None, tl, d_rnn), lambda b, t: (b, t, 1)),
            pl.BlockSpec((None, CONV_W - 1, d_rnn), lambda b, t: (b, 0, 0)),
            pl.BlockSpec((None, 1, d_rnn), lambda b, t: (b, 0, 0)),
            pl.BlockSpec((CONV_W, d_rnn), const2),
            pl.BlockSpec((1, d_rnn), const2),
            pl.BlockSpec(w_r.shape, lambda b, t: (0, 0, 0)),
            pl.BlockSpec((1, d_rnn), const2),
            pl.BlockSpec(w_i.shape, lambda b, t: (0, 0, 0)),
            pl.BlockSpec((1, d_rnn), const2),
            pl.BlockSpec((1, d_rnn), const2),
        ],
        out_specs=[
            pl.BlockSpec((None, tl, d_rnn), lambda b, t: (b, t, 0)),
            pl.BlockSpec((None, 1, d_rnn), lambda b, t: (b, 0, 0)),
        ],
        out_shape=[
            jax.ShapeDtypeStruct((bn, l, d_rnn), BF16),
            jax.ShapeDtypeStruct((bn, 1, d_rnn), F32),
        ],
        scratch_shapes=[
            pltpu.VMEM((tl + 8, d_rnn), F32),
            pltpu.VMEM((tl, d_rnn), F32),
            pltpu.VMEM((tl, d_rnn), F32),
            pltpu.VMEM((1, d_rnn), F32),
        ],
        compiler_params=_cparams("parallel", "arbitrary"),
        name="rglru",
    )(proj, proj, conv_buf, h0.reshape(bn, 1, d_rnn), conv_w, vec(conv_b), w_r, vec(b_r), w_i, vec(b_i),
      vec(lam))
    return ya, h_last.reshape(bn, d_rnn)


def _chunk_mask(kpos, qpos):
    shift = CHUNK.bit_length() - 1
    return jnp.right_shift(kpos, shift) <= jnp.right_shift(qpos, shift)


def _diff_lambda(lq1_ref, lk1_ref, lq2_ref, lk2_ref, lam_init):
    return (jnp.exp(jnp.sum(lq1_ref[...] * lk1_ref[...], axis=-1, keepdims=True))
            - jnp.exp(jnp.sum(lq2_ref[...] * lk2_ref[...], axis=-1, keepdims=True)) + lam_init)


ONES_ROWS = 16


def _attn_prompt_kernel(q_ref, k_ref, v_ref, lq1_ref, lk1_ref, lq2_ref, lk2_ref, sw_ref, o_ref,
                        vt_ref, sa_ref, sb_ref, m_ref, acc_ref, *, t, nblk, dh, scale, lam_init):
    i = pl.program_id(2)
    dv = 2 * dh

    @pl.when(i == 0)
    def _():
        ones = jnp.ones((ONES_ROWS, t), BF16)
        for jb in range(nblk):
            vt_ref[jb, :dv, :] = jnp.transpose(v_ref[jb * t:(jb + 1) * t, :].astype(F32)).astype(BF16)
            vt_ref[jb, dv:, :] = ones

    qt = jnp.transpose(q_ref[...] * (scale * math.log2(math.e)))
    sub = lax.broadcasted_iota(jnp.int32, qt.shape, 0)
    qt2 = jnp.concatenate([jnp.where(sub < dh, qt, 0.0), jnp.where(sub >= dh, qt, 0.0)],
                          axis=1).astype(BF16)
    m_ref[...] = jnp.full_like(m_ref, NEG)
    acc_ref[...] = jnp.zeros_like(acc_ref)

    def scores(j, s_ref):
        s_ref[...] = _dot(k_ref[pl.ds(pl.multiple_of(j * t, t), t), :], qt2)

    def update(j, s_ref, mask):
        s = s_ref[...]
        if mask is not None:
            s = jnp.where(mask, s, NEG)
        m_prev = m_ref[...]
        m_new = jnp.maximum(m_prev, jnp.max(s, axis=0, keepdims=True))
        alpha = jnp.exp2(m_prev - m_new)
        p = jnp.exp2(s - m_new).astype(BF16)
        acc_ref[...] = alpha * acc_ref[...] + _dot(vt_ref[j], p)
        m_ref[...] = m_new

    scores(0, sa_ref)

    def pair(pidx, carry):
        j = 2 * pidx
        scores(j + 1, sb_ref)
        update(j, sa_ref, None)
        scores(j + 2, sa_ref)
        update(j + 1, sb_ref, None)
        return carry

    lax.fori_loop(0, lax.shift_right_logical(i, 1), pair, 0)

    @pl.when((i & 1) == 1)
    def _():
        scores(i, sb_ref)
        update(i - 1, sa_ref, None)
        sa_ref[...] = sb_ref[...]

    base = i * t
    kpos = base + lax.broadcasted_iota(jnp.int32, (t, t), 0)
    qpos = base + lax.broadcasted_iota(jnp.int32, (t, t), 1)
    mask = _chunk_mask(kpos, qpos)
    update(i, sa_ref, jnp.concatenate([mask, mask], axis=1))

    lam = _diff_lambda(lq1_ref, lk1_ref, lq2_ref, lk2_ref, lam_init)
    acc = acc_ref[0:dv, :]
    l = acc_ref[dv:dv + 1, :]
    o_t = acc[:, :t] / l[:, :t] - lam * (acc[:, t:] / l[:, t:])
    o_ref[...] = (_rms(jnp.transpose(o_t), sw_ref[...]) * (1.0 - lam_init)).astype(BF16)


def _attn_cached_kernel(q_ref, kc_ref, vc_ref, kn_ref, vn_ref, lq1_ref, lk1_ref, lq2_ref, lk2_ref, sw_ref,
                        o_ref, q2_ref, m_ref, l_ref, acc_ref, *, n_heads, lq, n_cache, pos0, dh, scale,
                        lam_init):
    j = pl.program_id(1)
    hsl = lambda h: slice(h * LANES, (h + 1) * LANES)

    @pl.when(j == 0)
    def _():
        lane = lax.broadcasted_iota(jnp.int32, (lq, LANES), 1)
        for h in range(n_heads):
            q = q_ref[:, hsl(h)] * scale
            q2_ref[h] = jnp.concatenate([jnp.where(lane < dh, q, 0.0), jnp.where(lane >= dh, q, 0.0)],
                                        axis=0).astype(BF16)
        m_ref[...] = jnp.full_like(m_ref, NEG)
        l_ref[...] = jnp.zeros_like(l_ref)
        acc_ref[...] = jnp.zeros_like(acc_ref)

    def update(k_ref, v_ref, mask):
        for h in range(n_heads):
            if len(k_ref.shape) == 3:
                kb = k_ref[:, h, :].astype(BF16)
                vb = v_ref[:, h, :].astype(BF16)
            else:
                kb = k_ref[:, hsl(h)]
                vb = v_ref[:, hsl(h)]
            s = lax.dot_general(q2_ref[h], kb, _NT, preferred_element_type=F32)
            if mask is not None:
                s = jnp.where(mask, s, NEG)
            m_prev = m_ref[h]
            m_new = jnp.maximum(m_prev, jnp.max(s, axis=-1, keepdims=True))
            alpha = jnp.exp(m_prev - m_new)
            p = jnp.exp(s - m_new)
            l_ref[h] = alpha * l_ref[h] + jnp.sum(p, axis=-1, keepdims=True)
            acc_ref[h] = alpha * acc_ref[h] + _dot(p.astype(BF16), vb)
            m_ref[h] = m_new

    @pl.when(j < n_cache)
    def _():
        update(kc_ref, vc_ref, None)

    @pl.when(j == n_cache)
    def _():
        row = lax.broadcasted_iota(jnp.int32, (2 * lq, lq), 0)
        qpos = pos0 + jnp.where(row >= lq, row - lq, row)
        kpos = pos0 + lax.broadcasted_iota(jnp.int32, (2 * lq, lq), 1)
        update(kn_ref, vn_ref, _chunk_mask(kpos, qpos))
        lam = _diff_lambda(lq1_ref, lk1_ref, lq2_ref, lk2_ref, lam_init)
        outs = []
        for h in range(n_heads):
            acc = acc_ref[h]
            l = l_ref[h]
            o = acc[:lq] / l[:lq] - lam * (acc[lq:] / l[lq:])
            outs.append(_rms(o, sw_ref[...]) * (1.0 - lam_init))
        o_ref[...] = jnp.concatenate(outs, axis=1).astype(BF16)


def _diff_attention(qsrc, q_col, k, v, k_cache, v_cache, lq1, lk1, lq2, lk2, subln_w, n_heads, layer):
    bn, l = k.shape[:2]
    dv = LANES
    dh = dv // 2
    d_att = n_heads * dv
    lam_init = 0.8 - 0.6 * math.exp(-0.3 * layer)
    scale = dh ** -0.5
    params = [lq1.reshape(1, dh), lk1.reshape(1, dh), lq2.reshape(1, dh), lk2.reshape(1, dh),
              subln_w.reshape(1, dv)]
    if k_cache is None:
        t = min(512, l)
        nblk = l // t
        assert t % CHUNK == 0 and l % t == 0
        small = lambda n: pl.BlockSpec((1, n), lambda b, h, i: (0, 0))
        kern = functools.partial(_attn_prompt_kernel, t=t, nblk=nblk, dh=dh, scale=scale, lam_init=lam_init)
        return pl.pallas_call(
            kern,
            grid=(bn, n_heads, nblk),
            in_specs=[
                pl.BlockSpec((None, t, dv), lambda b, h, i: (b, i, q_col + h)),
                pl.BlockSpec((None, l, dv), lambda b, h, i: (b, 0, h)),
                pl.BlockSpec((None, l, dv), lambda b, h, i: (b, 0, h)),
                small(dh), small(dh), small(dh), small(dh), small(dv),
            ],
            out_specs=pl.BlockSpec((None, t, dv), lambda b, h, i: (b, i, h)),
            out_shape=jax.ShapeDtypeStruct((bn, l, d_att), BF16),
            scratch_shapes=[
                pltpu.VMEM((nblk, dv + ONES_ROWS, t), BF16),
                pltpu.VMEM((t, 2 * t), F32),
                pltpu.VMEM((t, 2 * t), F32),
                pltpu.VMEM((1, 2 * t), F32),
                pltpu.VMEM((dv + ONES_ROWS, 2 * t), F32),
            ],
            compiler_params=_cparams("parallel", "parallel", "arbitrary"),
            name="diff_attention_prompt",
        )(qsrc, k, v, *params)

    pos0 = k_cache.shape[1]
    tkc = min(512, pos0)
    assert pos0 % CHUNK == 0 and pos0 % tkc == 0 and l % SUBLANES == 0
    n_cache = pos0 // tkc
    small = lambda n: pl.BlockSpec((1, n), lambda b, j: (0, 0))
    kern = functools.partial(_attn_cached_kernel, n_heads=n_heads, lq=l, n_cache=n_cache, pos0=pos0, dh=dh,
                             scale=scale, lam_init=lam_init)
    cache_spec = pl.BlockSpec((None, tkc, n_heads, dv), lambda b, j: (b, jnp.minimum(j, n_cache - 1), 0, 0))
    new_spec = pl.BlockSpec((None, l, d_att), lambda b, j: (b, 0, 0))
    return pl.pallas_call(
        kern,
        grid=(bn, n_cache + 1),
        in_specs=[
            pl.BlockSpec((None, l, d_att), lambda b, j: (b, 0, q_col * dv // d_att)),
            cache_spec, cache_spec, new_spec, new_spec,
            small(dh), small(dh), small(dh), small(dh), small(dv),
        ],
        out_specs=pl.BlockSpec((None, l, d_att), lambda b, j: (b, 0, 0)),
        out_shape=jax.ShapeDtypeStruct((bn, l, d_att), BF16),
        scratch_shapes=[
            pltpu.VMEM((n_heads, 2 * l, dv), BF16),
            pltpu.VMEM((n_heads, 2 * l, 1), F32),
            pltpu.VMEM((n_heads, 2 * l, 1), F32),
            pltpu.VMEM((n_heads, 2 * l, dv), F32),
        ],
        compiler_params=_cparams("parallel", "arbitrary"),
        name="diff_attention_cached",
    )(qsrc, k_cache, v_cache, k, v, *params)


def _ssd_kernel(z_ref, x_ref, bc_ref, dt_ref, cbx_ref, cbbc_ref, s0_ref, cwx_ref, cbiasx_ref, cwbc_ref,
                cbiasbc_ref, dtb_ref, alog_ref, dskip_ref, nw_ref, expand_ref,
                y_ref, slast_ref, xsx_ref, xsbc_ref, s_ref, *, n_grp, hpg, p_dim, n_dim):
    t = pl.program_id(1)
    q = CHUNK
    d_ssd = n_grp * hpg * p_dim
    gw = hpg * p_dim

    @pl.when(t == 0)
    def _():
        xsx_ref[5:8, :] = cbx_ref[...]
        xsbc_ref[5:8, :] = cbbc_ref[...]
        s_ref[...] = s0_ref[...]

    xs = _silu(_conv_tile(xsx_ref, x_ref[...], cwx_ref, cbiasx_ref, q))
    bcs = _silu(_conv_tile(xsbc_ref, bc_ref[...], cwbc_ref, cbiasbc_ref, q))

    dt = _softplus(dt_ref[...] + dtb_ref[...])
    a = -jnp.exp(alog_ref[...])
    acum = _cumsum_rows(dt * a)
    a_last = acum[q - 1:q, :]
    wst = jnp.exp(a_last - acum) * dt
    eac = jnp.exp(acum)
    acum_t = jnp.transpose(acum)

    stacked = jnp.concatenate([dt, wst, eac], axis=0)
    expd = jnp.dot(stacked, expand_ref[...], precision=lax.Precision.HIGHEST, preferred_element_type=F32)
    dt_e, wst_e, eac_e = expd[0:q], expd[q:2 * q], expd[2 * q:3 * q]

    xdt = (xs * dt_e).astype(BF16)
    xw = (xs * wst_e).astype(BF16)
    row = lax.broadcasted_iota(jnp.int32, (q, q), 0)
    col = lax.broadcasted_iota(jnp.int32, (q, q), 1)
    causal = col <= row
    lane = lax.broadcasted_iota(jnp.int32, (q, LANES), 1)
    heads_per_tile = LANES // p_dim

    y_parts = []
    for g in range(n_grp):
        bg = bcs[:, g * n_dim:(g + 1) * n_dim]
        cg = bcs[:, (n_grp + g) * n_dim:(n_grp + g + 1) * n_dim].astype(BF16)
        cb = lax.dot_general(cg, bg.astype(BF16), _NT, preferred_element_type=F32)
        s_prev = s_ref[g]
        y_off = _dot(cg, s_prev.astype(BF16)) * eac_e[:, g * gw:(g + 1) * gw]
        y_diag = []
        for tile in range(gw // LANES):
            c0 = g * gw + tile * LANES
            x_tile = xdt[:, c0:c0 + LANES]
            acc = None
            for hh in range(heads_per_tile):
                h = (c0 // p_dim) + hh
                dec = jnp.exp(jnp.where(causal, acum[:, h:h + 1] - acum_t[h:h + 1, :], -jnp.inf))
                yh = _dot((cb * dec).astype(BF16), x_tile)
                sel = jnp.logical_and(lane >= hh * p_dim, lane < (hh + 1) * p_dim)
                acc = jnp.where(sel, yh, 0.0) if acc is None else jnp.where(sel, yh, acc)
            y_diag.append(acc)
        y_parts.append(jnp.concatenate(y_diag, axis=1) + y_off)
        bg_t = jnp.transpose(bg).astype(BF16)
        st = _dot(bg_t, xw[:, g * gw:(g + 1) * gw])
        s_ref[g] = eac_e[q - 1:q, g * gw:(g + 1) * gw] * s_prev + st
    y = jnp.concatenate(y_parts, axis=1)
    y = (y + dskip_ref[...] * xs) * _silu(z_ref[...])
    nw = nw_ref[...]
    y_ref[...] = jnp.concatenate(
        [_rms(y[:, g * gw:(g + 1) * gw], nw[:, g * gw:(g + 1) * gw]) for g in range(n_grp)],
        axis=1).astype(BF16)
    slast_ref[...] = s_ref[...]


def _ssd(proj, cols, conv_buf, s0, conv_w, conv_b, dt_bias, a_log, d_skip, norm_w, n_grp, n_dim, p_dim):
    bn, l, _ = proj.shape
    n_heads = dt_bias.shape[0]
    hpg = n_heads // n_grp
    d_ssd = n_heads * p_dim
    d_bc = 2 * n_grp * n_dim
    gw = hpg * p_dim
    q = CHUNK
    assert l % q == 0 and n_heads <= LANES and LANES % p_dim == 0
    z_col, x_col, bc_col, dt_col = cols
    pad = lambda v: jnp.pad(v.astype(F32), (0, LANES - n_heads)).reshape(1, LANES)
    expand = (jnp.arange(LANES)[:, None] == (jnp.arange(d_ssd)[None, :] // p_dim)).astype(F32)
    s0_t = s0.reshape(bn, n_grp, hpg, p_dim, n_dim).transpose(0, 1, 4, 2, 3).reshape(bn, n_grp, n_dim, gw)
    const2 = lambda b, t: (0, 0)
    kern = functools.partial(_ssd_kernel, n_grp=n_grp, hpg=hpg, p_dim=p_dim, n_dim=n_dim)
    y, s_last = pl.pallas_call(
        kern,
        grid=(bn, l // q),
        in_specs=[
            pl.BlockSpec((None, q, d_ssd), lambda b, t: (b, t, z_col)),
            pl.BlockSpec((None, q, d_ssd), lambda b, t: (b, t, x_col)),
            pl.BlockSpec((None, q, d_bc), lambda b, t: (b, t, bc_col)),
            pl.BlockSpec((None, q, LANES), lambda b, t: (b, t, dt_col)),
            pl.BlockSpec((None, CONV_W - 1, d_ssd), lambda b, t: (b, 0, 0)),
            pl.BlockSpec((None, CONV_W - 1, d_bc), lambda b, t: (b, 0, 0)),
            pl.BlockSpec((None, n_grp, n_dim, gw), lambda b, t: (b, 0, 0, 0)),
            pl.BlockSpec((CONV_W, d_ssd), const2),
            pl.BlockSpec((1, d_ssd), const2),
            pl.BlockSpec((CONV_W, d_bc), const2),
            pl.BlockSpec((1, d_bc), const2),
            pl.BlockSpec((1, LANES), const2),
            pl.BlockSpec((1, LANES), const2),
            pl.BlockSpec((1, d_ssd), const2),
            pl.BlockSpec((1, d_ssd), const2),
            pl.BlockSpec((LANES, d_ssd), const2),
        ],
        out_specs=[
            pl.BlockSpec((None, q, d_ssd), lambda b, t: (b, t, 0)),
            pl.BlockSpec((None, n_grp, n_dim, gw), lambda b, t: (b, 0, 0, 0)),
        ],
        out_shape=[
            jax.ShapeDtypeStruct((bn, l, d_ssd), BF16),
            jax.ShapeDtypeStruct((bn, n_grp, n_dim, gw), F32),
        ],
        scratch_shapes=[
            pltpu.VMEM((q + 8, d_ssd), F32),
            pltpu.VMEM((q + 8, d_bc), F32),
            pltpu.VMEM((n_grp, n_dim, gw), F32),
        ],
        compiler_params=_cparams("parallel", "arbitrary"),
        name="ssd",
    )(proj, proj, proj, proj, conv_buf[:, :, :d_ssd], conv_buf[:, :, d_ssd:], s0_t,
      conv_w[:, :d_ssd], conv_b[:d_ssd].reshape(1, d_ssd), conv_w[:, d_ssd:], conv_b[d_ssd:].reshape(1, d_bc),
      pad(dt_bias), pad(a_log), jnp.repeat(d_skip.astype(F32), p_dim).reshape(1, d_ssd),
      norm_w.reshape(1, d_ssd), expand)
    s_last = s_last.reshape(bn, n_grp, n_dim, hpg, p_dim).transpose(0, 1, 3, 4, 2).reshape(bn, n_heads, p_dim, n_dim)
    return y, s_last


def _hgrn_kernel(q_ref, f_ref, i_ref, lb_ref, nw_ref, s0_ref, y_ref, slast_ref, s_ref, *, n_heads):
    t = pl.program_id(1)
    c = CHUNK
    nsb = c // HG_SUB

    @pl.when(t == 0)
    def _():
        s_ref[...] = s0_ref[...]

    lbp = lb_ref[...]
    e = jnp.exp(lbp - jnp.max(lbp, axis=0, keepdims=True))
    lb = e[1:2, :] / jnp.sum(e, axis=0, keepdims=True)
    g = lb + (1.0 - lb) * _sigmoid(f_ref[...])
    logg = jnp.log(g)
    kk = 1.0 - g
    qq = _silu(q_ref[...])
    b = _cumsum_rows(logg)
    iv = i_ref[...]

    zero_blk = jnp.zeros((HG_SUB, LANES), F32)
    r2 = lax.broadcasted_iota(jnp.int32, (c, c), 0)
    c2 = lax.broadcasted_iota(jnp.int32, (c, c), 1)
    causal = c2 <= r2
    nw = nw_ref[...]

    outs = []
    for h in range(n_heads):
        sl = slice(h * LANES, (h + 1) * LANES)
        bh, qh, kh, ih = b[:, sl], qq[:, sl], kk[:, sl], iv[:, sl]
        sb = lambda x, r: x[r * HG_SUB:(r + 1) * HG_SUB, :]
        m_off = [bh[a * HG_SUB - 1:a * HG_SUB, :] for a in range(1, nsb)]
        m_mid = [bh[d * HG_SUB + HG_SUB // 2 - 1:d * HG_SUB + HG_SUB // 2, :] for d in range(nsb)]
        q_rows, k_rows = [], []
        for r in range(nsb):
            b_r, q_r, k_r = sb(bh, r), sb(qh, r), sb(kh, r)
            q_parts = [q_r * jnp.exp(b_r - m_off[a - 1]) if a == r else zero_blk for a in range(1, nsb)]
            k_parts = [k_r * jnp.exp(m_off[a - 1] - b_r) if a > r else zero_blk for a in range(1, nsb)]
            q_parts += [q_r * jnp.exp(b_r - m_mid[d]) if d == r else zero_blk for d in range(nsb)]
            k_parts += [k_r * jnp.exp(m_mid[d] - b_r) if d == r else zero_blk for d in range(nsb)]
            q_rows.append(jnp.concatenate(q_parts, axis=1))
            k_rows.append(jnp.concatenate(k_parts, axis=1))
        qcat = jnp.concatenate(q_rows, axis=0).astype(BF16)
        kcat = jnp.concatenate(k_rows, axis=0).astype(BF16)
        att = lax.dot_general(qcat, kcat, _NT, preferred_element_type=F32)
        att = jnp.where(causal, att, 0.0)
        ihb = ih.astype(BF16)
        s_t = s_ref[h]
        o = _dot(att.astype(BF16), ihb)
        o = o + lax.dot_general((qh * jnp.exp(bh)).astype(BF16), s_t.astype(BF16), _NT,
                                preferred_element_type=F32)
        b_last = bh[c - 1:c, :]
        kd = (kh * jnp.exp(b_last - bh)).astype(BF16)
        s_ref[h] = jnp.exp(b_last) * s_t + _dot(jnp.transpose(ih).astype(BF16), kd)
        outs.append(_rms(o, nw))
    y_ref[...] = jnp.concatenate(outs, axis=1).astype(BF16)
    slast_ref[...] = s_ref[...]


def _hgrn(proj, cols, s0, lb_param, norm_w, n_heads):
    bn, l, _ = proj.shape
    d = n_heads * LANES
    c = CHUNK
    assert l % c == 0
    q_col, f_col, i_col = cols
    depth = lb_param.shape[0]
    assert depth == 2
    s0_t = jnp.swapaxes(s0, -1, -2)
    kern = functools.partial(_hgrn_kernel, n_heads=n_heads)
    y, s_last = pl.pallas_call(
        kern,
        grid=(bn, l // c),
        in_specs=[
            pl.BlockSpec((None, c, d), lambda b, t: (b, t, q_col)),
            pl.BlockSpec((None, c, d), lambda b, t: (b, t, f_col)),
            pl.BlockSpec((None, c, d), lambda b, t: (b, t, i_col)),
            pl.BlockSpec((depth, d), lambda b, t: (0, 0)),
            pl.BlockSpec((1, LANES), lambda b, t: (0, 0)),
            pl.BlockSpec((None, n_heads, LANES, LANES), lambda b, t: (b, 0, 0, 0)),
        ],
        out_specs=[
            pl.BlockSpec((None, c, d), lambda b, t: (b, t, 0)),
            pl.BlockSpec((None, n_heads, LANES, LANES), lambda b, t: (b, 0, 0, 0)),
        ],
        out_shape=[
            jax.ShapeDtypeStruct((bn, l, d), BF16),
            jax.ShapeDtypeStruct((bn, n_heads, LANES, LANES), F32),
        ],
        scratch_shapes=[pltpu.VMEM((n_heads, LANES, LANES), F32)],
        compiler_params=_cparams("parallel", "arbitrary"),
        name="hgrn2",
    )(proj, proj, proj, lb_param, norm_w.reshape(1, LANES), s0_t)
    return y, jnp.swapaxes(s_last, -1, -2)


def kernel(x_prompt, x_sample, cache_diff_k, cache_diff_v, state_rglru_conv, state_rglru_h, state_ssd_conv,
           state_ssd, state_hgrn, norm_w, l0_w_in, l0_conv_w, l0_conv_b, l0_rg_w_r, l0_rg_b_r, l0_rg_w_i,
           l0_rg_b_i, l0_rg_lambda, l0_lq1, l0_lk1, l0_lq2, l0_lk2, l0_subln_w, l0_w_out, l1_w_in, l1_conv_w,
           l1_conv_b, l1_dt_bias, l1_a_log, l1_d_skip, l1_ssd_norm_w, l1_hg_lower_bound, l1_hg_norm_w, l1_w_out,
           ffn_w_gate, ffn_w_up, ffn_w_down):
    d_model = x_prompt.shape[-1]
    d_rnn = l0_conv_w.shape[1]
    n_heads_b, dv_b = cache_diff_k.shape[2], cache_diff_k.shape[3]
    d_att = n_heads_b * dv_b
    n_heads_c, p_c, n_c = state_ssd.shape[1:]
    d_ssd = n_heads_c * p_c
    d_xbc = l1_conv_w.shape[1]
    d_bc = d_xbc - d_ssd
    n_grp = d_bc // (2 * n_c)
    n_heads_d, dk_d, dv_d = state_hgrn.shape[1:]
    d_hg = n_heads_d * dk_d
    assert dv_b == LANES and dk_d == LANES and dv_d == LANES and d_rnn == d_att == d_ssd == d_hg

    w_in0_main = l0_w_in[:, :2 * d_rnn + d_att].astype(BF16)
    w_in0_kv = l0_w_in[:, 2 * d_rnn + d_att:].astype(BF16)
    o1 = d_ssd
    o2 = o1 + d_xbc
    o3 = o2 + n_heads_c
    w1 = l1_w_in
    w_in1 = jnp.concatenate(
        [w1[:, :o1], w1[:, o1:o1 + d_ssd], w1[:, o3:], w1[:, o1 + d_ssd:o2],
         jnp.pad(w1[:, o2:o3], ((0, 0), (0, LANES - n_heads_c)))], axis=1).astype(BF16)
    w_out0 = l0_w_out.astype(BF16)
    w_out1 = l1_w_out.astype(BF16)
    wg = ffn_w_gate.astype(BF16)
    wu = ffn_w_up.astype(BF16)
    wd = ffn_w_down.astype(BF16)
    w_r = l0_rg_w_r.astype(BF16)
    w_i = l0_rg_w_i.astype(BF16)
    tn0 = 512
    tn1 = 1152
    assert w_in0_main.shape[1] % tn0 == 0 and w_in1.shape[1] % tn1 == 0
    ssd_cols = (0, 1, (2 * d_ssd + 3 * d_hg) // d_bc, (2 * d_ssd + 3 * d_hg + d_bc) // LANES)
    hg_cols = (2, 3, 4)

    def trunk(x, k_cache, v_cache, rg_conv, rg_h, ssd_conv, ssd_s, hg_s):
        bn, l, _ = x.shape
        m = bn * l
        x0 = x.reshape(m, d_model)
        proj0, xn0 = _norm_matmul(x0, norm_w[0, 0], w_in0_main, tn0)
        proj0 = proj0.reshape(bn, l, -1)
        k_new, v_new, k_bf, v_bf = _head_proj(xn0, w_in0_kv, n_heads_b)
        k_new = k_new.reshape(bn, l, n_heads_b, dv_b)
        v_new = v_new.reshape(bn, l, n_heads_b, dv_b)
        ya, rg_h_new = _rglru(proj0, rg_conv, rg_h, l0_conv_w, l0_conv_b, w_r, l0_rg_b_r, w_i, l0_rg_b_i,
                              l0_rg_lambda, d_rnn)
        yb = _diff_attention(proj0, 2 * d_rnn // LANES, k_bf.reshape(bn, l, d_att), v_bf.reshape(bn, l, d_att),
                             k_cache, v_cache, l0_lq1, l0_lk1, l0_lq2, l0_lk2, l0_subln_w, n_heads_b, 0)
        rg_conv_new = proj0[:, l - (CONV_W - 1):, :d_rnn]
        x1 = _out_proj(ya.reshape(m, d_rnn), yb.reshape(m, d_att), w_out0, x0, norm_w[0, 1])
        x2 = _ffn(x1, norm_w[0, 2], norm_w[0, 3], wg[0], wu[0], wd[0])
        proj1 = _norm_matmul(x2, norm_w[1, 0], w_in1, tn1)[0].reshape(bn, l, -1)
        ys, ssd_s_new = _ssd(proj1, ssd_cols, ssd_conv, ssd_s, l1_conv_w, l1_conv_b, l1_dt_bias, l1_a_log,
                             l1_d_skip, l1_ssd_norm_w, n_grp, n_c, p_c)
        yh, hg_s_new = _hgrn(proj1, hg_cols, hg_s, l1_hg_lower_bound, l1_hg_norm_w, n_heads_d)
        tail = proj1[:, l - (CONV_W - 1):, :]
        ssd_conv_new = jnp.concatenate(
            [tail[:, :, d_ssd:2 * d_ssd], tail[:, :, 2 * d_ssd + 3 * d_hg:2 * d_ssd + 3 * d_hg + d_bc]], axis=-1)
        x3 = _out_proj(ys.reshape(m, d_ssd), yh.reshape(m, d_hg), w_out1, x2, norm_w[1, 1])
        x4 = _ffn(x3, norm_w[1, 2], norm_w[1, 3], wg[1], wu[1], wd[1])
        return (x4.reshape(bn, l, d_model), k_new, v_new, rg_conv_new, rg_h_new, ssd_conv_new, ssd_s_new, hg_s_new)

    bp = x_prompt.shape[0]
    zeros = lambda *s: jnp.zeros(s, F32)
    outs_p = trunk(x_prompt, None, None, zeros(bp, CONV_W - 1, d_rnn), zeros(bp, d_rnn),
                   zeros(bp, CONV_W - 1, d_xbc), zeros(bp, n_heads_c, p_c, n_c), zeros(bp, n_heads_d, dk_d, dv_d))
    outs_s = trunk(x_sample, cache_diff_k, cache_diff_v, state_rglru_conv, state_rglru_h, state_ssd_conv,
                   state_ssd, state_hgrn)
    return (outs_p[0], outs_s[0]) + tuple(outs_p[1:]) + tuple(outs_s[1:])
```

```python
import functools
import math

import jax
import jax.numpy as jnp
from jax import lax
from jax.experimental import pallas as pl
from jax.experimental.pallas import tpu as pltpu

F32 = jnp.float32
BF16 = jnp.bfloat16

EPS = 1e-6
CHUNK = 64
CONV_W = 4
RG_C = 8.0
LANES = 128
SUBLANES = 8
HG_SUB = 16
NEG = -1e30
VMEM_LIMIT_BYTES = 56 * 1024 * 1024

_NT = (((1,), (1,)), ((), ()))
_TN = (((0,), (0,)), ((), ()))


def _cparams(*sem):
    return pltpu.CompilerParams(dimension_semantics=sem, vmem_limit_bytes=VMEM_LIMIT_BYTES)


def _sigmoid(x):
    return 1.0 / (1.0 + jnp.exp(-x))


def _silu(x):
    return x * _sigmoid(x)


def _softplus(x):
    return jnp.maximum(x, 0.0) + jnp.log1p(jnp.exp(-jnp.abs(x)))


def _neg_expm1(x):
    return -jnp.tanh(0.5 * x) * (jnp.exp(x) + 1.0)


def _gelu_tanh(x):
    c = math.sqrt(2.0 / math.pi)
    return x * (0.5 * (1.0 + jnp.tanh(c * (x + 0.044715 * (x * x * x)))))


def _rms(x, w):
    return x * lax.rsqrt(jnp.mean(x * x, axis=-1, keepdims=True) + EPS) * w


def _dot(a, b):
    return jnp.dot(a, b, preferred_element_type=F32)


def _cumsum_rows(x):
    n = x.shape[0]
    row = lax.broadcasted_iota(jnp.int32, x.shape, 0)
    s = 1
    while s < n:
        x = x + jnp.where(row >= s, pltpu.roll(x, s, axis=0), 0.0)
        s *= 2
    return x


def _norm_matmul_kernel(x_ref, nw_ref, w_ref, o_ref, xn_ref):
    @pl.when(pl.program_id(1) == 0)
    def _():
        xn_ref[...] = _rms(x_ref[...], nw_ref[...]).astype(BF16)

    o_ref[...] = _dot(xn_ref[...], w_ref[...])


def _norm_matmul(x, nw, w, tn, emit_xn):
    m, d = x.shape
    n = w.shape[1]
    assert n % tn == 0
    tm = min(1024, m)
    out_specs = [pl.BlockSpec((tm, tn), lambda i, j: (i, j))]
    out_shape = [jax.ShapeDtypeStruct((m, n), F32)]
    scratch = []
    if emit_xn:
        out_specs.append(pl.BlockSpec((tm, d), lambda i, j: (i, 0)))
        out_shape.append(jax.ShapeDtypeStruct((m, d), BF16))
    else:
        scratch.append(pltpu.VMEM((tm, d), BF16))
    return pl.pallas_call(
        _norm_matmul_kernel,
        grid=(m // tm, n // tn),
        in_specs=[
            pl.BlockSpec((tm, d), lambda i, j: (i, 0)),
            pl.BlockSpec((1, d), lambda i, j: (0, 0)),
            pl.BlockSpec((d, tn), lambda i, j: (0, j)),
        ],
        out_specs=out_specs,
        out_shape=out_shape,
        scratch_shapes=scratch,
        compiler_params=_cparams("parallel", "arbitrary"),
        name="norm_in_proj",
    )(x, nw.reshape(1, d), w)


def _head_proj_kernel(xn_ref, w_ref, *o_refs):
    n_out = len(o_refs) // 2
    j = pl.program_id(1)
    for k in range(n_out):
        @pl.when(j == k)
        def _(k=k):
            o_ref, ob_ref = o_refs[k], o_refs[n_out + k]
            res = _dot(xn_ref[...], w_ref[...])
            for h in range(o_ref.shape[1]):
                o_ref[:, h, :] = res[:, h * LANES:(h + 1) * LANES]
            ob_ref[...] = res.astype(BF16)


def _head_proj(xn, w, n_heads):
    m, d = xn.shape
    width = n_heads * LANES
    n_out = w.shape[1] // width
    tm = min(1024, m)
    return pl.pallas_call(
        _head_proj_kernel,
        grid=(m // tm, n_out),
        in_specs=[
            pl.BlockSpec((tm, d), lambda i, j: (i, 0)),
            pl.BlockSpec((d, width), lambda i, j: (0, j)),
        ],
        out_specs=([pl.BlockSpec((tm, n_heads, LANES), lambda i, j: (i, 0, 0)) for _ in range(n_out)]
                   + [pl.BlockSpec((tm, width), lambda i, j: (i, 0)) for _ in range(n_out)]),
        out_shape=([jax.ShapeDtypeStruct((m, n_heads, LANES), F32) for _ in range(n_out)]
                   + [jax.ShapeDtypeStruct((m, width), BF16) for _ in range(n_out)]),
        compiler_params=_cparams("parallel", "arbitrary"),
        name="head_proj",
    )(xn, w)


def _out_proj_kernel(ya_ref, yb_ref, wa_ref, wb_ref, x_ref, nw_ref, o_ref):
    m = _dot(ya_ref[...], wa_ref[...]) + _dot(yb_ref[...], wb_ref[...])
    o_ref[...] = x_ref[...] + _rms(m, nw_ref[...])


def _out_proj(ya, yb, w_out, x, nw):
    m, d = x.shape
    da, db = ya.shape[1], yb.shape[1]
    tm = min(512, m)
    return pl.pallas_call(
        _out_proj_kernel,
        grid=(m // tm,),
        in_specs=[
            pl.BlockSpec((tm, da), lambda i: (i, 0)),
            pl.BlockSpec((tm, db), lambda i: (i, 0)),
            pl.BlockSpec((da, d), lambda i: (0, 0)),
            pl.BlockSpec((db, d), lambda i: (0, 0)),
            pl.BlockSpec((tm, d), lambda i: (i, 0)),
            pl.BlockSpec((1, d), lambda i: (0, 0)),
        ],
        out_specs=pl.BlockSpec((tm, d), lambda i: (i, 0)),
        out_shape=jax.ShapeDtypeStruct((m, d), F32),
        compiler_params=_cparams("parallel"),
        name="out_proj",
    )(ya, yb, w_out[:da], w_out[da:], x, nw.reshape(1, d))


def _ffn_kernel(x_ref, nwa_ref, nwb_ref, wg_ref, wu_ref, wd_ref, o_ref, hn_ref, acc_ref):
    f = pl.program_id(1)

    @pl.when(f == 0)
    def _():
        hn_ref[...] = _rms(x_ref[...], nwa_ref[...]).astype(BF16)
        acc_ref[...] = jnp.zeros_like(acc_ref)

    hn = hn_ref[...]
    g = _dot(hn, wg_ref[...])
    u = _dot(hn, wu_ref[...])
    a = (_silu(g) * u).astype(BF16)
    acc_ref[...] += _dot(a, wd_ref[...])

    @pl.when(f == pl.num_programs(1) - 1)
    def _():
        o_ref[...] = x_ref[...] + _rms(acc_ref[...], nwb_ref[...])


def _ffn(x, nw_pre, nw_post, wg, wu, wd):
    m, d = x.shape
    dff = wg.shape[1]
    tm = min(512, m)
    tf = 512
    return pl.pallas_call(
        _ffn_kernel,
        grid=(m // tm, dff // tf),
        in_specs=[
            pl.BlockSpec((tm, d), lambda i, f: (i, 0)),
            pl.BlockSpec((1, d), lambda i, f: (0, 0)),
            pl.BlockSpec((1, d), lambda i, f: (0, 0)),
            pl.BlockSpec((d, tf), lambda i, f: (0, f)),
            pl.BlockSpec((d, tf), lambda i, f: (0, f)),
            pl.BlockSpec((tf, d), lambda i, f: (f, 0)),
        ],
        out_specs=pl.BlockSpec((tm, d), lambda i, f: (i, 0)),
        out_shape=jax.ShapeDtypeStruct((m, d), F32),
        scratch_shapes=[pltpu.VMEM((tm, d), BF16), pltpu.VMEM((tm, d), F32)],
        compiler_params=_cparams("parallel", "arbitrary"),
        name="ffn",
    )(x, nw_pre.reshape(1, d), nw_post.reshape(1, d), wg, wu, wd)


def _conv_init(prev_ref, conv_buf):
    prev_ref[...] = jnp.zeros_like(prev_ref)
    prev_ref[SUBLANES - (CONV_W - 1):SUBLANES, :] = conv_buf


def _conv_tile(prev_ref, x, cw_ref, cb_ref):
    tl = x.shape[0]
    prev = prev_ref[...]
    row = lax.broadcasted_iota(jnp.int32, prev.shape, 0)
    y = cb_ref[...] + x * cw_ref[CONV_W - 1:CONV_W, :]
    for k in range(1, CONV_W):
        xr = pltpu.roll(x, k, axis=0)
        head = jnp.where(row < k, pltpu.roll(prev, k, axis=0), xr[0:SUBLANES])
        xk = head if tl == SUBLANES else jnp.concatenate([head, xr[SUBLANES:]], axis=0)
        y = y + xk * cw_ref[CONV_W - 1 - k:CONV_W - k, :]
    prev_ref[...] = x[tl - SUBLANES:tl]
    return y


def _rglru_kernel(xa_ref, ga_ref, cbuf_ref, h0_ref, cw_ref, cb_ref, wr_ref, br_ref, wi_ref, bi_ref,
                  lam_ref, ya_ref, hlast_ref, xs_ref, a_ref, u_ref, h_ref, *, tl, n_blk):
    t = pl.program_id(1)

    @pl.when(t == 0)
    def _():
        _conv_init(xs_ref, cbuf_ref[...])
        h_ref[...] = h0_ref[...]

    xc = _conv_tile(xs_ref, xa_ref[...], cw_ref, cb_ref)
    xcb = xc.astype(BF16)
    r_pre = jnp.concatenate(
        [_dot(xcb[:, hb * LANES:(hb + 1) * LANES], wr_ref[hb]) for hb in range(n_blk)], axis=1)
    i_pre = jnp.concatenate(
        [_dot(xcb[:, hb * LANES:(hb + 1) * LANES], wi_ref[hb]) for hb in range(n_blk)], axis=1)
    r = _sigmoid(r_pre + br_ref[...])
    gi = _sigmoid(i_pre + bi_ref[...])
    log_a = -RG_C * r * _softplus(-lam_ref[...])
    a_ref[...] = jnp.exp(log_a)
    u_ref[...] = jnp.sqrt(_neg_expm1(2.0 * log_a)) * (gi * xc)

    row = lax.broadcasted_iota(jnp.int32, (SUBLANES, a_ref.shape[1]), 0)

    def group(gidx, h_prev):
        r0 = pl.multiple_of(gidx * SUBLANES, SUBLANES)
        ag = a_ref[pl.ds(r0, SUBLANES), :]
        ug = u_ref[pl.ds(r0, SUBLANES), :]
        s = 1
        while s < SUBLANES:
            a_sh = jnp.where(row >= s, pltpu.roll(ag, s, axis=0), 1.0)
            u_sh = jnp.where(row >= s, pltpu.roll(ug, s, axis=0), 0.0)
            ug = ag * u_sh + ug
            ag = ag * a_sh
            s *= 2
        hg = ag * h_prev + ug
        u_ref[pl.ds(r0, SUBLANES), :] = hg
        return hg[SUBLANES - 1:SUBLANES, :]

    h_last = lax.fori_loop(0, tl // SUBLANES, group, h_ref[...])
    h_ref[...] = h_last
    hlast_ref[...] = h_last
    ya_ref[...] = (u_ref[...] * _gelu_tanh(ga_ref[...])).astype(BF16)


def _rglru(proj, conv_buf, h0, conv_w, conv_b, w_r, b_r, w_i, b_i, lam, d_rnn):
    bn, l, _ = proj.shape
    tl = min(256, l)
    n_blk = w_r.shape[0]
    vec = lambda v: v.reshape(1, d_rnn)
    const2 = lambda b, t: (0, 0)
    kern = functools.partial(_rglru_kernel, tl=tl, n_blk=n_blk)
    ya, h_last = pl.pallas_call(
        kern,
        grid=(bn, l // tl),
        in_specs=[
            pl.BlockSpec((None, tl, d_rnn), lambda b, t: (b, t, 0)),
            pl.BlockSpec((None, tl, d_rnn), lambda b, t: (b, t, 1)),
            pl.BlockSpec((None, CONV_W - 1, d_rnn), lambda b, t: (b, 0, 0)),
            pl.BlockSpec((None, 1, d_rnn), lambda b, t: (b, 0, 0)),
            pl.BlockSpec((CONV_W, d_rnn), const2),
            pl.BlockSpec((1, d_rnn), const2),
            pl.BlockSpec(w_r.shape, lambda b, t: (0, 0, 0)),
            pl.BlockSpec((1, d_rnn), const2),
            pl.BlockSpec(w_i.shape, lambda b, t: (0, 0, 0)),
            pl.BlockSpec((1, d_rnn), const2),
            pl.BlockSpec((1, d_rnn), const2),
        ],
        out_specs=[
            pl.BlockSpec((None, tl, d_rnn), lambda b, t: (b, t, 0)),
            pl.BlockSpec((None, 1, d_rnn), lambda b, t: (b, 0, 0)),
        ],
        out_shape=[
            jax.ShapeDtypeStruct((bn, l, d_rnn), BF16),
            jax.ShapeDtypeStruct((bn, 1, d_rnn), F32),
        ],
        scratch_shapes=[
            pltpu.VMEM((SUBLANES, d_rnn), F32),
            pltpu.VMEM((tl, d_rnn), F32),
            pltpu.VMEM((tl, d_rnn), F32),
            pltpu.VMEM((1, d_rnn), F32),
        ],
        compiler_params=_cparams("parallel", "arbitrary"),
        name="rglru",
    )(proj, proj, conv_buf, h0.reshape(bn, 1, d_rnn), conv_w, vec(conv_b), w_r, vec(b_r), w_i, vec(b_i),
      vec(lam))
    return ya, h_last.reshape(bn, d_rnn)


def _chunk_mask(kpos, qpos):
    shift = CHUNK.bit_length() - 1
    return jnp.right_shift(kpos, shift) <= jnp.right_shift(qpos, shift)


def _diff_lambda(lq1_ref, lk1_ref, lq2_ref, lk2_ref, lam_init):
    return (jnp.exp(jnp.sum(lq1_ref[...] * lk1_ref[...], axis=-1, keepdims=True))
            - jnp.exp(jnp.sum(lq2_ref[...] * lk2_ref[...], axis=-1, keepdims=True)) + lam_init)


ONES_ROWS = 16


def _attn_prompt_kernel(q_ref, k_ref, v_ref, lq1_ref, lk1_ref, lq2_ref, lk2_ref, sw_ref, o_ref,
                        vt_ref, sa_ref, sb_ref, m_ref, acc_ref, *, t, nblk, dh, scale, lam_init):
    i = pl.program_id(2)
    dv = 2 * dh

    @pl.when(i == 0)
    def _():
        ones = jnp.ones((ONES_ROWS, t), BF16)
        for jb in range(nblk):
            vt_ref[jb, :dv, :] = jnp.transpose(v_ref[jb * t:(jb + 1) * t, :].astype(F32)).astype(BF16)
            vt_ref[jb, dv:, :] = ones

    qt = jnp.transpose(q_ref[...] * (scale * math.log2(math.e)))
    sub = lax.broadcasted_iota(jnp.int32, qt.shape, 0)
    qt2 = jnp.concatenate([jnp.where(sub < dh, qt, 0.0), jnp.where(sub >= dh, qt, 0.0)],
                          axis=1).astype(BF16)
    m_ref[...] = jnp.full_like(m_ref, NEG)
    acc_ref[...] = jnp.zeros_like(acc_ref)

    def scores(j, s_ref):
        s_ref[...] = _dot(k_ref[pl.ds(pl.multiple_of(j * t, t), t), :], qt2)

    def update(j, s_ref, diagonal):
        if diagonal:
            qoff = lax.broadcasted_iota(jnp.int32, (1, 2 * t), 1)
            qoff = jnp.where(qoff >= t, qoff - t, qoff)
            s = jnp.concatenate(
                [s_ref[c * CHUNK:(c + 1) * CHUNK, :] + jnp.where(qoff >= c * CHUNK, 0.0, NEG)
                 for c in range(t // CHUNK)], axis=0)
        else:
            s = s_ref[...]
        m_prev = m_ref[...]
        m_new = jnp.maximum(m_prev, jnp.max(s, axis=0, keepdims=True))
        alpha = jnp.exp2(m_prev - m_new)
        p = jnp.exp2(s - m_new).astype(BF16)
        acc_ref[...] = alpha * acc_ref[...] + _dot(vt_ref[j], p)
        m_ref[...] = m_new

    scores(0, sa_ref)

    def pair(pidx, carry):
        j = 2 * pidx
        scores(j + 1, sb_ref)
        update(j, sa_ref, False)
        scores(j + 2, sa_ref)
        update(j + 1, sb_ref, False)
        return carry

    lax.fori_loop(0, lax.shift_right_logical(i, 1), pair, 0)

    @pl.when((i & 1) == 1)
    def _():
        scores(i, sb_ref)
        update(i - 1, sa_ref, False)
        sa_ref[...] = sb_ref[...]

    update(i, sa_ref, True)

    lam = _diff_lambda(lq1_ref, lk1_ref, lq2_ref, lk2_ref, lam_init)
    acc = acc_ref[0:dv, :]
    l = acc_ref[dv:dv + 1, :]
    o_t = acc[:, :t] / l[:, :t] - lam * (acc[:, t:] / l[:, t:])
    o_ref[...] = (_rms(jnp.transpose(o_t), sw_ref[...]) * (1.0 - lam_init)).astype(BF16)


def _attn_cached_kernel(q_ref, kc_ref, vc_ref, kn_ref, vn_ref, lq1_ref, lk1_ref, lq2_ref, lk2_ref, sw_ref,
                        o_ref, qt_ref, m_ref, acc_ref, *, n_heads, lq, n_cache, pos0, dh, scale, lam_init):
    j = pl.program_id(1)
    dv = 2 * dh
    hsl = lambda h: slice(h * LANES, (h + 1) * LANES)

    @pl.when(j == 0)
    def _():
        sub = lax.broadcasted_iota(jnp.int32, (dv, lq), 0)
        for h in range(n_heads):
            qt = jnp.transpose(q_ref[:, hsl(h)] * (scale * math.log2(math.e)))
            qt_ref[h] = jnp.concatenate([jnp.where(sub < dh, qt, 0.0), jnp.where(sub >= dh, qt, 0.0)],
                                        axis=1).astype(BF16)
        m_ref[...] = jnp.full_like(m_ref, NEG)
        acc_ref[...] = jnp.zeros_like(acc_ref)

    def update(k_ref, v_ref, bias):
        tk = k_ref.shape[0]
        ones = jnp.ones((ONES_ROWS, tk), BF16)
        for h in range(n_heads):
            s = _dot(k_ref[:, hsl(h)], qt_ref[h])
            if bias is not None:
                s = s + bias
            m_prev = m_ref[h]
            m_new = jnp.maximum(m_prev, jnp.max(s, axis=0, keepdims=True))
            alpha = jnp.exp2(m_prev - m_new)
            p = jnp.exp2(s - m_new).astype(BF16)
            vt = jnp.transpose(v_ref[:, hsl(h)].astype(F32)).astype(BF16)
            acc_ref[h] = alpha * acc_ref[h] + _dot(jnp.concatenate([vt, ones], axis=0), p)
            m_ref[h] = m_new

    @pl.when(j < n_cache)
    def _():
        update(kc_ref, vc_ref, None)

    @pl.when(j == n_cache)
    def _():
        col = lax.broadcasted_iota(jnp.int32, (lq, 2 * lq), 1)
        qpos = pos0 + jnp.where(col >= lq, col - lq, col)
        kpos = pos0 + lax.broadcasted_iota(jnp.int32, (lq, 2 * lq), 0)
        update(kn_ref, vn_ref, jnp.where(_chunk_mask(kpos, qpos), 0.0, NEG))
        lam = _diff_lambda(lq1_ref, lk1_ref, lq2_ref, lk2_ref, lam_init)
        outs = []
        for h in range(n_heads):
            acc = acc_ref[h, 0:dv, :]
            l = acc_ref[h, dv:dv + 1, :]
            o_t = acc[:, :lq] / l[:, :lq] - lam * (acc[:, lq:] / l[:, lq:])
            outs.append(_rms(jnp.transpose(o_t), sw_ref[...]) * (1.0 - lam_init))
        o_ref[...] = jnp.concatenate(outs, axis=1).astype(BF16)


def _diff_attention(qsrc, q_col, k, v, k_cache, v_cache, lq1, lk1, lq2, lk2, subln_w, n_heads, layer):
    bn, l = k.shape[:2]
    dv = LANES
    dh = dv // 2
    d_att = n_heads * dv
    lam_init = 0.8 - 0.6 * math.exp(-0.3 * layer)
    scale = dh ** -0.5
    params = [lq1.reshape(1, dh), lk1.reshape(1, dh), lq2.reshape(1, dh), lk2.reshape(1, dh),
              subln_w.reshape(1, dv)]
    if k_cache is None:
        t = min(512, l)
        nblk = l // t
        assert t % CHUNK == 0 and l % t == 0
        small = lambda n: pl.BlockSpec((1, n), lambda b, h, i: (0, 0))
        kern = functools.partial(_attn_prompt_kernel, t=t, nblk=nblk, dh=dh, scale=scale, lam_init=lam_init)
        return pl.pallas_call(
            kern,
            grid=(bn, n_heads, nblk),
            in_specs=[
                pl.BlockSpec((None, t, dv), lambda b, h, i: (b, i, q_col + h)),
                pl.BlockSpec((None, l, dv), lambda b, h, i: (b, 0, h)),
                pl.BlockSpec((None, l, dv), lambda b, h, i: (b, 0, h)),
                small(dh), small(dh), small(dh), small(dh), small(dv),
            ],
            out_specs=pl.BlockSpec((None, t, dv), lambda b, h, i: (b, i, h)),
            out_shape=jax.ShapeDtypeStruct((bn, l, d_att), BF16),
            scratch_shapes=[
                pltpu.VMEM((nblk, dv + ONES_ROWS, t), BF16),
                pltpu.VMEM((t, 2 * t), F32),
                pltpu.VMEM((t, 2 * t), F32),
                pltpu.VMEM((1, 2 * t), F32),
                pltpu.VMEM((dv + ONES_ROWS, 2 * t), F32),
            ],
            compiler_params=_cparams("parallel", "parallel", "arbitrary"),
            name="diff_attention_prompt",
        )(qsrc, k, v, *params)

    pos0 = k_cache.shape[1]
    tkc = min(512, pos0)
    assert pos0 % CHUNK == 0 and pos0 % tkc == 0 and l % SUBLANES == 0
    n_cache = pos0 // tkc
    small = lambda n: pl.BlockSpec((1, n), lambda b, j: (0, 0))
    kern = functools.partial(_attn_cached_kernel, n_heads=n_heads, lq=l, n_cache=n_cache, pos0=pos0, dh=dh,
                             scale=scale, lam_init=lam_init)
    kc = k_cache.astype(BF16).reshape(bn, pos0, d_att)
    vc = v_cache.astype(BF16).reshape(bn, pos0, d_att)
    cache_spec = pl.BlockSpec((None, tkc, d_att), lambda b, j: (b, jnp.minimum(j, n_cache - 1), 0))
    new_spec = pl.BlockSpec((None, l, d_att), lambda b, j: (b, 0, 0))
    return pl.pallas_call(
        kern,
        grid=(bn, n_cache + 1),
        in_specs=[
            pl.BlockSpec((None, l, d_att), lambda b, j: (b, 0, q_col * dv // d_att)),
            cache_spec, cache_spec, new_spec, new_spec,
            small(dh), small(dh), small(dh), small(dh), small(dv),
        ],
        out_specs=pl.BlockSpec((None, l, d_att), lambda b, j: (b, 0, 0)),
        out_shape=jax.ShapeDtypeStruct((bn, l, d_att), BF16),
        scratch_shapes=[
            pltpu.VMEM((n_heads, dv, 2 * l), BF16),
            pltpu.VMEM((n_heads, 1, 2 * l), F32),
            pltpu.VMEM((n_heads, dv + ONES_ROWS, 2 * l), F32),
        ],
        compiler_params=_cparams("parallel", "arbitrary"),
        name="diff_attention_cached",
    )(qsrc, kc, vc, k, v, *params)


def _ssd_kernel(z_ref, x_ref, bc_ref, dt_ref, cbx_ref, cbbc_ref, s0_ref, cwx_ref, cbiasx_ref, cwbc_ref,
                cbiasbc_ref, dtb_ref, alog_ref, dskip_ref, nw_ref, expand_ref,
                y_ref, s_ref, xsx_ref, xsbc_ref, *, n_grp, hpg, p_dim, n_dim):
    t = pl.program_id(1)
    q = CHUNK
    d_ssd = n_grp * hpg * p_dim
    gw = hpg * p_dim

    @pl.when(t == 0)
    def _():
        _conv_init(xsx_ref, cbx_ref[...])
        _conv_init(xsbc_ref, cbbc_ref[...])
        s_ref[...] = s0_ref[...]

    xs = _silu(_conv_tile(xsx_ref, x_ref[...], cwx_ref, cbiasx_ref))
    bcs = _silu(_conv_tile(xsbc_ref, bc_ref[...], cwbc_ref, cbiasbc_ref))

    dt = _softplus(dt_ref[...] + dtb_ref[...])
    a = -jnp.exp(alog_ref[...])
    acum = _cumsum_rows(dt * a)
    a_last = acum[q - 1:q, :]
    wst = jnp.exp(a_last - acum) * dt
    eac = jnp.exp(acum)
    acum_t = jnp.transpose(acum)

    stacked = jnp.concatenate([dt, wst, eac], axis=0)
    hi = stacked.astype(BF16)
    rem = stacked - hi.astype(F32)
    mid = rem.astype(BF16)
    lo = (rem - mid.astype(F32)).astype(BF16)
    expand = expand_ref[...]
    expd = _dot(hi, expand) + _dot(mid, expand) + _dot(lo, expand)
    dt_e, wst_e, eac_e = expd[0:q], expd[q:2 * q], expd[2 * q:3 * q]

    xdt = (xs * dt_e).astype(BF16)
    xw = (xs * wst_e).astype(BF16)
    row = lax.broadcasted_iota(jnp.int32, (q, q), 0)
    col = lax.broadcasted_iota(jnp.int32, (q, q), 1)
    causal = col <= row
    lane = lax.broadcasted_iota(jnp.int32, (q, LANES), 1)
    heads_per_tile = LANES // p_dim

    bgs = [bcs[:, g * n_dim:(g + 1) * n_dim] for g in range(n_grp)]
    cgs = [bcs[:, (n_grp + g) * n_dim:(n_grp + g + 1) * n_dim].astype(BF16) for g in range(n_grp)]
    cbs = [lax.dot_general(cgs[g], bgs[g].astype(BF16), _NT, preferred_element_type=F32) for g in range(n_grp)]
    bg_ts = [jnp.transpose(bgs[g]).astype(BF16) for g in range(n_grp)]
    wdec = []
    for h in range(n_grp * hpg):
        dec = jnp.exp(jnp.where(causal, acum[:, h:h + 1] - acum_t[h:h + 1, :], -jnp.inf))
        wdec.append((cbs[h // hpg] * dec).astype(BF16))

    y_parts = []
    for g in range(n_grp):
        s_prev = s_ref[g]
        y_off = _dot(cgs[g], s_prev.astype(BF16)) * eac_e[:, g * gw:(g + 1) * gw]
        y_diag = []
        for tile in range(gw // LANES):
            c0 = g * gw + tile * LANES
            x_tile = xdt[:, c0:c0 + LANES]
            acc = None
            for hh in range(heads_per_tile):
                yh = _dot(wdec[c0 // p_dim + hh], x_tile)
                sel = jnp.logical_and(lane >= hh * p_dim, lane < (hh + 1) * p_dim)
                acc = jnp.where(sel, yh, 0.0) if acc is None else jnp.where(sel, yh, acc)
            y_diag.append(acc)
        y_parts.append(jnp.concatenate(y_diag, axis=1) + y_off)
        st = _dot(bg_ts[g], xw[:, g * gw:(g + 1) * gw])
        s_ref[g] = eac_e[q - 1:q, g * gw:(g + 1) * gw] * s_prev + st
    y = jnp.concatenate(y_parts, axis=1)
    y = (y + dskip_ref[...] * xs) * _silu(z_ref[...])
    nw = nw_ref[...]
    y_ref[...] = jnp.concatenate(
        [_rms(y[:, g * gw:(g + 1) * gw], nw[:, g * gw:(g + 1) * gw]) for g in range(n_grp)],
        axis=1).astype(BF16)


def _ssd(proj, cols, conv_buf, s0, conv_w, conv_b, dt_bias, a_log, d_skip, norm_w, n_grp, n_dim, p_dim):
    bn, l, _ = proj.shape
    n_heads = dt_bias.shape[0]
    hpg = n_heads // n_grp
    d_ssd = n_heads * p_dim
    d_bc = 2 * n_grp * n_dim
    gw = hpg * p_dim
    q = CHUNK
    assert l % q == 0 and n_heads <= LANES and LANES % p_dim == 0
    z_col, x_col, bc_col, dt_col = cols
    pad = lambda v: jnp.pad(v.astype(F32), (0, LANES - n_heads)).reshape(1, LANES)
    expand = (jnp.arange(LANES)[:, None] == (jnp.arange(d_ssd)[None, :] // p_dim)).astype(BF16)
    s0_t = s0.reshape(bn, n_grp, hpg, p_dim, n_dim).transpose(0, 1, 4, 2, 3).reshape(bn, n_grp, n_dim, gw)
    const2 = lambda b, t: (0, 0)
    kern = functools.partial(_ssd_kernel, n_grp=n_grp, hpg=hpg, p_dim=p_dim, n_dim=n_dim)
    y, s_last = pl.pallas_call(
        kern,
        grid=(bn, l // q),
        in_specs=[
            pl.BlockSpec((None, q, d_ssd), lambda b, t: (b, t, z_col)),
            pl.BlockSpec((None, q, d_ssd), lambda b, t: (b, t, x_col)),
            pl.BlockSpec((None, q, d_bc), lambda b, t: (b, t, bc_col)),
            pl.BlockSpec((None, q, LANES), lambda b, t: (b, t, dt_col)),
            pl.BlockSpec((None, CONV_W - 1, d_ssd), lambda b, t: (b, 0, 0)),
            pl.BlockSpec((None, CONV_W - 1, d_bc), lambda b, t: (b, 0, 0)),
            pl.BlockSpec((None, n_grp, n_dim, gw), lambda b, t: (b, 0, 0, 0)),
            pl.BlockSpec((CONV_W, d_ssd), const2),
            pl.BlockSpec((1, d_ssd), const2),
            pl.BlockSpec((CONV_W, d_bc), const2),
            pl.BlockSpec((1, d_bc), const2),
            pl.BlockSpec((1, LANES), const2),
            pl.BlockSpec((1, LANES), const2),
            pl.BlockSpec((1, d_ssd), const2),
            pl.BlockSpec((1, d_ssd), const2),
            pl.BlockSpec((LANES, d_ssd), const2),
        ],
        out_specs=[
            pl.BlockSpec((None, q, d_ssd), lambda b, t: (b, t, 0)),
            pl.BlockSpec((None, n_grp, n_dim, gw), lambda b, t: (b, 0, 0, 0)),
        ],
        out_shape=[
            jax.ShapeDtypeStruct((bn, l, d_ssd), BF16),
            jax.ShapeDtypeStruct((bn, n_grp, n_dim, gw), F32),
        ],
        scratch_shapes=[
            pltpu.VMEM((SUBLANES, d_ssd), F32),
            pltpu.VMEM((SUBLANES, d_bc), F32),
        ],
        compiler_params=_cparams("parallel", "arbitrary"),
        name="ssd",
    )(proj, proj, proj, proj, conv_buf[:, :, :d_ssd], conv_buf[:, :, d_ssd:], s0_t,
      conv_w[:, :d_ssd], conv_b[:d_ssd].reshape(1, d_ssd), conv_w[:, d_ssd:], conv_b[d_ssd:].reshape(1, d_bc),
      pad(dt_bias), pad(a_log), jnp.repeat(d_skip.astype(F32), p_dim).reshape(1, d_ssd),
      norm_w.reshape(1, d_ssd), expand)
    s_last = s_last.reshape(bn, n_grp, n_dim, hpg, p_dim).transpose(0, 1, 3, 4, 2).reshape(bn, n_heads, p_dim, n_dim)
    return y, s_last


def _hgrn_kernel(q_ref, f_ref, i_ref, lb_ref, nw_ref, s0_ref, y_ref, slast_ref, *, n_heads):
    t = pl.program_id(1)
    c = CHUNK
    nsb = c // HG_SUB

    @pl.when(t == 0)
    def _():
        slast_ref[...] = s0_ref[...]

    lbp = lb_ref[...]
    e = jnp.exp(lbp - jnp.max(lbp, axis=0, keepdims=True))
    lb = e[1:2, :] / jnp.sum(e, axis=0, keepdims=True)
    g = lb + (1.0 - lb) * _sigmoid(f_ref[...])
    logg = jnp.log(g)
    kk = 1.0 - g
    qq = _silu(q_ref[...])
    b = _cumsum_rows(logg)
    iv = i_ref[...]

    zero_blk = jnp.zeros((HG_SUB, LANES), F32)
    r2 = lax.broadcasted_iota(jnp.int32, (c, c), 0)
    c2 = lax.broadcasted_iota(jnp.int32, (c, c), 1)
    causal = c2 <= r2
    nw = nw_ref[...]

    b_last = b[c - 1:c, :]
    qe = (qq * jnp.exp(b)).astype(BF16)
    kd = (kk * jnp.exp(b_last - b)).astype(BF16)
    s_decay = jnp.exp(b_last)
    ivb = iv.astype(BF16)
    sb = lambda x, r: x[r * HG_SUB:(r + 1) * HG_SUB, :]
    m_off = [b[a * HG_SUB - 1:a * HG_SUB, :] for a in range(1, nsb)]
    m_mid = [b[r * HG_SUB + HG_SUB // 2 - 1:r * HG_SUB + HG_SUB // 2, :] for r in range(nsb)]
    q_off = [None] + [sb(qq, a) * jnp.exp(sb(b, a) - m_off[a - 1]) for a in range(1, nsb)]
    q_mid = [sb(qq, r) * jnp.exp(sb(b, r) - m_mid[r]) for r in range(nsb)]
    k_mid = [sb(kk, r) * jnp.exp(m_mid[r] - sb(b, r)) for r in range(nsb)]
    k_off = [[sb(kk, r) * jnp.exp(m_off[a - 1] - sb(b, r)) if a > r else None for a in range(1, nsb)]
             for r in range(nsb)]

    atts = []
    for h in range(n_heads):
        sl = slice(h * LANES, (h + 1) * LANES)
        q_rows, k_rows = [], []
        for r in range(nsb):
            q_parts = [q_off[a][:, sl] if a == r else zero_blk for a in range(1, nsb)]
            k_parts = [k_off[r][a - 1][:, sl] if a > r else zero_blk for a in range(1, nsb)]
            q_parts += [q_mid[r][:, sl] if d == r else zero_blk for d in range(nsb)]
            k_parts += [k_mid[r][:, sl] if d == r else zero_blk for d in range(nsb)]
            q_rows.append(jnp.concatenate(q_parts, axis=1))
            k_rows.append(jnp.concatenate(k_parts, axis=1))
        qcat = jnp.concatenate(q_rows, axis=0).astype(BF16)
        kcat = jnp.concatenate(k_rows, axis=0).astype(BF16)
        att = lax.dot_general(qcat, kcat, _NT, preferred_element_type=F32)
        atts.append(jnp.where(causal, att, 0.0).astype(BF16))
    outs = []
    for h in range(n_heads):
        sl = slice(h * LANES, (h + 1) * LANES)
        s_t = slast_ref[h]
        o = _dot(atts[h], ivb[:, sl])
        o = o + lax.dot_general(qe[:, sl], s_t.astype(BF16), _NT, preferred_element_type=F32)
        slast_ref[h] = s_decay[:, sl] * s_t + _dot(jnp.transpose(iv[:, sl]).astype(BF16), kd[:, sl])
        outs.append(_rms(o, nw))
    y_ref[...] = jnp.concatenate(outs, axis=1).astype(BF16)


def _hgrn(proj, cols, s0, lb_param, norm_w, n_heads):
    bn, l, _ = proj.shape
    d = n_heads * LANES
    c = CHUNK
    assert l % c == 0
    q_col, f_col, i_col = cols
    depth = lb_param.shape[0]
    assert depth == 2
    s0_t = jnp.swapaxes(s0, -1, -2)
    kern = functools.partial(_hgrn_kernel, n_heads=n_heads)
    y, s_last = pl.pallas_call(
        kern,
        grid=(bn, l // c),
        in_specs=[
            pl.BlockSpec((None, c, d), lambda b, t: (b, t, q_col)),
            pl.BlockSpec((None, c, d), lambda b, t: (b, t, f_col)),
            pl.BlockSpec((None, c, d), lambda b, t: (b, t, i_col)),
            pl.BlockSpec((depth, d), lambda b, t: (0, 0)),
            pl.BlockSpec((1, LANES), lambda b, t: (0, 0)),
            pl.BlockSpec((None, n_heads, LANES, LANES), lambda b, t: (b, 0, 0, 0)),
        ],
        out_specs=[
            pl.BlockSpec((None, c, d), lambda b, t: (b, t, 0)),
            pl.BlockSpec((None, n_heads, LANES, LANES), lambda b, t: (b, 0, 0, 0)),
        ],
        out_shape=[
            jax.ShapeDtypeStruct((bn, l, d), BF16),
            jax.ShapeDtypeStruct((bn, n_heads, LANES, LANES), F32),
        ],
        compiler_params=_cparams("parallel", "arbitrary"),
        name="hgrn2",
    )(proj, proj, proj, lb_param, norm_w.reshape(1, LANES), s0_t)
    return y, jnp.swapaxes(s_last, -1, -2)


def kernel(x_prompt, x_sample, cache_diff_k, cache_diff_v, state_rglru_conv, state_rglru_h, state_ssd_conv,
           state_ssd, state_hgrn, norm_w, l0_w_in, l0_conv_w, l0_conv_b, l0_rg_w_r, l0_rg_b_r, l0_rg_w_i,
           l0_rg_b_i, l0_rg_lambda, l0_lq1, l0_lk1, l0_lq2, l0_lk2, l0_subln_w, l0_w_out, l1_w_in, l1_conv_w,
           l1_conv_b, l1_dt_bias, l1_a_log, l1_d_skip, l1_ssd_norm_w, l1_hg_lower_bound, l1_hg_norm_w, l1_w_out,
           ffn_w_gate, ffn_w_up, ffn_w_down):
    d_model = x_prompt.shape[-1]
    d_rnn = l0_conv_w.shape[1]
    n_heads_b, dv_b = cache_diff_k.shape[2], cache_diff_k.shape[3]
    d_att = n_heads_b * dv_b
    n_heads_c, p_c, n_c = state_ssd.shape[1:]
    d_ssd = n_heads_c * p_c
    d_xbc = l1_conv_w.shape[1]
    d_bc = d_xbc - d_ssd
    n_grp = d_bc // (2 * n_c)
    n_heads_d, dk_d, dv_d = state_hgrn.shape[1:]
    d_hg = n_heads_d * dk_d
    assert dv_b == LANES and dk_d == LANES and dv_d == LANES and d_rnn == d_att == d_ssd == d_hg

    w_in0_main = l0_w_in[:, :2 * d_rnn + d_att].astype(BF16)
    w_in0_kv = l0_w_in[:, 2 * d_rnn + d_att:].astype(BF16)
    o1 = d_ssd
    o2 = o1 + d_xbc
    o3 = o2 + n_heads_c
    w1 = l1_w_in
    w_in1 = jnp.concatenate(
        [w1[:, :o1], w1[:, o1:o1 + d_ssd], w1[:, o3:], w1[:, o1 + d_ssd:o2],
         jnp.pad(w1[:, o2:o3], ((0, 0), (0, LANES - n_heads_c)))], axis=1).astype(BF16)
    w_out0 = l0_w_out.astype(BF16)
    w_out1 = l1_w_out.astype(BF16)
    wg = ffn_w_gate.astype(BF16)
    wu = ffn_w_up.astype(BF16)
    wd = ffn_w_down.astype(BF16)
    w_r = l0_rg_w_r.astype(BF16)
    w_i = l0_rg_w_i.astype(BF16)
    tn0 = 512
    tn1 = 1152
    assert w_in0_main.shape[1] % tn0 == 0 and w_in1.shape[1] % tn1 == 0
    ssd_cols = (0, 1, (2 * d_ssd + 3 * d_hg) // d_bc, (2 * d_ssd + 3 * d_hg + d_bc) // LANES)
    hg_cols = (2, 3, 4)

    def trunk(x, k_cache, v_cache, rg_conv, rg_h, ssd_conv, ssd_s, hg_s):
        bn, l, _ = x.shape
        m = bn * l
        x0 = x.reshape(m, d_model)
        proj0, xn0 = _norm_matmul(x0, norm_w[0, 0], w_in0_main, tn0, True)
        proj0 = proj0.reshape(bn, l, -1)
        k_new, v_new, k_bf, v_bf = _head_proj(xn0, w_in0_kv, n_heads_b)
        k_new = k_new.reshape(bn, l, n_heads_b, dv_b)
        v_new = v_new.reshape(bn, l, n_heads_b, dv_b)
        ya, rg_h_new = _rglru(proj0, rg_conv, rg_h, l0_conv_w, l0_conv_b, w_r, l0_rg_b_r, w_i, l0_rg_b_i,
                              l0_rg_lambda, d_rnn)
        yb = _diff_attention(proj0, 2 * d_rnn // LANES, k_bf.reshape(bn, l, d_att), v_bf.reshape(bn, l, d_att),
                             k_cache, v_cache, l0_lq1, l0_lk1, l0_lq2, l0_lk2, l0_subln_w, n_heads_b, 0)
        rg_conv_new = proj0[:, l - (CONV_W - 1):, :d_rnn]
        x1 = _out_proj(ya.reshape(m, d_rnn), yb.reshape(m, d_att), w_out0, x0, norm_w[0, 1])
        x2 = _ffn(x1, norm_w[0, 2], norm_w[0, 3], wg[0], wu[0], wd[0])
        proj1 = _norm_matmul(x2, norm_w[1, 0], w_in1, tn1, False)[0].reshape(bn, l, -1)
        ys, ssd_s_new = _ssd(proj1, ssd_cols, ssd_conv, ssd_s, l1_conv_w, l1_conv_b, l1_dt_bias, l1_a_log,
                             l1_d_skip, l1_ssd_norm_w, n_grp, n_c, p_c)
        yh, hg_s_new = _hgrn(proj1, hg_cols, hg_s, l1_hg_lower_bound, l1_hg_norm_w, n_heads_d)
        tail = proj1[:, l - (CONV_W - 1):, :]
        ssd_conv_new = jnp.concatenate(
            [tail[:, :, d_ssd:2 * d_ssd], tail[:, :, 2 * d_ssd + 3 * d_hg:2 * d_ssd + 3 * d_hg + d_bc]], axis=-1)
        x3 = _out_proj(ys.reshape(m, d_ssd), yh.reshape(m, d_hg), w_out1, x2, norm_w[1, 1])
        x4 = _ffn(x3, norm_w[1, 2], norm_w[1, 3], wg[1], wu[1], wd[1])
        return (x4.reshape(bn, l, d_model), k_new, v_new, rg_conv_new, rg_h_new, ssd_conv_new, ssd_s_new, hg_s_new)

    bp = x_prompt.shape[0]
    zeros = lambda *s: jnp.zeros(s, F32)
    outs_p = trunk(x_prompt, None, None, zeros(bp, CONV_W - 1, d_rnn), zeros(bp, d_rnn),
                   zeros(bp, CONV_W - 1, d_xbc), zeros(bp, n_heads_c, p_c, n_c), zeros(bp, n_heads_d, dk_d, dv_d))
    outs_s = trunk(x_sample, cache_diff_k, cache_diff_v, state_rglru_conv, state_rglru_h, state_ssd_conv,
                   state_ssd, state_hgrn)
    return (outs_p[0], outs_s[0]) + tuple(outs_p[1:]) + tuple(outs_s[1:])
```

```python
import functools
import math

import jax
import jax.numpy as jnp
from jax import lax
from jax.experimental import pallas as pl
from jax.experimental.pallas import tpu as pltpu

F32 = jnp.float32
BF16 = jnp.bfloat16

EPS = 1e-6
CHUNK = 64
CONV_W = 4
RG_C = 8.0
LANES = 128
SUBLANES = 8
HG_SUB = 16
NEG = -1e30
VMEM_LIMIT_BYTES = 56 * 1024 * 1024

_NT = (((1,), (1,)), ((), ()))
_TN = (((0,), (0,)), ((), ()))


def _cparams(*sem):
    return pltpu.CompilerParams(dimension_semantics=sem, vmem_limit_bytes=VMEM_LIMIT_BYTES)


def _sigmoid(x):
    return 1.0 / (1.0 + jnp.exp(-x))


def _silu(x):
    return x * _sigmoid(x)


def _softplus(x):
    return jnp.maximum(x, 0.0) + jnp.log1p(jnp.exp(-jnp.abs(x)))


def _neg_expm1(x):
    return -jnp.tanh(0.5 * x) * (jnp.exp(x) + 1.0)


def _gelu_tanh(x):
    c = math.sqrt(2.0 / math.pi)
    return x * (0.5 * (1.0 + jnp.tanh(c * (x + 0.044715 * (x * x * x)))))


def _rms(x, w):
    return x * lax.rsqrt(jnp.mean(x * x, axis=-1, keepdims=True) + EPS) * w


def _dot(a, b):
    return jnp.dot(a, b, preferred_element_type=F32)


def _cumsum_rows(x):
    n = x.shape[0]
    row = lax.broadcasted_iota(jnp.int32, x.shape, 0)
    s = 1
    while s < n:
        x = x + jnp.where(row >= s, pltpu.roll(x, s, axis=0), 0.0)
        s *= 2
    return x


def _norm_matmul_kernel(x_ref, nw_ref, w_ref, o_ref, xn_ref):
    @pl.when(pl.program_id(1) == 0)
    def _():
        xn_ref[...] = _rms(x_ref[...], nw_ref[...]).astype(BF16)

    o_ref[...] = _dot(xn_ref[...], w_ref[...])


def _norm_matmul(x, nw, w, tn):
    m, d = x.shape
    n = w.shape[1]
    assert n % tn == 0
    tm = min(1024, m)
    return pl.pallas_call(
        _norm_matmul_kernel,
        grid=(m // tm, n // tn),
        in_specs=[
            pl.BlockSpec((tm, d), lambda i, j: (i, 0)),
            pl.BlockSpec((1, d), lambda i, j: (0, 0)),
            pl.BlockSpec((d, tn), lambda i, j: (0, j)),
        ],
        out_specs=[pl.BlockSpec((tm, tn), lambda i, j: (i, j)), pl.BlockSpec((tm, d), lambda i, j: (i, 0))],
        out_shape=[jax.ShapeDtypeStruct((m, n), F32), jax.ShapeDtypeStruct((m, d), BF16)],
        compiler_params=_cparams("parallel", "arbitrary"),
        name="norm_in_proj",
    )(x, nw.reshape(1, d), w)


def _head_proj_kernel(xn_ref, w_ref, *o_refs):
    n_out = len(o_refs) // 2
    j = pl.program_id(1)
    for k in range(n_out):
        @pl.when(j == k)
        def _(k=k):
            o_ref, ob_ref = o_refs[k], o_refs[n_out + k]
            res = _dot(xn_ref[...], w_ref[...])
            for h in range(o_ref.shape[1]):
                o_ref[:, h, :] = res[:, h * LANES:(h + 1) * LANES]
            ob_ref[...] = res.astype(BF16)


def _head_proj(xn, w, n_heads):
    m, d = xn.shape
    width = n_heads * LANES
    n_out = w.shape[1] // width
    tm = min(1024, m)
    return pl.pallas_call(
        _head_proj_kernel,
        grid=(m // tm, n_out),
        in_specs=[
            pl.BlockSpec((tm, d), lambda i, j: (i, 0)),
            pl.BlockSpec((d, width), lambda i, j: (0, j)),
        ],
        out_specs=([pl.BlockSpec((tm, n_heads, LANES), lambda i, j: (i, 0, 0)) for _ in range(n_out)]
                   + [pl.BlockSpec((tm, width), lambda i, j: (i, 0)) for _ in range(n_out)]),
        out_shape=([jax.ShapeDtypeStruct((m, n_heads, LANES), F32) for _ in range(n_out)]
                   + [jax.ShapeDtypeStruct((m, width), BF16) for _ in range(n_out)]),
        compiler_params=_cparams("parallel", "arbitrary"),
        name="head_proj",
    )(xn, w)


def _out_proj_kernel(ya_ref, yb_ref, wa_ref, wb_ref, x_ref, nw_ref, nwn_ref, o_ref, hn_ref):
    m = _dot(ya_ref[...], wa_ref[...]) + _dot(yb_ref[...], wb_ref[...])
    o = x_ref[...] + _rms(m, nw_ref[...])
    o_ref[...] = o
    hn_ref[...] = _rms(o, nwn_ref[...]).astype(BF16)


def _out_proj(ya, yb, w_out, x, nw, nw_next):
    m, d = x.shape
    da, db = ya.shape[1], yb.shape[1]
    assert da == db and w_out.shape[0] == da + db
    tm = min(512, m)
    row_spec = pl.BlockSpec((tm, d), lambda i: (i, 0))
    vec_spec = pl.BlockSpec((1, d), lambda i: (0, 0))
    return pl.pallas_call(
        _out_proj_kernel,
        grid=(m // tm,),
        in_specs=[
            pl.BlockSpec((tm, da), lambda i: (i, 0)),
            pl.BlockSpec((tm, db), lambda i: (i, 0)),
            pl.BlockSpec((da, d), lambda i: (0, 0)),
            pl.BlockSpec((db, d), lambda i: (1, 0)),
            row_spec, vec_spec, vec_spec,
        ],
        out_specs=[row_spec, row_spec],
        out_shape=[jax.ShapeDtypeStruct((m, d), F32), jax.ShapeDtypeStruct((m, d), BF16)],
        compiler_params=_cparams("parallel"),
        name="out_proj",
    )(ya, yb, w_out, w_out, x, nw.reshape(1, d), nw_next.reshape(1, d))


def _ffn_kernel(x_ref, hn_ref, nwb_ref, nwn_ref, wg_ref, wu_ref, wd_ref, o_ref, *rest, emit_next):
    acc_ref = rest[-1]
    f = pl.program_id(1)

    @pl.when(f == 0)
    def _():
        acc_ref[...] = jnp.zeros_like(acc_ref)

    hn = hn_ref[...]
    g = _dot(hn, wg_ref[...])
    u = _dot(hn, wu_ref[...])
    a = (_silu(g) * u).astype(BF16)
    acc_ref[...] += _dot(a, wd_ref[...])

    @pl.when(f == pl.num_programs(1) - 1)
    def _():
        o = x_ref[...] + _rms(acc_ref[...], nwb_ref[...])
        o_ref[...] = o
        if emit_next:
            rest[0][...] = _rms(o, nwn_ref[...]).astype(BF16)


def _ffn(x, hn, nw_post, nw_next, wg, wu, wd, layer):
    emit_next = nw_next is not None
    nw_next = nw_post if nw_next is None else nw_next
    m, d = x.shape
    dff = wg.shape[2]
    tm = min(512, m)
    tf = 512
    row_spec = pl.BlockSpec((tm, d), lambda i, f: (i, 0))
    vec_spec = pl.BlockSpec((1, d), lambda i, f: (0, 0))
    out_specs = [row_spec]
    out_shape = [jax.ShapeDtypeStruct((m, d), F32)]
    if emit_next:
        out_specs.append(row_spec)
        out_shape.append(jax.ShapeDtypeStruct((m, d), BF16))
    return pl.pallas_call(
        functools.partial(_ffn_kernel, emit_next=emit_next),
        grid=(m // tm, dff // tf),
        in_specs=[
            row_spec, row_spec, vec_spec, vec_spec,
            pl.BlockSpec((None, d, tf), lambda i, f: (layer, 0, f)),
            pl.BlockSpec((None, d, tf), lambda i, f: (layer, 0, f)),
            pl.BlockSpec((None, tf, d), lambda i, f: (layer, f, 0)),
        ],
        out_specs=out_specs,
        out_shape=out_shape,
        scratch_shapes=[pltpu.VMEM((tm, d), F32)],
        compiler_params=_cparams("parallel", "arbitrary"),
        name="ffn",
    )(x, hn, nw_post.reshape(1, d), nw_next.reshape(1, d), wg, wu, wd)


def _matmul_kernel(x_ref, w_ref, o_ref):
    o_ref[...] = _dot(x_ref[...], w_ref[...])


def _matmul(x, w, tn):
    m, d = x.shape
    n = w.shape[1]
    assert n % tn == 0
    tm = min(1024, m)
    return pl.pallas_call(
        _matmul_kernel,
        grid=(m // tm, n // tn),
        in_specs=[pl.BlockSpec((tm, d), lambda i, j: (i, 0)), pl.BlockSpec((d, tn), lambda i, j: (0, j))],
        out_specs=pl.BlockSpec((tm, tn), lambda i, j: (i, j)),
        out_shape=jax.ShapeDtypeStruct((m, n), F32),
        compiler_params=_cparams("parallel", "parallel"),
        name="in_proj",
    )(x, w)


def _conv_init(prev_ref, conv_buf):
    prev_ref[...] = jnp.zeros_like(prev_ref)
    prev_ref[SUBLANES - (CONV_W - 1):SUBLANES, :] = conv_buf


def _conv_tile(prev_ref, x, cw_ref, cb_ref):
    tl = x.shape[0]
    prev = prev_ref[...]
    row = lax.broadcasted_iota(jnp.int32, prev.shape, 0)
    y = cb_ref[...] + x * cw_ref[CONV_W - 1:CONV_W, :]
    for k in range(1, CONV_W):
        xr = pltpu.roll(x, k, axis=0)
        head = jnp.where(row < k, pltpu.roll(prev, k, axis=0), xr[0:SUBLANES])
        xk = head if tl == SUBLANES else jnp.concatenate([head, xr[SUBLANES:]], axis=0)
        y = y + xk * cw_ref[CONV_W - 1 - k:CONV_W - k, :]
    prev_ref[...] = x[tl - SUBLANES:tl]
    return y


def _rglru_kernel(xa_ref, ga_ref, cbuf_ref, h0_ref, cw_ref, cb_ref, wr_ref, br_ref, wi_ref, bi_ref,
                  lam_ref, ya_ref, hlast_ref, xs_ref, a_ref, u_ref, h_ref, *, tl, n_blk):
    t = pl.program_id(1)

    @pl.when(t == 0)
    def _():
        _conv_init(xs_ref, cbuf_ref[...])
        h_ref[...] = h0_ref[...]

    xc = _conv_tile(xs_ref, xa_ref[...], cw_ref, cb_ref)
    xcb = xc.astype(BF16)
    r_pre = jnp.concatenate(
        [_dot(xcb[:, hb * LANES:(hb + 1) * LANES], wr_ref[hb]) for hb in range(n_blk)], axis=1)
    i_pre = jnp.concatenate(
        [_dot(xcb[:, hb * LANES:(hb + 1) * LANES], wi_ref[hb]) for hb in range(n_blk)], axis=1)
    r = _sigmoid(r_pre + br_ref[...])
    gi = _sigmoid(i_pre + bi_ref[...])
    log_a = -RG_C * r * _softplus(-lam_ref[...])
    a_ref[...] = jnp.exp(log_a)
    u_ref[...] = jnp.sqrt(_neg_expm1(2.0 * log_a)) * (gi * xc)

    row = lax.broadcasted_iota(jnp.int32, (SUBLANES, a_ref.shape[1]), 0)

    def group(gidx, h_prev):
        r0 = pl.multiple_of(gidx * SUBLANES, SUBLANES)
        ag = a_ref[pl.ds(r0, SUBLANES), :]
        ug = u_ref[pl.ds(r0, SUBLANES), :]
        s = 1
        while s < SUBLANES:
            a_sh = jnp.where(row >= s, pltpu.roll(ag, s, axis=0), 1.0)
            u_sh = jnp.where(row >= s, pltpu.roll(ug, s, axis=0), 0.0)
            ug = ag * u_sh + ug
            ag = ag * a_sh
            s *= 2
        hg = ag * h_prev + ug
        u_ref[pl.ds(r0, SUBLANES), :] = hg
        return hg[SUBLANES - 1:SUBLANES, :]

    h_last = lax.fori_loop(0, tl // SUBLANES, group, h_ref[...])
    h_ref[...] = h_last
    hlast_ref[...] = h_last
    ya_ref[...] = (u_ref[...] * _gelu_tanh(ga_ref[...])).astype(BF16)


def _rglru(proj, conv_buf, h0, conv_w, conv_b, w_r, b_r, w_i, b_i, lam, d_rnn):
    bn, l, _ = proj.shape
    tl = min(256, l)
    n_blk = w_r.shape[0]
    vec = lambda v: v.reshape(1, d_rnn)
    const2 = lambda b, t: (0, 0)
    kern = functools.partial(_rglru_kernel, tl=tl, n_blk=n_blk)
    ya, h_last = pl.pallas_call(
        kern,
        grid=(bn, l // tl),
        in_specs=[
            pl.BlockSpec((None, tl, d_rnn), lambda b, t: (b, t, 0)),
            pl.BlockSpec((None, tl, d_rnn), lambda b, t: (b, t, 1)),
            pl.BlockSpec((None, CONV_W - 1, d_rnn), lambda b, t: (b, 0, 0)),
            pl.BlockSpec((None, 1, d_rnn), lambda b, t: (b, 0, 0)),
            pl.BlockSpec((CONV_W, d_rnn), const2),
            pl.BlockSpec((1, d_rnn), const2),
            pl.BlockSpec(w_r.shape, lambda b, t: (0, 0, 0)),
            pl.BlockSpec((1, d_rnn), const2),
            pl.BlockSpec(w_i.shape, lambda b, t: (0, 0, 0)),
            pl.BlockSpec((1, d_rnn), const2),
            pl.BlockSpec((1, d_rnn), const2),
        ],
        out_specs=[
            pl.BlockSpec((None, tl, d_rnn), lambda b, t: (b, t, 0)),
            pl.BlockSpec((None, 1, d_rnn), lambda b, t: (b, 0, 0)),
        ],
        out_shape=[
            jax.ShapeDtypeStruct((bn, l, d_rnn), BF16),
            jax.ShapeDtypeStruct((bn, 1, d_rnn), F32),
        ],
        scratch_shapes=[
            pltpu.VMEM((SUBLANES, d_rnn), F32),
            pltpu.VMEM((tl, d_rnn), F32),
            pltpu.VMEM((tl, d_rnn), F32),
            pltpu.VMEM((1, d_rnn), F32),
        ],
        compiler_params=_cparams("parallel", "arbitrary"),
        name="rglru",
    )(proj, proj, conv_buf, h0.reshape(bn, 1, d_rnn), conv_w, vec(conv_b), w_r, vec(b_r), w_i, vec(b_i),
      vec(lam))
    return ya, h_last.reshape(bn, d_rnn)


def _chunk_mask(kpos, qpos):
    shift = CHUNK.bit_length() - 1
    return jnp.right_shift(kpos, shift) <= jnp.right_shift(qpos, shift)


def _diff_lambda(lq1_ref, lk1_ref, lq2_ref, lk2_ref, lam_init):
    return (jnp.exp(jnp.sum(lq1_ref[...] * lk1_ref[...], axis=-1, keepdims=True))
            - jnp.exp(jnp.sum(lq2_ref[...] * lk2_ref[...], axis=-1, keepdims=True)) + lam_init)


ONES_ROWS = 16


def _attn_prompt_kernel(q_ref, k_ref, v_ref, lq1_ref, lk1_ref, lq2_ref, lk2_ref, sw_ref, o_ref,
                        vt_ref, sa_ref, sb_ref, m_ref, acc_ref, *, t, nblk, dh, scale, lam_init):
    i = pl.program_id(2)
    dv = 2 * dh

    @pl.when(i == 0)
    def _():
        ones = jnp.ones((ONES_ROWS, t), BF16)
        for jb in range(nblk):
            vt_ref[jb, :dv, :] = jnp.transpose(v_ref[jb * t:(jb + 1) * t, :].astype(F32)).astype(BF16)
            vt_ref[jb, dv:, :] = ones

    qt = jnp.transpose(q_ref[...] * (scale * math.log2(math.e)))
    sub = lax.broadcasted_iota(jnp.int32, qt.shape, 0)
    qt2 = jnp.concatenate([jnp.where(sub < dh, qt, 0.0), jnp.where(sub >= dh, qt, 0.0)],
                          axis=1).astype(BF16)
    m_ref[...] = jnp.full_like(m_ref, NEG)
    acc_ref[...] = jnp.zeros_like(acc_ref)

    def scores(j, s_ref):
        s_ref[...] = _dot(k_ref[pl.ds(pl.multiple_of(j * t, t), t), :], qt2)

    def update(j, s_ref, diagonal):
        if diagonal:
            qoff = lax.broadcasted_iota(jnp.int32, (1, 2 * t), 1)
            qoff = jnp.where(qoff >= t, qoff - t, qoff)
            s = jnp.concatenate(
                [s_ref[c * CHUNK:(c + 1) * CHUNK, :] + jnp.where(qoff >= c * CHUNK, 0.0, NEG)
                 for c in range(t // CHUNK)], axis=0)
        else:
            s = s_ref[...]
        m_prev = m_ref[...]
        m_new = jnp.maximum(m_prev, jnp.max(s, axis=0, keepdims=True))
        alpha = jnp.exp2(m_prev - m_new)
        p = jnp.exp2(s - m_new).astype(BF16)
        acc_ref[...] = alpha * acc_ref[...] + _dot(vt_ref[j], p)
        m_ref[...] = m_new

    scores(0, sa_ref)

    def pair(pidx, carry):
        j = 2 * pidx
        scores(j + 1, sb_ref)
        update(j, sa_ref, False)
        scores(j + 2, sa_ref)
        update(j + 1, sb_ref, False)
        return carry

    lax.fori_loop(0, lax.shift_right_logical(i, 1), pair, 0)

    @pl.when((i & 1) == 1)
    def _():
        scores(i, sb_ref)
        update(i - 1, sa_ref, False)
        sa_ref[...] = sb_ref[...]

    update(i, sa_ref, True)

    lam = _diff_lambda(lq1_ref, lk1_ref, lq2_ref, lk2_ref, lam_init)
    acc = acc_ref[0:dv, :]
    l = acc_ref[dv:dv + 1, :]
    o_t = acc[:, :t] / l[:, :t] - lam * (acc[:, t:] / l[:, t:])
    o_ref[...] = (_rms(jnp.transpose(o_t), sw_ref[...]) * (1.0 - lam_init)).astype(BF16)


def _attn_cached_kernel(q_ref, kc_hbm, vc_hbm, kn_ref, vn_ref, lq1_ref, lk1_ref, lq2_ref, lk2_ref, sw_ref,
                        o_ref, qt_ref, m_ref, acc_ref, kbuf, vbuf, sem, *, n_heads, lq, n_cache, tkc, pos0,
                        dh, scale, lam_init):
    b = pl.program_id(0)
    j = pl.program_id(1)
    dv = 2 * dh
    hsl = lambda h: slice(h * LANES, (h + 1) * LANES)

    def cache_copies(bi, step, slot):
        rows = pl.ds(pl.multiple_of(step * tkc, tkc), tkc)
        return ([pltpu.make_async_copy(kc_hbm.at[bi, rows, h, :], kbuf.at[slot, h], sem.at[slot, 0])
                 for h in range(n_heads)]
                + [pltpu.make_async_copy(vc_hbm.at[bi, rows, h, :], vbuf.at[slot, h], sem.at[slot, 1])
                   for h in range(n_heads)])

    def fetch(bi, step, slot):
        for cp in cache_copies(bi, step, slot):
            cp.start()

    @pl.when(jnp.logical_and(b == 0, j == 0))
    def _():
        fetch(0, 0, 0)

    @pl.when(j == 0)
    def _():
        sub = lax.broadcasted_iota(jnp.int32, (dv, lq), 0)
        for h in range(n_heads):
            qt = jnp.transpose(q_ref[:, hsl(h)] * (scale * math.log2(math.e)))
            qt_ref[h] = jnp.concatenate([jnp.where(sub < dh, qt, 0.0), jnp.where(sub >= dh, qt, 0.0)],
                                        axis=1).astype(BF16)
        m_ref[...] = jnp.full_like(m_ref, NEG)
        acc_ref[...] = jnp.zeros_like(acc_ref)

    def update(k_of, v_of, tk, bias):
        ones = jnp.ones((ONES_ROWS, tk), BF16)
        for h in range(n_heads):
            s = _dot(k_of(h).astype(BF16), qt_ref[h])
            if bias is not None:
                s = s + bias
            m_prev = m_ref[h]
            m_new = jnp.maximum(m_prev, jnp.max(s, axis=0, keepdims=True))
            alpha = jnp.exp2(m_prev - m_new)
            p = jnp.exp2(s - m_new).astype(BF16)
            vt = jnp.transpose(v_of(h).astype(F32)).astype(BF16)
            acc_ref[h] = alpha * acc_ref[h] + _dot(jnp.concatenate([vt, ones], axis=0), p)
            m_ref[h] = m_new

    @pl.when(j < n_cache)
    def _():
        slot = j & 1

        @pl.when(j + 1 < n_cache)
        def _():
            fetch(b, j + 1, 1 - slot)

        for cp in cache_copies(b, j, slot):
            cp.wait()
        update(lambda h: kbuf[slot, h], lambda h: vbuf[slot, h], tkc, None)

    @pl.when(j == n_cache)
    def _():
        @pl.when(b + 1 < pl.num_programs(0))
        def _():
            fetch(b + 1, 0, 0)

        col = lax.broadcasted_iota(jnp.int32, (lq, 2 * lq), 1)
        qpos = pos0 + jnp.where(col >= lq, col - lq, col)
        kpos = pos0 + lax.broadcasted_iota(jnp.int32, (lq, 2 * lq), 0)
        update(lambda h: kn_ref[:, hsl(h)], lambda h: vn_ref[:, hsl(h)], lq,
               jnp.where(_chunk_mask(kpos, qpos), 0.0, NEG))
        lam = _diff_lambda(lq1_ref, lk1_ref, lq2_ref, lk2_ref, lam_init)
        outs = []
        for h in range(n_heads):
            acc = acc_ref[h, 0:dv, :]
            l = acc_ref[h, dv:dv + 1, :]
            o_t = acc[:, :lq] / l[:, :lq] - lam * (acc[:, lq:] / l[:, lq:])
            outs.append(_rms(jnp.transpose(o_t), sw_ref[...]) * (1.0 - lam_init))
        o_ref[...] = jnp.concatenate(outs, axis=1).astype(BF16)


def _diff_attention(qsrc, q_col, k, v, k_cache, v_cache, lq1, lk1, lq2, lk2, subln_w, n_heads, layer):
    bn, l = k.shape[:2]
    dv = LANES
    dh = dv // 2
    d_att = n_heads * dv
    lam_init = 0.8 - 0.6 * math.exp(-0.3 * layer)
    scale = dh ** -0.5
    params = [lq1.reshape(1, dh), lk1.reshape(1, dh), lq2.reshape(1, dh), lk2.reshape(1, dh),
              subln_w.reshape(1, dv)]
    if k_cache is None:
        t = min(512, l)
        nblk = l // t
        assert t % CHUNK == 0 and l % t == 0
        small = lambda n: pl.BlockSpec((1, n), lambda b, h, i: (0, 0))
        kern = functools.partial(_attn_prompt_kernel, t=t, nblk=nblk, dh=dh, scale=scale, lam_init=lam_init)
        return pl.pallas_call(
            kern,
            grid=(bn, n_heads, nblk),
            in_specs=[
                pl.BlockSpec((None, t, dv), lambda b, h, i: (b, i, q_col + h)),
                pl.BlockSpec((None, l, dv), lambda b, h, i: (b, 0, h)),
                pl.BlockSpec((None, l, dv), lambda b, h, i: (b, 0, h)),
                small(dh), small(dh), small(dh), small(dh), small(dv),
            ],
            out_specs=pl.BlockSpec((None, t, dv), lambda b, h, i: (b, i, h)),
            out_shape=jax.ShapeDtypeStruct((bn, l, d_att), BF16),
            scratch_shapes=[
                pltpu.VMEM((nblk, dv + ONES_ROWS, t), BF16),
                pltpu.VMEM((t, 2 * t), F32),
                pltpu.VMEM((t, 2 * t), F32),
                pltpu.VMEM((1, 2 * t), F32),
                pltpu.VMEM((dv + ONES_ROWS, 2 * t), F32),
            ],
            compiler_params=_cparams("parallel", "parallel", "arbitrary"),
            name="diff_attention_prompt",
        )(qsrc, k, v, *params)

    pos0 = k_cache.shape[1]
    tkc = min(512, pos0)
    assert pos0 % CHUNK == 0 and pos0 % tkc == 0 and l % SUBLANES == 0
    n_cache = pos0 // tkc
    small = lambda n: pl.BlockSpec((1, n), lambda b, j: (0, 0))
    kern = functools.partial(_attn_cached_kernel, n_heads=n_heads, lq=l, n_cache=n_cache, tkc=tkc, pos0=pos0,
                             dh=dh, scale=scale, lam_init=lam_init)
    cache_spec = pl.BlockSpec(memory_space=pl.ANY)
    new_spec = pl.BlockSpec((None, l, d_att), lambda b, j: (b, 0, 0))
    return pl.pallas_call(
        kern,
        grid=(bn, n_cache + 1),
        in_specs=[
            pl.BlockSpec((None, l, d_att), lambda b, j: (b, 0, q_col * dv // d_att)),
            cache_spec, cache_spec, new_spec, new_spec,
            small(dh), small(dh), small(dh), small(dh), small(dv),
        ],
        out_specs=pl.BlockSpec((None, l, d_att), lambda b, j: (b, 0, 0)),
        out_shape=jax.ShapeDtypeStruct((bn, l, d_att), BF16),
        scratch_shapes=[
            pltpu.VMEM((n_heads, dv, 2 * l), BF16),
            pltpu.VMEM((n_heads, 1, 2 * l), F32),
            pltpu.VMEM((n_heads, dv + ONES_ROWS, 2 * l), F32),
            pltpu.VMEM((2, n_heads, tkc, dv), F32),
            pltpu.VMEM((2, n_heads, tkc, dv), F32),
            pltpu.SemaphoreType.DMA((2, 2)),
        ],
        compiler_params=_cparams("arbitrary", "arbitrary"),
        name="diff_attention_cached",
    )(qsrc, k_cache, v_cache, k, v, *params)


def _ssd_kernel(z_ref, x_ref, bc_ref, dt_ref, cbx_ref, cbbc_ref, s0_ref, cwx_ref, cbiasx_ref, cwbc_ref,
                cbiasbc_ref, dtb_ref, alog_ref, dskip_ref, nw_ref, expand_ref,
                y_ref, s_ref, xsx_ref, xsbc_ref, *, n_grp, hpg, p_dim, n_dim):
    t = pl.program_id(1)
    q = CHUNK
    d_ssd = n_grp * hpg * p_dim
    gw = hpg * p_dim

    @pl.when(t == 0)
    def _():
        _conv_init(xsx_ref, cbx_ref[...])
        _conv_init(xsbc_ref, cbbc_ref[...])
        s_ref[...] = s0_ref[...]

    xs = _silu(_conv_tile(xsx_ref, x_ref[...], cwx_ref, cbiasx_ref))
    bcs = _silu(_conv_tile(xsbc_ref, bc_ref[...], cwbc_ref, cbiasbc_ref))

    dt = _softplus(dt_ref[...] + dtb_ref[...])
    a = -jnp.exp(alog_ref[...])
    acum = _cumsum_rows(dt * a)
    a_last = acum[q - 1:q, :]
    wst = jnp.exp(a_last - acum) * dt
    eac = jnp.exp(acum)
    acum_t = jnp.transpose(acum)

    stacked = jnp.concatenate([dt, wst, eac], axis=0)
    hi = stacked.astype(BF16)
    rem = stacked - hi.astype(F32)
    mid = rem.astype(BF16)
    lo = (rem - mid.astype(F32)).astype(BF16)
    expand = expand_ref[...]
    expd = _dot(hi, expand) + _dot(mid, expand) + _dot(lo, expand)
    dt_e, wst_e, eac_e = expd[0:q], expd[q:2 * q], expd[2 * q:3 * q]

    xdt = (xs * dt_e).astype(BF16)
    xw = (xs * wst_e).astype(BF16)
    row = lax.broadcasted_iota(jnp.int32, (q, q), 0)
    col = lax.broadcasted_iota(jnp.int32, (q, q), 1)
    causal = col <= row
    lane = lax.broadcasted_iota(jnp.int32, (q, LANES), 1)
    heads_per_tile = LANES // p_dim

    bgs = [bcs[:, g * n_dim:(g + 1) * n_dim] for g in range(n_grp)]
    cgs = [bcs[:, (n_grp + g) * n_dim:(n_grp + g + 1) * n_dim].astype(BF16) for g in range(n_grp)]
    cbs = [lax.dot_general(cgs[g], bgs[g].astype(BF16), _NT, preferred_element_type=F32) for g in range(n_grp)]
    bg_ts = [jnp.transpose(bgs[g]).astype(BF16) for g in range(n_grp)]
    wdec = []
    for h in range(n_grp * hpg):
        dec = jnp.exp(jnp.where(causal, acum[:, h:h + 1] - acum_t[h:h + 1, :], -jnp.inf))
        wdec.append((cbs[h // hpg] * dec).astype(BF16))

    y_parts = []
    for g in range(n_grp):
        s_prev = s_ref[g]
        y_off = _dot(cgs[g], s_prev.astype(BF16)) * eac_e[:, g * gw:(g + 1) * gw]
        y_diag = []
        for tile in range(gw // LANES):
            c0 = g * gw + tile * LANES
            x_tile = xdt[:, c0:c0 + LANES]
            acc = None
            for hh in range(heads_per_tile):
                yh = _dot(wdec[c0 // p_dim + hh], x_tile)
                sel = jnp.logical_and(lane >= hh * p_dim, lane < (hh + 1) * p_dim)
                acc = jnp.where(sel, yh, 0.0) if acc is None else jnp.where(sel, yh, acc)
            y_diag.append(acc)
        y_parts.append(jnp.concatenate(y_diag, axis=1) + y_off)
        st = _dot(bg_ts[g], xw[:, g * gw:(g + 1) * gw])
        s_ref[g] = eac_e[q - 1:q, g * gw:(g + 1) * gw] * s_prev + st
    y = jnp.concatenate(y_parts, axis=1)
    y = (y + dskip_ref[...] * xs) * _silu(z_ref[...])
    nw = nw_ref[...]
    y_ref[...] = jnp.concatenate(
        [_rms(y[:, g * gw:(g + 1) * gw], nw[:, g * gw:(g + 1) * gw]) for g in range(n_grp)],
        axis=1).astype(BF16)


def _ssd(proj, cols, conv_buf, s0, conv_w, conv_b, dt_bias, a_log, d_skip, norm_w, n_grp, n_dim, p_dim):
    bn, l, _ = proj.shape
    n_heads = dt_bias.shape[0]
    hpg = n_heads // n_grp
    d_ssd = n_heads * p_dim
    d_bc = 2 * n_grp * n_dim
    gw = hpg * p_dim
    q = CHUNK
    assert l % q == 0 and n_heads <= LANES and LANES % p_dim == 0
    z_col, x_col, bc_col, dt_col = cols
    pad = lambda v: jnp.pad(v.astype(F32), (0, LANES - n_heads)).reshape(1, LANES)
    expand = (jnp.arange(LANES)[:, None] == (jnp.arange(d_ssd)[None, :] // p_dim)).astype(BF16)
    s0_t = s0.reshape(bn, n_grp, hpg, p_dim, n_dim).transpose(0, 1, 4, 2, 3).reshape(bn, n_grp, n_dim, gw)
    const2 = lambda b, t: (0, 0)
    kern = functools.partial(_ssd_kernel, n_grp=n_grp, hpg=hpg, p_dim=p_dim, n_dim=n_dim)
    y, s_last = pl.pallas_call(
        kern,
        grid=(bn, l // q),
        in_specs=[
            pl.BlockSpec((None, q, d_ssd), lambda b, t: (b, t, z_col)),
            pl.BlockSpec((None, q, d_ssd), lambda b, t: (b, t, x_col)),
            pl.BlockSpec((None, q, d_bc), lambda b, t: (b, t, bc_col)),
            pl.BlockSpec((None, q, LANES), lambda b, t: (b, t, dt_col)),
            pl.BlockSpec((None, CONV_W - 1, d_ssd), lambda b, t: (b, 0, 0)),
            pl.BlockSpec((None, CONV_W - 1, d_bc), lambda b, t: (b, 0, 0)),
            pl.BlockSpec((None, n_grp, n_dim, gw), lambda b, t: (b, 0, 0, 0)),
            pl.BlockSpec((CONV_W, d_ssd), const2),
            pl.BlockSpec((1, d_ssd), const2),
            pl.BlockSpec((CONV_W, d_bc), const2),
            pl.BlockSpec((1, d_bc), const2),
            pl.BlockSpec((1, LANES), const2),
            pl.BlockSpec((1, LANES), const2),
            pl.BlockSpec((1, d_ssd), const2),
            pl.BlockSpec((1, d_ssd), const2),
            pl.BlockSpec((LANES, d_ssd), const2),
        ],
        out_specs=[
            pl.BlockSpec((None, q, d_ssd), lambda b, t: (b, t, 0)),
            pl.BlockSpec((None, n_grp, n_dim, gw), lambda b, t: (b, 0, 0, 0)),
        ],
        out_shape=[
            jax.ShapeDtypeStruct((bn, l, d_ssd), BF16),
            jax.ShapeDtypeStruct((bn, n_grp, n_dim, gw), F32),
        ],
        scratch_shapes=[
            pltpu.VMEM((SUBLANES, d_ssd), F32),
            pltpu.VMEM((SUBLANES, d_bc), F32),
        ],
        compiler_params=_cparams("parallel", "arbitrary"),
        name="ssd",
    )(proj, proj, proj, proj, conv_buf[:, :, :d_ssd], conv_buf[:, :, d_ssd:], s0_t,
      conv_w[:, :d_ssd], conv_b[:d_ssd].reshape(1, d_ssd), conv_w[:, d_ssd:], conv_b[d_ssd:].reshape(1, d_bc),
      pad(dt_bias), pad(a_log), jnp.repeat(d_skip.astype(F32), p_dim).reshape(1, d_ssd),
      norm_w.reshape(1, d_ssd), expand)
    s_last = s_last.reshape(bn, n_grp, n_dim, hpg, p_dim).transpose(0, 1, 3, 4, 2).reshape(bn, n_heads, p_dim, n_dim)
    return y, s_last


def _hgrn_kernel(q_ref, f_ref, i_ref, lb_ref, nw_ref, s0_ref, y_ref, slast_ref, *, n_heads):
    t = pl.program_id(1)
    c = CHUNK
    nsb = c // HG_SUB

    @pl.when(t == 0)
    def _():
        slast_ref[...] = s0_ref[...]

    lbp = lb_ref[...]
    e = jnp.exp(lbp - jnp.max(lbp, axis=0, keepdims=True))
    lb = e[1:2, :] / jnp.sum(e, axis=0, keepdims=True)
    g = lb + (1.0 - lb) * _sigmoid(f_ref[...])
    logg = jnp.log(g)
    kk = 1.0 - g
    qq = _silu(q_ref[...])
    b = _cumsum_rows(logg)
    iv = i_ref[...]

    zero_blk = jnp.zeros((HG_SUB, LANES), F32)
    r2 = lax.broadcasted_iota(jnp.int32, (c, c), 0)
    c2 = lax.broadcasted_iota(jnp.int32, (c, c), 1)
    causal = c2 <= r2
    nw = nw_ref[...]

    b_last = b[c - 1:c, :]
    qe = (qq * jnp.exp(b)).astype(BF16)
    kd = (kk * jnp.exp(b_last - b)).astype(BF16)
    s_decay = jnp.exp(b_last)
    ivb = iv.astype(BF16)
    sb = lambda x, r: x[r * HG_SUB:(r + 1) * HG_SUB, :]
    m_off = [b[a * HG_SUB - 1:a * HG_SUB, :] for a in range(1, nsb)]
    m_mid = [b[r * HG_SUB + HG_SUB // 2 - 1:r * HG_SUB + HG_SUB // 2, :] for r in range(nsb)]
    q_off = [None] + [sb(qq, a) * jnp.exp(sb(b, a) - m_off[a - 1]) for a in range(1, nsb)]
    q_mid = [sb(qq, r) * jnp.exp(sb(b, r) - m_mid[r]) for r in range(nsb)]
    k_mid = [sb(kk, r) * jnp.exp(m_mid[r] - sb(b, r)) for r in range(nsb)]
    k_off = [[sb(kk, r) * jnp.exp(m_off[a - 1] - sb(b, r)) if a > r else None for a in range(1, nsb)]
             for r in range(nsb)]

    atts = []
    for h in range(n_heads):
        sl = slice(h * LANES, (h + 1) * LANES)
        q_rows, k_rows = [], []
        for r in range(nsb):
            q_parts = [q_off[a][:, sl] if a == r else zero_blk for a in range(1, nsb)]
            k_parts = [k_off[r][a - 1][:, sl] if a > r else zero_blk for a in range(1, nsb)]
            q_parts += [q_mid[r][:, sl] if d == r else zero_blk for d in range(nsb)]
            k_parts += [k_mid[r][:, sl] if d == r else zero_blk for d in range(nsb)]
            q_rows.append(jnp.concatenate(q_parts, axis=1))
            k_rows.append(jnp.concatenate(k_parts, axis=1))
        qcat = jnp.concatenate(q_rows, axis=0).astype(BF16)
        kcat = jnp.concatenate(k_rows, axis=0).astype(BF16)
        att = lax.dot_general(qcat, kcat, _NT, preferred_element_type=F32)
        atts.append(jnp.where(causal, att, 0.0).astype(BF16))
    outs = []
    for h in range(n_heads):
        sl = slice(h * LANES, (h + 1) * LANES)
        s_t = slast_ref[h]
        o = _dot(atts[h], ivb[:, sl])
        o = o + lax.dot_general(qe[:, sl], s_t.astype(BF16), _NT, preferred_element_type=F32)
        slast_ref[h] = s_decay[:, sl] * s_t + _dot(jnp.transpose(iv[:, sl]).astype(BF16), kd[:, sl])
        outs.append(_rms(o, nw))
    y_ref[...] = jnp.concatenate(outs, axis=1).astype(BF16)


def _hgrn(proj, cols, s0, lb_param, norm_w, n_heads):
    bn, l, _ = proj.shape
    d = n_heads * LANES
    c = CHUNK
    assert l % c == 0
    q_col, f_col, i_col = cols
    depth = lb_param.shape[0]
    assert depth == 2
    s0_t = jnp.swapaxes(s0, -1, -2)
    kern = functools.partial(_hgrn_kernel, n_heads=n_heads)
    y, s_last = pl.pallas_call(
        kern,
        grid=(bn, l // c),
        in_specs=[
            pl.BlockSpec((None, c, d), lambda b, t: (b, t, q_col)),
            pl.BlockSpec((None, c, d), lambda b, t: (b, t, f_col)),
            pl.BlockSpec((None, c, d), lambda b, t: (b, t, i_col)),
            pl.BlockSpec((depth, d), lambda b, t: (0, 0)),
            pl.BlockSpec((1, LANES), lambda b, t: (0, 0)),
            pl.BlockSpec((None, n_heads, LANES, LANES), lambda b, t: (b, 0, 0, 0)),
        ],
        out_specs=[
            pl.BlockSpec((None, c, d), lambda b, t: (b, t, 0)),
            pl.BlockSpec((None, n_heads, LANES, LANES), lambda b, t: (b, 0, 0, 0)),
        ],
        out_shape=[
            jax.ShapeDtypeStruct((bn, l, d), BF16),
            jax.ShapeDtypeStruct((bn, n_heads, LANES, LANES), F32),
        ],
        compiler_params=_cparams("parallel", "arbitrary"),
        name="hgrn2",
    )(proj, proj, proj, lb_param, norm_w.reshape(1, LANES), s0_t)
    return y, jnp.swapaxes(s_last, -1, -2)


def kernel(x_prompt, x_sample, cache_diff_k, cache_diff_v, state_rglru_conv, state_rglru_h, state_ssd_conv,
           state_ssd, state_hgrn, norm_w, l0_w_in, l0_conv_w, l0_conv_b, l0_rg_w_r, l0_rg_b_r, l0_rg_w_i,
           l0_rg_b_i, l0_rg_lambda, l0_lq1, l0_lk1, l0_lq2, l0_lk2, l0_subln_w, l0_w_out, l1_w_in, l1_conv_w,
           l1_conv_b, l1_dt_bias, l1_a_log, l1_d_skip, l1_ssd_norm_w, l1_hg_lower_bound, l1_hg_norm_w, l1_w_out,
           ffn_w_gate, ffn_w_up, ffn_w_down):
    d_model = x_prompt.shape[-1]
    d_rnn = l0_conv_w.shape[1]
    n_heads_b, dv_b = cache_diff_k.shape[2], cache_diff_k.shape[3]
    d_att = n_heads_b * dv_b
    n_heads_c, p_c, n_c = state_ssd.shape[1:]
    d_ssd = n_heads_c * p_c
    d_xbc = l1_conv_w.shape[1]
    d_bc = d_xbc - d_ssd
    n_grp = d_bc // (2 * n_c)
    n_heads_d, dk_d, dv_d = state_hgrn.shape[1:]
    d_hg = n_heads_d * dk_d
    assert dv_b == LANES and dk_d == LANES and dv_d == LANES and d_rnn == d_att == d_ssd == d_hg

    w_in0_main = l0_w_in[:, :2 * d_rnn + d_att].astype(BF16)
    w_in0_kv = l0_w_in[:, 2 * d_rnn + d_att:].astype(BF16)
    o1 = d_ssd
    o2 = o1 + d_xbc
    o3 = o2 + n_heads_c
    w1 = l1_w_in
    w_in1 = jnp.concatenate(
        [w1[:, :o1], w1[:, o1:o1 + d_ssd], w1[:, o3:], w1[:, o1 + d_ssd:o2],
         jnp.pad(w1[:, o2:o3], ((0, 0), (0, LANES - n_heads_c)))], axis=1).astype(BF16)
    w_out0 = l0_w_out.astype(BF16)
    w_out1 = l1_w_out.astype(BF16)
    wg = ffn_w_gate.astype(BF16)
    wu = ffn_w_up.astype(BF16)
    wd = ffn_w_down.astype(BF16)
    w_r = l0_rg_w_r.astype(BF16)
    w_i = l0_rg_w_i.astype(BF16)
    tn0 = 512
    tn1 = 1152
    assert w_in0_main.shape[1] % tn0 == 0 and w_in1.shape[1] % tn1 == 0
    ssd_cols = (0, 1, (2 * d_ssd + 3 * d_hg) // d_bc, (2 * d_ssd + 3 * d_hg + d_bc) // LANES)
    hg_cols = (2, 3, 4)

    def trunk(x, k_cache, v_cache, rg_conv, rg_h, ssd_conv, ssd_s, hg_s):
        bn, l, _ = x.shape
        m = bn * l
        x0 = x.reshape(m, d_model)
        proj0, xn0 = _norm_matmul(x0, norm_w[0, 0], w_in0_main, tn0)
        proj0 = proj0.reshape(bn, l, -1)
        k_new, v_new, k_bf, v_bf = _head_proj(xn0, w_in0_kv, n_heads_b)
        k_new = k_new.reshape(bn, l, n_heads_b, dv_b)
        v_new = v_new.reshape(bn, l, n_heads_b, dv_b)
        ya, rg_h_new = _rglru(proj0, rg_conv, rg_h, l0_conv_w, l0_conv_b, w_r, l0_rg_b_r, w_i, l0_rg_b_i,
                              l0_rg_lambda, d_rnn)
        yb = _diff_attention(proj0, 2 * d_rnn // LANES, k_bf.reshape(bn, l, d_att), v_bf.reshape(bn, l, d_att),
                             k_cache, v_cache, l0_lq1, l0_lk1, l0_lq2, l0_lk2, l0_subln_w, n_heads_b, 0)
        rg_conv_new = proj0[:, l - (CONV_W - 1):, :d_rnn]
        x1, hn1 = _out_proj(ya.reshape(m, d_rnn), yb.reshape(m, d_att), w_out0, x0, norm_w[0, 1], norm_w[0, 2])
        x2, xn2 = _ffn(x1, hn1, norm_w[0, 3], norm_w[1, 0], wg, wu, wd, 0)
        proj1 = _matmul(xn2, w_in1, tn1).reshape(bn, l, -1)
        ys, ssd_s_new = _ssd(proj1, ssd_cols, ssd_conv, ssd_s, l1_conv_w, l1_conv_b, l1_dt_bias, l1_a_log,
                             l1_d_skip, l1_ssd_norm_w, n_grp, n_c, p_c)
        yh, hg_s_new = _hgrn(proj1, hg_cols, hg_s, l1_hg_lower_bound, l1_hg_norm_w, n_heads_d)
        tail = proj1[:, l - (CONV_W - 1):, :]
        ssd_conv_new = jnp.concatenate(
            [tail[:, :, d_ssd:2 * d_ssd], tail[:, :, 2 * d_ssd + 3 * d_hg:2 * d_ssd + 3 * d_hg + d_bc]], axis=-1)
        x3, hn3 = _out_proj(ys.reshape(m, d_ssd), yh.reshape(m, d_hg), w_out1, x2, norm_w[1, 1], norm_w[1, 2])
        x4 = _ffn(x3, hn3, norm_w[1, 3], None, wg, wu, wd, 1)[0]
        return (x4.reshape(bn, l, d_model), k_new, v_new, rg_conv_new, rg_h_new, ssd_conv_new, ssd_s_new, hg_s_new)

    bp = x_prompt.shape[0]
    zeros = lambda *s: jnp.zeros(s, F32)
    outs_p = trunk(x_prompt, None, None, zeros(bp, CONV_W - 1, d_rnn), zeros(bp, d_rnn),
                   zeros(bp, CONV_W - 1, d_xbc), zeros(bp, n_heads_c, p_c, n_c), zeros(bp, n_heads_d, dk_d, dv_d))
    outs_s = trunk(x_sample, cache_diff_k, cache_diff_v, state_rglru_conv, state_rglru_h, state_ssd_conv,
                   state_ssd, state_hgrn)
    return (outs_p[0], outs_s[0]) + tuple(outs_p[1:]) + tuple(outs_s[1:])
```

```python
import functools
import math

import jax
import jax.numpy as jnp
from jax import lax
from jax.experimental import pallas as pl
from jax.experimental.pallas import tpu as pltpu

F32 = jnp.float32
BF16 = jnp.bfloat16

EPS = 1e-6
CHUNK = 64
CONV_W = 4
RG_C = 8.0
LANES = 128
SUBLANES = 8
HG_SUB = 16
NEG = -1e30
VMEM_LIMIT_BYTES = 56 * 1024 * 1024

_NT = (((1,), (1,)), ((), ()))
_TN = (((0,), (0,)), ((), ()))


def _cparams(*sem):
    return pltpu.CompilerParams(dimension_semantics=sem, vmem_limit_bytes=VMEM_LIMIT_BYTES)


def _sigmoid(x):
    return 0.5 * jnp.tanh(0.5 * x) + 0.5


def _silu(x):
    h = 0.5 * x
    return h * jnp.tanh(h) + h


def _softplus(x):
    return jnp.maximum(x, 0.0) + jnp.log1p(jnp.exp(-jnp.abs(x)))


def _gelu_tanh(x):
    c = math.sqrt(2.0 / math.pi)
    return x * (0.5 * (1.0 + jnp.tanh(c * (x + 0.044715 * (x * x * x)))))


def _rms(x, w):
    return x * lax.rsqrt(jnp.mean(x * x, axis=-1, keepdims=True) + EPS) * w


def _dot(a, b):
    return jnp.dot(a, b, preferred_element_type=F32)


def _cumsum_rows(x):
    n = x.shape[0]
    row = lax.broadcasted_iota(jnp.int32, x.shape, 0)
    s = 1
    while s < n:
        x = x + jnp.where(row >= s, pltpu.roll(x, s, axis=0), 0.0)
        s *= 2
    return x


def _norm_matmul_kernel(x_ref, nw_ref, w_ref, o_ref, xn_ref):
    @pl.when(pl.program_id(1) == 0)
    def _():
        xn_ref[...] = _rms(x_ref[...], nw_ref[...]).astype(BF16)

    o_ref[...] = _dot(xn_ref[...], w_ref[...])


def _norm_matmul(x, nw, w, tn):
    m, d = x.shape
    n = w.shape[1]
    assert n % tn == 0
    tm = min(1024, m)
    return pl.pallas_call(
        _norm_matmul_kernel,
        grid=(m // tm, n // tn),
        in_specs=[
            pl.BlockSpec((tm, d), lambda i, j: (i, 0)),
            pl.BlockSpec((1, d), lambda i, j: (0, 0)),
            pl.BlockSpec((d, tn), lambda i, j: (0, j)),
        ],
        out_specs=[pl.BlockSpec((tm, tn), lambda i, j: (i, j)), pl.BlockSpec((tm, d), lambda i, j: (i, 0))],
        out_shape=[jax.ShapeDtypeStruct((m, n), F32), jax.ShapeDtypeStruct((m, d), BF16)],
        compiler_params=_cparams("parallel", "arbitrary"),
        name="norm_in_proj",
    )(x, nw.reshape(1, d), w)


def _head_proj_kernel(xn_ref, w_ref, *o_refs):
    n_out = len(o_refs) // 2
    j = pl.program_id(1)
    for k in range(n_out):
        @pl.when(j == k)
        def _(k=k):
            o_ref, ob_ref = o_refs[k], o_refs[n_out + k]
            res = _dot(xn_ref[...], w_ref[...])
            for h in range(o_ref.shape[1]):
                o_ref[:, h, :] = res[:, h * LANES:(h + 1) * LANES]
            ob_ref[...] = res.astype(BF16)


def _head_proj(xn, w, n_heads):
    m, d = xn.shape
    width = n_heads * LANES
    n_out = w.shape[1] // width
    tm = min(1024, m)
    return pl.pallas_call(
        _head_proj_kernel,
        grid=(m // tm, n_out),
        in_specs=[
            pl.BlockSpec((tm, d), lambda i, j: (i, 0)),
            pl.BlockSpec((d, width), lambda i, j: (0, j)),
        ],
        out_specs=([pl.BlockSpec((tm, n_heads, LANES), lambda i, j: (i, 0, 0)) for _ in range(n_out)]
                   + [pl.BlockSpec((tm, width), lambda i, j: (i, 0)) for _ in range(n_out)]),
        out_shape=([jax.ShapeDtypeStruct((m, n_heads, LANES), F32) for _ in range(n_out)]
                   + [jax.ShapeDtypeStruct((m, width), BF16) for _ in range(n_out)]),
        compiler_params=_cparams("parallel", "arbitrary"),
        name="head_proj",
    )(xn, w)


def _out_proj_kernel(ya_ref, yb_ref, wa_ref, wb_ref, x_ref, nw_ref, nwn_ref, o_ref, hn_ref):
    tm = x_ref.shape[0]
    grp = max(tm // 4, LANES)
    for r0 in range(0, tm, grp):
        rows = slice(r0, r0 + grp)
        m = _dot(ya_ref[rows, :], wa_ref[...]) + _dot(yb_ref[rows, :], wb_ref[...])
        o = x_ref[rows, :] + _rms(m, nw_ref[...])
        o_ref[rows, :] = o
        hn_ref[rows, :] = _rms(o, nwn_ref[...]).astype(BF16)


def _out_proj(ya, yb, w_out, x, nw, nw_next):
    m, d = x.shape
    da, db = ya.shape[1], yb.shape[1]
    assert da == db and w_out.shape[0] == da + db
    tm = min(512, m)
    row_spec = pl.BlockSpec((tm, d), lambda i: (i, 0))
    vec_spec = pl.BlockSpec((1, d), lambda i: (0, 0))
    return pl.pallas_call(
        _out_proj_kernel,
        grid=(m // tm,),
        in_specs=[
            pl.BlockSpec((tm, da), lambda i: (i, 0)),
            pl.BlockSpec((tm, db), lambda i: (i, 0)),
            pl.BlockSpec((da, d), lambda i: (0, 0)),
            pl.BlockSpec((db, d), lambda i: (1, 0)),
            row_spec, vec_spec, vec_spec,
        ],
        out_specs=[row_spec, row_spec],
        out_shape=[jax.ShapeDtypeStruct((m, d), F32), jax.ShapeDtypeStruct((m, d), BF16)],
        compiler_params=_cparams("parallel"),
        name="out_proj",
    )(ya, yb, w_out, w_out, x, nw.reshape(1, d), nw_next.reshape(1, d))


def _ffn_kernel(x_ref, hn_ref, nwb_ref, nwn_ref, wg_ref, wu_ref, wd_ref, o_ref, *rest, emit_next):
    acc_ref = rest[-1]
    f = pl.program_id(1)

    @pl.when(f == 0)
    def _():
        acc_ref[...] = jnp.zeros_like(acc_ref)

    hn = hn_ref[...]
    g = _dot(hn, wg_ref[...])
    u = _dot(hn, wu_ref[...])
    a = (_silu(g) * u).astype(BF16)
    acc_ref[...] += _dot(a, wd_ref[...])

    @pl.when(f == pl.num_programs(1) - 1)
    def _():
        o = x_ref[...] + _rms(acc_ref[...], nwb_ref[...])
        o_ref[...] = o
        if emit_next:
            rest[0][...] = _rms(o, nwn_ref[...]).astype(BF16)


def _ffn(x, hn, nw_post, nw_next, wg, wu, wd, layer):
    emit_next = nw_next is not None
    nw_next = nw_post if nw_next is None else nw_next
    m, d = x.shape
    dff = wg.shape[2]
    tm = min(512, m)
    tf = 512
    row_spec = pl.BlockSpec((tm, d), lambda i, f: (i, 0))
    vec_spec = pl.BlockSpec((1, d), lambda i, f: (0, 0))
    out_specs = [row_spec]
    out_shape = [jax.ShapeDtypeStruct((m, d), F32)]
    if emit_next:
        out_specs.append(row_spec)
        out_shape.append(jax.ShapeDtypeStruct((m, d), BF16))
    return pl.pallas_call(
        functools.partial(_ffn_kernel, emit_next=emit_next),
        grid=(m // tm, dff // tf),
        in_specs=[
            row_spec, row_spec, vec_spec, vec_spec,
            pl.BlockSpec((None, d, tf), lambda i, f: (layer, 0, f)),
            pl.BlockSpec((None, d, tf), lambda i, f: (layer, 0, f)),
            pl.BlockSpec((None, tf, d), lambda i, f: (layer, f, 0)),
        ],
        out_specs=out_specs,
        out_shape=out_shape,
        scratch_shapes=[pltpu.VMEM((tm, d), F32)],
        compiler_params=_cparams("parallel", "arbitrary"),
        name="ffn",
    )(x, hn, nw_post.reshape(1, d), nw_next.reshape(1, d), wg, wu, wd)


def _matmul_kernel(x_ref, w_ref, o_ref):
    o_ref[...] = _dot(x_ref[...], w_ref[...])


def _matmul(x, w, tn):
    m, d = x.shape
    n = w.shape[1]
    assert n % tn == 0
    tm = min(1024, m)
    return pl.pallas_call(
        _matmul_kernel,
        grid=(m // tm, n // tn),
        in_specs=[pl.BlockSpec((tm, d), lambda i, j: (i, 0)), pl.BlockSpec((d, tn), lambda i, j: (0, j))],
        out_specs=pl.BlockSpec((tm, tn), lambda i, j: (i, j)),
        out_shape=jax.ShapeDtypeStruct((m, n), F32),
        compiler_params=_cparams("parallel", "parallel"),
        name="in_proj",
    )(x, w)


def _conv_init(prev_ref, conv_buf):
    prev_ref[...] = jnp.zeros_like(prev_ref)
    prev_ref[SUBLANES - (CONV_W - 1):SUBLANES, :] = conv_buf


def _conv_rep(conv_w, conv_b):
    c = conv_w.shape[1]
    return jnp.repeat(conv_w, SUBLANES, axis=0), jnp.broadcast_to(conv_b.reshape(1, c), (SUBLANES, c))


def _conv_tile(prev_ref, x, cw_ref, cb_ref):
    tl, c = x.shape
    prev = prev_ref[...]
    row = lax.broadcasted_iota(jnp.int32, prev.shape, 0)
    tap = lambda i: cw_ref[i * SUBLANES:(i + 1) * SUBLANES, :][None]
    tiles = lambda v: v.reshape(tl // SUBLANES, SUBLANES, c)
    y = cb_ref[...][None] + tiles(x) * tap(CONV_W - 1)
    for k in range(1, CONV_W):
        xr = pltpu.roll(x, k, axis=0)
        head = jnp.where(row < k, pltpu.roll(prev, k, axis=0), xr[0:SUBLANES])
        xk = head if tl == SUBLANES else jnp.concatenate([head, xr[SUBLANES:]], axis=0)
        y = y + tiles(xk) * tap(CONV_W - 1 - k)
    prev_ref[...] = x[tl - SUBLANES:tl]
    return y.reshape(tl, c)


def _rglru_kernel(xa_ref, ga_ref, cbuf_ref, h0_ref, cw_ref, cb_ref, wr_ref, br_ref, wi_ref, bi_ref,
                  lam_ref, ya_ref, hlast_ref, xs_ref, a_ref, u_ref, h_ref, *, tl, n_blk):
    t = pl.program_id(1)

    @pl.when(t == 0)
    def _():
        _conv_init(xs_ref, cbuf_ref[...])
        h_ref[...] = h0_ref[...]

    xc = _conv_tile(xs_ref, xa_ref[...], cw_ref, cb_ref)
    xcb = xc.astype(BF16)
    r_pre = jnp.concatenate(
        [_dot(xcb[:, hb * LANES:(hb + 1) * LANES], wr_ref[hb]) for hb in range(n_blk)], axis=1)
    i_pre = jnp.concatenate(
        [_dot(xcb[:, hb * LANES:(hb + 1) * LANES], wi_ref[hb]) for hb in range(n_blk)], axis=1)
    r = _sigmoid(r_pre + br_ref[...])
    gi = _sigmoid(i_pre + bi_ref[...])
    nla = RG_C * r * _softplus(-lam_ref[...])
    a = jnp.exp(-nla)
    a_ref[...] = a
    u_ref[...] = jnp.sqrt(jnp.tanh(nla) * (a * a + 1.0)) * (gi * xc)

    row = lax.broadcasted_iota(jnp.int32, (SUBLANES, a_ref.shape[1]), 0)

    def group(gidx, h_prev):
        r0 = pl.multiple_of(gidx * SUBLANES, SUBLANES)
        ag = a_ref[pl.ds(r0, SUBLANES), :]
        ug = u_ref[pl.ds(r0, SUBLANES), :]
        s = 1
        while s < SUBLANES:
            a_sh = jnp.where(row >= s, pltpu.roll(ag, s, axis=0), 1.0)
            u_sh = jnp.where(row >= s, pltpu.roll(ug, s, axis=0), 0.0)
            ug = ag * u_sh + ug
            ag = ag * a_sh
            s *= 2
        hg = ag * h_prev + ug
        u_ref[pl.ds(r0, SUBLANES), :] = hg
        return hg[SUBLANES - 1:SUBLANES, :]

    h_last = lax.fori_loop(0, tl // SUBLANES, group, h_ref[...])
    h_ref[...] = h_last
    hlast_ref[...] = h_last
    ya_ref[...] = (u_ref[...] * _gelu_tanh(ga_ref[...])).astype(BF16)


def _rglru(proj, conv_buf, h0, conv_w, conv_b, w_r, b_r, w_i, b_i, lam, d_rnn):
    bn, l, _ = proj.shape
    tl = min(256, l)
    n_blk = w_r.shape[0]
    vec = lambda v: v.reshape(1, d_rnn)
    const2 = lambda b, t: (0, 0)
    kern = functools.partial(_rglru_kernel, tl=tl, n_blk=n_blk)
    ya, h_last = pl.pallas_call(
        kern,
        grid=(bn, l // tl),
        in_specs=[
            pl.BlockSpec((None, tl, d_rnn), lambda b, t: (b, t, 0)),
            pl.BlockSpec((None, tl, d_rnn), lambda b, t: (b, t, 1)),
            pl.BlockSpec((None, CONV_W - 1, d_rnn), lambda b, t: (b, 0, 0)),
            pl.BlockSpec((None, 1, d_rnn), lambda b, t: (b, 0, 0)),
            pl.BlockSpec((CONV_W * SUBLANES, d_rnn), const2),
            pl.BlockSpec((SUBLANES, d_rnn), const2),
            pl.BlockSpec(w_r.shape, lambda b, t: (0, 0, 0)),
            pl.BlockSpec((1, d_rnn), const2),
            pl.BlockSpec(w_i.shape, lambda b, t: (0, 0, 0)),
            pl.BlockSpec((1, d_rnn), const2),
            pl.BlockSpec((1, d_rnn), const2),
        ],
        out_specs=[
            pl.BlockSpec((None, tl, d_rnn), lambda b, t: (b, t, 0)),
            pl.BlockSpec((None, 1, d_rnn), lambda b, t: (b, 0, 0)),
        ],
        out_shape=[
            jax.ShapeDtypeStruct((bn, l, d_rnn), BF16),
            jax.ShapeDtypeStruct((bn, 1, d_rnn), F32),
        ],
        scratch_shapes=[
            pltpu.VMEM((SUBLANES, d_rnn), F32),
            pltpu.VMEM((tl, d_rnn), F32),
            pltpu.VMEM((tl, d_rnn), F32),
            pltpu.VMEM((1, d_rnn), F32),
        ],
        compiler_params=_cparams("parallel", "arbitrary"),
        name="rglru",
    )(proj, proj, conv_buf, h0.reshape(bn, 1, d_rnn), *_conv_rep(conv_w, conv_b), w_r, vec(b_r), w_i, vec(b_i),
      vec(lam))
    return ya, h_last.reshape(bn, d_rnn)


def _chunk_mask(kpos, qpos):
    shift = CHUNK.bit_length() - 1
    return jnp.right_shift(kpos, shift) <= jnp.right_shift(qpos, shift)


def _diff_lambda(lq1_ref, lk1_ref, lq2_ref, lk2_ref, lam_init):
    return (jnp.exp(jnp.sum(lq1_ref[...] * lk1_ref[...], axis=-1, keepdims=True))
            - jnp.exp(jnp.sum(lq2_ref[...] * lk2_ref[...], axis=-1, keepdims=True)) + lam_init)


ONES_ROWS = 16


def _attn_prompt_kernel(q_ref, k_ref, v_ref, lq1_ref, lk1_ref, lq2_ref, lk2_ref, sw_ref, o_ref,
                        vt_ref, sa_ref, sb_ref, m_ref, acc_ref, *, t, nblk, dh, scale, lam_init):
    i = pl.program_id(2)
    dv = 2 * dh

    @pl.when(i == 0)
    def _():
        ones = jnp.ones((ONES_ROWS, t), BF16)
        for jb in range(nblk):
            vt_ref[jb, :dv, :] = jnp.transpose(v_ref[jb * t:(jb + 1) * t, :].astype(F32)).astype(BF16)
            vt_ref[jb, dv:, :] = ones

    qt = jnp.transpose(q_ref[...] * (scale * math.log2(math.e)))
    sub = lax.broadcasted_iota(jnp.int32, qt.shape, 0)
    qt2 = jnp.concatenate([jnp.where(sub < dh, qt, 0.0), jnp.where(sub >= dh, qt, 0.0)],
                          axis=1).astype(BF16)
    m_ref[...] = jnp.full_like(m_ref, NEG)
    acc_ref[...] = jnp.zeros_like(acc_ref)

    def scores(j, s_ref):
        s_ref[...] = _dot(k_ref[pl.ds(pl.multiple_of(j * t, t), t), :], qt2)

    def update(j, s_ref, diagonal):
        if diagonal:
            qoff = lax.broadcasted_iota(jnp.int32, (1, 2 * t), 1)
            qoff = jnp.where(qoff >= t, qoff - t, qoff)
            s = jnp.concatenate(
                [s_ref[c * CHUNK:(c + 1) * CHUNK, :] + jnp.where(qoff >= c * CHUNK, 0.0, NEG)
                 for c in range(t // CHUNK)], axis=0)
        else:
            s = s_ref[...]
        m_prev = m_ref[...]
        m_new = jnp.maximum(m_prev, jnp.max(s, axis=0, keepdims=True))
        alpha = jnp.exp2(m_prev - m_new)
        p = jnp.exp2(s - m_new).astype(BF16)
        acc_ref[...] = alpha * acc_ref[...] + _dot(vt_ref[j], p)
        m_ref[...] = m_new

    scores(0, sa_ref)

    def pair(pidx, carry):
        j = 2 * pidx
        scores(j + 1, sb_ref)
        update(j, sa_ref, False)
        scores(j + 2, sa_ref)
        update(j + 1, sb_ref, False)
        return carry

    lax.fori_loop(0, lax.shift_right_logical(i, 1), pair, 0)

    @pl.when((i & 1) == 1)
    def _():
        scores(i, sb_ref)
        update(i - 1, sa_ref, False)
        sa_ref[...] = sb_ref[...]

    update(i, sa_ref, True)

    lam = _diff_lambda(lq1_ref, lk1_ref, lq2_ref, lk2_ref, lam_init)
    acc = acc_ref[0:dv, :]
    l = acc_ref[dv:dv + 1, :]
    o_t = acc[:, :t] / l[:, :t] - lam * (acc[:, t:] / l[:, t:])
    o_ref[...] = (_rms(jnp.transpose(o_t), sw_ref[...]) * (1.0 - lam_init)).astype(BF16)


def _attn_cached_kernel(q_ref, kc_hbm, vc_hbm, kn_ref, vn_ref, lq1_ref, lk1_ref, lq2_ref, lk2_ref, sw_ref,
                        o_ref, qt_ref, m_ref, acc_ref, kbuf, vbuf, sem, *, n_heads, lq, n_cache, tkc, pos0,
                        dh, scale, lam_init):
    b = pl.program_id(0)
    j = pl.program_id(1)
    dv = 2 * dh
    hsl = lambda h: slice(h * LANES, (h + 1) * LANES)

    def cache_copies(bi, step, slot):
        rows = pl.ds(pl.multiple_of(step * tkc, tkc), tkc)
        return ([pltpu.make_async_copy(kc_hbm.at[bi, rows, h, :], kbuf.at[slot, h], sem.at[slot, 0])
                 for h in range(n_heads)]
                + [pltpu.make_async_copy(vc_hbm.at[bi, rows, h, :], vbuf.at[slot, h], sem.at[slot, 1])
                   for h in range(n_heads)])

    def fetch(bi, step, slot):
        for cp in cache_copies(bi, step, slot):
            cp.start()

    @pl.when(jnp.logical_and(b == 0, j == 0))
    def _():
        fetch(0, 0, 0)

    @pl.when(j == 0)
    def _():
        sub = lax.broadcasted_iota(jnp.int32, (dv, lq), 0)
        for h in range(n_heads):
            qt = jnp.transpose(q_ref[:, hsl(h)] * (scale * math.log2(math.e)))
            qt_ref[h] = jnp.concatenate([jnp.where(sub < dh, qt, 0.0), jnp.where(sub >= dh, qt, 0.0)],
                                        axis=1).astype(BF16)
        m_ref[...] = jnp.full_like(m_ref, NEG)
        acc_ref[...] = jnp.zeros_like(acc_ref)

    def update(k_of, v_of, tk, bias):
        ones = jnp.ones((ONES_ROWS, tk), BF16)
        for h in range(n_heads):
            s = _dot(k_of(h).astype(BF16), qt_ref[h])
            if bias is not None:
                s = s + bias
            m_prev = m_ref[h]
            m_new = jnp.maximum(m_prev, jnp.max(s, axis=0, keepdims=True))
            alpha = jnp.exp2(m_prev - m_new)
            p = jnp.exp2(s - m_new).astype(BF16)
            vt = jnp.transpose(v_of(h).astype(F32)).astype(BF16)
            acc_ref[h] = alpha * acc_ref[h] + _dot(jnp.concatenate([vt, ones], axis=0), p)
            m_ref[h] = m_new

    @pl.when(j < n_cache)
    def _():
        slot = j & 1

        @pl.when(j + 1 < n_cache)
        def _():
            fetch(b, j + 1, 1 - slot)

        for cp in cache_copies(b, j, slot):
            cp.wait()
        update(lambda h: kbuf[slot, h], lambda h: vbuf[slot, h], tkc, None)

    @pl.when(j == n_cache)
    def _():
        @pl.when(b + 1 < pl.num_programs(0))
        def _():
            fetch(b + 1, 0, 0)

        col = lax.broadcasted_iota(jnp.int32, (lq, 2 * lq), 1)
        qpos = pos0 + jnp.where(col >= lq, col - lq, col)
        kpos = pos0 + lax.broadcasted_iota(jnp.int32, (lq, 2 * lq), 0)
        update(lambda h: kn_ref[:, hsl(h)], lambda h: vn_ref[:, hsl(h)], lq,
               jnp.where(_chunk_mask(kpos, qpos), 0.0, NEG))
        lam = _diff_lambda(lq1_ref, lk1_ref, lq2_ref, lk2_ref, lam_init)
        outs = []
        for h in range(n_heads):
            acc = acc_ref[h, 0:dv, :]
            l = acc_ref[h, dv:dv + 1, :]
            o_t = acc[:, :lq] / l[:, :lq] - lam * (acc[:, lq:] / l[:, lq:])
            outs.append(_rms(jnp.transpose(o_t), sw_ref[...]) * (1.0 - lam_init))
        o_ref[...] = jnp.concatenate(outs, axis=1).astype(BF16)


def _diff_attention(qsrc, q_col, k, v, k_cache, v_cache, lq1, lk1, lq2, lk2, subln_w, n_heads, layer):
    bn, l = k.shape[:2]
    dv = LANES
    dh = dv // 2
    d_att = n_heads * dv
    lam_init = 0.8 - 0.6 * math.exp(-0.3 * layer)
    scale = dh ** -0.5
    params = [lq1.reshape(1, dh), lk1.reshape(1, dh), lq2.reshape(1, dh), lk2.reshape(1, dh),
              subln_w.reshape(1, dv)]
    if k_cache is None:
        t = min(512, l)
        nblk = l // t
        assert t % CHUNK == 0 and l % t == 0
        small = lambda n: pl.BlockSpec((1, n), lambda b, h, i: (0, 0))
        kern = functools.partial(_attn_prompt_kernel, t=t, nblk=nblk, dh=dh, scale=scale, lam_init=lam_init)
        return pl.pallas_call(
            kern,
            grid=(bn, n_heads, nblk),
            in_specs=[
                pl.BlockSpec((None, t, dv), lambda b, h, i: (b, i, q_col + h)),
                pl.BlockSpec((None, l, dv), lambda b, h, i: (b, 0, h)),
                pl.BlockSpec((None, l, dv), lambda b, h, i: (b, 0, h)),
                small(dh), small(dh), small(dh), small(dh), small(dv),
            ],
            out_specs=pl.BlockSpec((None, t, dv), lambda b, h, i: (b, i, h)),
            out_shape=jax.ShapeDtypeStruct((bn, l, d_att), BF16),
            scratch_shapes=[
                pltpu.VMEM((nblk, dv + ONES_ROWS, t), BF16),
                pltpu.VMEM((t, 2 * t), F32),
                pltpu.VMEM((t, 2 * t), F32),
                pltpu.VMEM((1, 2 * t), F32),
                pltpu.VMEM((dv + ONES_ROWS, 2 * t), F32),
            ],
            compiler_params=_cparams("parallel", "parallel", "arbitrary"),
            name="diff_attention_prompt",
        )(qsrc, k, v, *params)

    pos0 = k_cache.shape[1]
    tkc = min(512, pos0)
    assert pos0 % CHUNK == 0 and pos0 % tkc == 0 and l % SUBLANES == 0
    n_cache = pos0 // tkc
    small = lambda n: pl.BlockSpec((1, n), lambda b, j: (0, 0))
    kern = functools.partial(_attn_cached_kernel, n_heads=n_heads, lq=l, n_cache=n_cache, tkc=tkc, pos0=pos0,
                             dh=dh, scale=scale, lam_init=lam_init)
    cache_spec = pl.BlockSpec(memory_space=pl.ANY)
    new_spec = pl.BlockSpec((None, l, d_att), lambda b, j: (b, 0, 0))
    return pl.pallas_call(
        kern,
        grid=(bn, n_cache + 1),
        in_specs=[
            pl.BlockSpec((None, l, d_att), lambda b, j: (b, 0, q_col * dv // d_att)),
            cache_spec, cache_spec, new_spec, new_spec,
            small(dh), small(dh), small(dh), small(dh), small(dv),
        ],
        out_specs=pl.BlockSpec((None, l, d_att), lambda b, j: (b, 0, 0)),
        out_shape=jax.ShapeDtypeStruct((bn, l, d_att), BF16),
        scratch_shapes=[
            pltpu.VMEM((n_heads, dv, 2 * l), BF16),
            pltpu.VMEM((n_heads, 1, 2 * l), F32),
            pltpu.VMEM((n_heads, dv + ONES_ROWS, 2 * l), F32),
            pltpu.VMEM((2, n_heads, tkc, dv), F32),
            pltpu.VMEM((2, n_heads, tkc, dv), F32),
            pltpu.SemaphoreType.DMA((2, 2)),
        ],
        compiler_params=_cparams("arbitrary", "arbitrary"),
        name="diff_attention_cached",
    )(qsrc, k_cache, v_cache, k, v, *params)


def _ssd_kernel(z_ref, x_ref, bc_ref, dt_ref, cbx_ref, cbbc_ref, s0_ref, cwx_ref, cbiasx_ref, cwbc_ref,
                cbiasbc_ref, dtb_ref, alog_ref, dskip_ref, nw_ref, expand_ref,
                y_ref, s_ref, xsx_ref, xsbc_ref, *, n_grp, hpg, p_dim, n_dim):
    t = pl.program_id(1)
    q = CHUNK
    d_ssd = n_grp * hpg * p_dim
    gw = hpg * p_dim

    @pl.when(t == 0)
    def _():
        _conv_init(xsx_ref, cbx_ref[...])
        _conv_init(xsbc_ref, cbbc_ref[...])
        s_ref[...] = s0_ref[...]

    xs = _silu(_conv_tile(xsx_ref, x_ref[...], cwx_ref, cbiasx_ref))
    bcs = _silu(_conv_tile(xsbc_ref, bc_ref[...], cwbc_ref, cbiasbc_ref))

    dt = _softplus(dt_ref[...] + dtb_ref[...])
    a = -jnp.exp(alog_ref[...])
    acum = _cumsum_rows(dt * a)
    a_last = acum[q - 1:q, :]
    wst = jnp.exp(a_last - acum) * dt
    eac = jnp.exp(acum)
    acum_t = jnp.transpose(acum)

    stacked = jnp.concatenate([dt, wst, eac], axis=0)
    hi = stacked.astype(BF16)
    rem = stacked - hi.astype(F32)
    mid = rem.astype(BF16)
    lo = (rem - mid.astype(F32)).astype(BF16)
    expand = expand_ref[...]
    expd = _dot(hi, expand) + _dot(mid, expand) + _dot(lo, expand)
    dt_e, wst_e, eac_e = expd[0:q], expd[q:2 * q], expd[2 * q:3 * q]

    xdt = (xs * dt_e).astype(BF16)
    xw = (xs * wst_e).astype(BF16)
    row = lax.broadcasted_iota(jnp.int32, (q, q), 0)
    col = lax.broadcasted_iota(jnp.int32, (q, q), 1)
    causal = col <= row
    lane = lax.broadcasted_iota(jnp.int32, (q, LANES), 1)
    heads_per_tile = LANES // p_dim

    bgs = [bcs[:, g * n_dim:(g + 1) * n_dim] for g in range(n_grp)]
    cgs = [bcs[:, (n_grp + g) * n_dim:(n_grp + g + 1) * n_dim].astype(BF16) for g in range(n_grp)]
    cbs = [lax.dot_general(cgs[g], bgs[g].astype(BF16), _NT, preferred_element_type=F32) for g in range(n_grp)]
    bg_ts = [jnp.transpose(bgs[g]).astype(BF16) for g in range(n_grp)]
    wdec = []
    for h in range(n_grp * hpg):
        dec = jnp.exp(jnp.where(causal, acum[:, h:h + 1] - acum_t[h:h + 1, :], -jnp.inf))
        wdec.append((cbs[h // hpg] * dec).astype(BF16))

    y_parts = []
    for g in range(n_grp):
        s_prev = s_ref[g]
        y_off = _dot(cgs[g], s_prev.astype(BF16)) * eac_e[:, g * gw:(g + 1) * gw]
        y_diag = []
        for tile in range(gw // LANES):
            c0 = g * gw + tile * LANES
            x_tile = xdt[:, c0:c0 + LANES]
            acc = None
            for hh in range(heads_per_tile):
                yh = _dot(wdec[c0 // p_dim + hh], x_tile)
                sel = jnp.logical_and(lane >= hh * p_dim, lane < (hh + 1) * p_dim)
                acc = jnp.where(sel, yh, 0.0) if acc is None else jnp.where(sel, yh, acc)
            y_diag.append(acc)
        y_parts.append(jnp.concatenate(y_diag, axis=1) + y_off)
        st = _dot(bg_ts[g], xw[:, g * gw:(g + 1) * gw])
        s_ref[g] = eac_e[q - 1:q, g * gw:(g + 1) * gw] * s_prev + st
    y = jnp.concatenate(y_parts, axis=1)
    y = (y + dskip_ref[...] * xs) * _silu(z_ref[...])
    nw = nw_ref[...]
    y_ref[...] = jnp.concatenate(
        [_rms(y[:, g * gw:(g + 1) * gw], nw[:, g * gw:(g + 1) * gw]) for g in range(n_grp)],
        axis=1).astype(BF16)


def _ssd(proj, cols, conv_buf, s0, conv_w, conv_b, dt_bias, a_log, d_skip, norm_w, n_grp, n_dim, p_dim):
    bn, l, _ = proj.shape
    n_heads = dt_bias.shape[0]
    hpg = n_heads // n_grp
    d_ssd = n_heads * p_dim
    d_bc = 2 * n_grp * n_dim
    gw = hpg * p_dim
    q = CHUNK
    assert l % q == 0 and n_heads <= LANES and LANES % p_dim == 0
    z_col, x_col, bc_col, dt_col = cols
    pad = lambda v: jnp.pad(v.astype(F32), (0, LANES - n_heads)).reshape(1, LANES)
    expand = (jnp.arange(LANES)[:, None] == (jnp.arange(d_ssd)[None, :] // p_dim)).astype(BF16)
    s0_t = s0.reshape(bn, n_grp, hpg, p_dim, n_dim).transpose(0, 1, 4, 2, 3).reshape(bn, n_grp, n_dim, gw)
    const2 = lambda b, t: (0, 0)
    kern = functools.partial(_ssd_kernel, n_grp=n_grp, hpg=hpg, p_dim=p_dim, n_dim=n_dim)
    y, s_last = pl.pallas_call(
        kern,
        grid=(bn, l // q),
        in_specs=[
            pl.BlockSpec((None, q, d_ssd), lambda b, t: (b, t, z_col)),
            pl.BlockSpec((None, q, d_ssd), lambda b, t: (b, t, x_col)),
            pl.BlockSpec((None, q, d_bc), lambda b, t: (b, t, bc_col)),
            pl.BlockSpec((None, q, LANES), lambda b, t: (b, t, dt_col)),
            pl.BlockSpec((None, CONV_W - 1, d_ssd), lambda b, t: (b, 0, 0)),
            pl.BlockSpec((None, CONV_W - 1, d_bc), lambda b, t: (b, 0, 0)),
            pl.BlockSpec((None, n_grp, n_dim, gw), lambda b, t: (b, 0, 0, 0)),
            pl.BlockSpec((CONV_W * SUBLANES, d_ssd), const2),
            pl.BlockSpec((SUBLANES, d_ssd), const2),
            pl.BlockSpec((CONV_W * SUBLANES, d_bc), const2),
            pl.BlockSpec((SUBLANES, d_bc), const2),
            pl.BlockSpec((1, LANES), const2),
            pl.BlockSpec((1, LANES), const2),
            pl.BlockSpec((1, d_ssd), const2),
            pl.BlockSpec((1, d_ssd), const2),
            pl.BlockSpec((LANES, d_ssd), const2),
        ],
        out_specs=[
            pl.BlockSpec((None, q, d_ssd), lambda b, t: (b, t, 0)),
            pl.BlockSpec((None, n_grp, n_dim, gw), lambda b, t: (b, 0, 0, 0)),
        ],
        out_shape=[
            jax.ShapeDtypeStruct((bn, l, d_ssd), BF16),
            jax.ShapeDtypeStruct((bn, n_grp, n_dim, gw), F32),
        ],
        scratch_shapes=[
            pltpu.VMEM((SUBLANES, d_ssd), F32),
            pltpu.VMEM((SUBLANES, d_bc), F32),
        ],
        compiler_params=_cparams("parallel", "arbitrary"),
        name="ssd",
    )(proj, proj, proj, proj, conv_buf[:, :, :d_ssd], conv_buf[:, :, d_ssd:], s0_t,
      *_conv_rep(conv_w[:, :d_ssd], conv_b[:d_ssd]), *_conv_rep(conv_w[:, d_ssd:], conv_b[d_ssd:]),
      pad(dt_bias), pad(a_log), jnp.repeat(d_skip.astype(F32), p_dim).reshape(1, d_ssd),
      norm_w.reshape(1, d_ssd), expand)
    s_last = s_last.reshape(bn, n_grp, n_dim, hpg, p_dim).transpose(0, 1, 3, 4, 2).reshape(bn, n_heads, p_dim, n_dim)
    return y, s_last


def _hgrn_kernel(q_ref, f_ref, i_ref, lb_ref, nw_ref, s0_ref, y_ref, slast_ref, *, n_heads):
    t = pl.program_id(1)
    c = CHUNK
    nsb = c // HG_SUB

    @pl.when(t == 0)
    def _():
        slast_ref[...] = s0_ref[...]

    lbp = lb_ref[...]
    e = jnp.exp(lbp - jnp.max(lbp, axis=0, keepdims=True))
    lb = e[1:2, :] / jnp.sum(e, axis=0, keepdims=True)
    g = lb + (1.0 - lb) * _sigmoid(f_ref[...])
    logg = jnp.log(g)
    kk = 1.0 - g
    qq = _silu(q_ref[...])
    b = _cumsum_rows(logg)
    iv = i_ref[...]

    zero_blk = jnp.zeros((HG_SUB, LANES), F32)
    r2 = lax.broadcasted_iota(jnp.int32, (c, c), 0)
    c2 = lax.broadcasted_iota(jnp.int32, (c, c), 1)
    causal = c2 <= r2
    nw = nw_ref[...]

    b_last = b[c - 1:c, :]
    qe = (qq * jnp.exp(b)).astype(BF16)
    kd = (kk * jnp.exp(b_last - b)).astype(BF16)
    s_decay = jnp.exp(b_last)
    ivb = iv.astype(BF16)
    sb = lambda x, r: x[r * HG_SUB:(r + 1) * HG_SUB, :]
    m_off = [b[a * HG_SUB - 1:a * HG_SUB, :] for a in range(1, nsb)]
    m_mid = [b[r * HG_SUB + HG_SUB // 2 - 1:r * HG_SUB + HG_SUB // 2, :] for r in range(nsb)]
    q_off = [None] + [sb(qq, a) * jnp.exp(sb(b, a) - m_off[a - 1]) for a in range(1, nsb)]
    q_mid = [sb(qq, r) * jnp.exp(sb(b, r) - m_mid[r]) for r in range(nsb)]
    k_mid = [sb(kk, r) * jnp.exp(m_mid[r] - sb(b, r)) for r in range(nsb)]
    k_off = [[sb(kk, r) * jnp.exp(m_off[a - 1] - sb(b, r)) if a > r else None for a in range(1, nsb)]
             for r in range(nsb)]

    atts = []
    for h in range(n_heads):
        sl = slice(h * LANES, (h + 1) * LANES)
        q_rows, k_rows = [], []
        for r in range(nsb):
            q_parts = [q_off[a][:, sl] if a == r else zero_blk for a in range(1, nsb)]
            k_parts = [k_off[r][a - 1][:, sl] if a > r else zero_blk for a in range(1, nsb)]
            q_parts += [q_mid[r][:, sl] if d == r else zero_blk for d in range(nsb)]
            k_parts += [k_mid[r][:, sl] if d == r else zero_blk for d in range(nsb)]
            q_rows.append(jnp.concatenate(q_parts, axis=1))
            k_rows.append(jnp.concatenate(k_parts, axis=1))
        qcat = jnp.concatenate(q_rows, axis=0).astype(BF16)
        kcat = jnp.concatenate(k_rows, axis=0).astype(BF16)
        att = lax.dot_general(qcat, kcat, _NT, preferred_element_type=F32)
        atts.append(jnp.where(causal, att, 0.0).astype(BF16))
    outs = []
    for h in range(n_heads):
        sl = slice(h * LANES, (h + 1) * LANES)
        s_t = slast_ref[h]
        o = _dot(atts[h], ivb[:, sl])
        o = o + lax.dot_general(qe[:, sl], s_t.astype(BF16), _NT, preferred_element_type=F32)
        slast_ref[h] = s_decay[:, sl] * s_t + _dot(jnp.transpose(iv[:, sl]).astype(BF16), kd[:, sl])
        outs.append(_rms(o, nw))
    y_ref[...] = jnp.concatenate(outs, axis=1).astype(BF16)


def _hgrn(proj, cols, s0, lb_param, norm_w, n_heads):
    bn, l, _ = proj.shape
    d = n_heads * LANES
    c = CHUNK
    assert l % c == 0
    q_col, f_col, i_col = cols
    depth = lb_param.shape[0]
    assert depth == 2
    s0_t = jnp.swapaxes(s0, -1, -2)
    kern = functools.partial(_hgrn_kernel, n_heads=n_heads)
    y, s_last = pl.pallas_call(
        kern,
        grid=(bn, l // c),
        in_specs=[
            pl.BlockSpec((None, c, d), lambda b, t: (b, t, q_col)),
            pl.BlockSpec((None, c, d), lambda b, t: (b, t, f_col)),
            pl.BlockSpec((None, c, d), lambda b, t: (b, t, i_col)),
            pl.BlockSpec((depth, d), lambda b, t: (0, 0)),
            pl.BlockSpec((1, LANES), lambda b, t: (0, 0)),
            pl.BlockSpec((None, n_heads, LANES, LANES), lambda b, t: (b, 0, 0, 0)),
        ],
        out_specs=[
            pl.BlockSpec((None, c, d), lambda b, t: (b, t, 0)),
            pl.BlockSpec((None, n_heads, LANES, LANES), lambda b, t: (b, 0, 0, 0)),
        ],
        out_shape=[
            jax.ShapeDtypeStruct((bn, l, d), BF16),
            jax.ShapeDtypeStruct((bn, n_heads, LANES, LANES), F32),
        ],
        compiler_params=_cparams("parallel", "arbitrary"),
        name="hgrn2",
    )(proj, proj, proj, lb_param, norm_w.reshape(1, LANES), s0_t)
    return y, jnp.swapaxes(s_last, -1, -2)


def kernel(x_prompt, x_sample, cache_diff_k, cache_diff_v, state_rglru_conv, state_rglru_h, state_ssd_conv,
           state_ssd, state_hgrn, norm_w, l0_w_in, l0_conv_w, l0_conv_b, l0_rg_w_r, l0_rg_b_r, l0_rg_w_i,
           l0_rg_b_i, l0_rg_lambda, l0_lq1, l0_lk1, l0_lq2, l0_lk2, l0_subln_w, l0_w_out, l1_w_in, l1_conv_w,
           l1_conv_b, l1_dt_bias, l1_a_log, l1_d_skip, l1_ssd_norm_w, l1_hg_lower_bound, l1_hg_norm_w, l1_w_out,
           ffn_w_gate, ffn_w_up, ffn_w_down):
    d_model = x_prompt.shape[-1]
    d_rnn = l0_conv_w.shape[1]
    n_heads_b, dv_b = cache_diff_k.shape[2], cache_diff_k.shape[3]
    d_att = n_heads_b * dv_b
    n_heads_c, p_c, n_c = state_ssd.shape[1:]
    d_ssd = n_heads_c * p_c
    d_xbc = l1_conv_w.shape[1]
    d_bc = d_xbc - d_ssd
    n_grp = d_bc // (2 * n_c)
    n_heads_d, dk_d, dv_d = state_hgrn.shape[1:]
    d_hg = n_heads_d * dk_d
    assert dv_b == LANES and dk_d == LANES and dv_d == LANES and d_rnn == d_att == d_ssd == d_hg

    w_in0_main = l0_w_in[:, :2 * d_rnn + d_att].astype(BF16)
    w_in0_kv = l0_w_in[:, 2 * d_rnn + d_att:].astype(BF16)
    o1 = d_ssd
    o2 = o1 + d_xbc
    o3 = o2 + n_heads_c
    w1 = l1_w_in
    w_in1 = jnp.concatenate(
        [w1[:, :o1], w1[:, o1:o1 + d_ssd], w1[:, o3:], w1[:, o1 + d_ssd:o2],
         jnp.pad(w1[:, o2:o3], ((0, 0), (0, LANES - n_heads_c)))], axis=1).astype(BF16)
    w_out0 = l0_w_out.astype(BF16)
    w_out1 = l1_w_out.astype(BF16)
    wg = ffn_w_gate.astype(BF16)
    wu = ffn_w_up.astype(BF16)
    wd = ffn_w_down.astype(BF16)
    w_r = l0_rg_w_r.astype(BF16)
    w_i = l0_rg_w_i.astype(BF16)
    tn0 = 1024
    tn1 = 1152
    assert w_in0_main.shape[1] % tn0 == 0 and w_in1.shape[1] % tn1 == 0
    ssd_cols = (0, 1, (2 * d_ssd + 3 * d_hg) // d_bc, (2 * d_ssd + 3 * d_hg + d_bc) // LANES)
    hg_cols = (2, 3, 4)

    def trunk(x, k_cache, v_cache, rg_conv, rg_h, ssd_conv, ssd_s, hg_s):
        bn, l, _ = x.shape
        m = bn * l
        x0 = x.reshape(m, d_model)
        proj0, xn0 = _norm_matmul(x0, norm_w[0, 0], w_in0_main, tn0)
        proj0 = proj0.reshape(bn, l, -1)
        k_new, v_new, k_bf, v_bf = _head_proj(xn0, w_in0_kv, n_heads_b)
        k_new = k_new.reshape(bn, l, n_heads_b, dv_b)
        v_new = v_new.reshape(bn, l, n_heads_b, dv_b)
        ya, rg_h_new = _rglru(proj0, rg_conv, rg_h, l0_conv_w, l0_conv_b, w_r, l0_rg_b_r, w_i, l0_rg_b_i,
                              l0_rg_lambda, d_rnn)
        yb = _diff_attention(proj0, 2 * d_rnn // LANES, k_bf.reshape(bn, l, d_att), v_bf.reshape(bn, l, d_att),
                             k_cache, v_cache, l0_lq1, l0_lk1, l0_lq2, l0_lk2, l0_subln_w, n_heads_b, 0)
        rg_conv_new = proj0[:, l - (CONV_W - 1):, :d_rnn]
        x1, hn1 = _out_proj(ya.reshape(m, d_rnn), yb.reshape(m, d_att), w_out0, x0, norm_w[0, 1], norm_w[0, 2])
        x2, xn2 = _ffn(x1, hn1, norm_w[0, 3], norm_w[1, 0], wg, wu, wd, 0)
        proj1 = _matmul(xn2, w_in1, tn1).reshape(bn, l, -1)
        ys, ssd_s_new = _ssd(proj1, ssd_cols, ssd_conv, ssd_s, l1_conv_w, l1_conv_b, l1_dt_bias, l1_a_log,
                             l1_d_skip, l1_ssd_norm_w, n_grp, n_c, p_c)
        yh, hg_s_new = _hgrn(proj1, hg_cols, hg_s, l1_hg_lower_bound, l1_hg_norm_w, n_heads_d)
        tail = proj1[:, l - (CONV_W - 1):, :]
        ssd_conv_new = jnp.concatenate(
            [tail[:, :, d_ssd:2 * d_ssd], tail[:, :, 2 * d_ssd + 3 * d_hg:2 * d_ssd + 3 * d_hg + d_bc]], axis=-1)
        x3, hn3 = _out_proj(ys.reshape(m, d_ssd), yh.reshape(m, d_hg), w_out1, x2, norm_w[1, 1], norm_w[1, 2])
        x4 = _ffn(x3, hn3, norm_w[1, 3], None, wg, wu, wd, 1)[0]
        return (x4.reshape(bn, l, d_model), k_new, v_new, rg_conv_new, rg_h_new, ssd_conv_new, ssd_s_new, hg_s_new)

    bp = x_prompt.shape[0]
    zeros = lambda *s: jnp.zeros(s, F32)
    outs_p = trunk(x_prompt, None, None, zeros(bp, CONV_W - 1, d_rnn), zeros(bp, d_rnn),
                   zeros(bp, CONV_W - 1, d_xbc), zeros(bp, n_heads_c, p_c, n_c), zeros(bp, n_heads_d, dk_d, dv_d))
    outs_s = trunk(x_sample, cache_diff_k, cache_diff_v, state_rglru_conv, state_rglru_h, state_ssd_conv,
                   state_ssd, state_hgrn)
    return (outs_p[0], outs_s[0]) + tuple(outs_p[1:]) + tuple(outs_s[1:])
```

```python
import functools
import math

import jax
import jax.numpy as jnp
from jax import lax
from jax.experimental import pallas as pl
from jax.experimental.pallas import tpu as pltpu

F32 = jnp.float32
BF16 = jnp.bfloat16

EPS = 1e-6
CHUNK = 64
CONV_W = 4
RG_C = 8.0
LANES = 128
SUBLANES = 8
HG_SUB = 16
NEG = -1e30
VMEM_LIMIT_BYTES = 56 * 1024 * 1024

_NT = (((1,), (1,)), ((), ()))
_TN = (((0,), (0,)), ((), ()))


def _cparams(*sem):
    return pltpu.CompilerParams(dimension_semantics=sem, vmem_limit_bytes=VMEM_LIMIT_BYTES)


def _sigmoid(x):
    return 0.5 * jnp.tanh(0.5 * x) + 0.5


def _silu(x):
    h = 0.5 * x
    return h * jnp.tanh(h) + h


def _softplus(x):
    return jnp.maximum(x, 0.0) + jnp.log1p(jnp.exp(-jnp.abs(x)))


def _gelu_tanh(x):
    c = math.sqrt(2.0 / math.pi)
    return x * (0.5 * (1.0 + jnp.tanh(c * (x + 0.044715 * (x * x * x)))))


def _rms(x, w):
    return x * lax.rsqrt(jnp.mean(x * x, axis=-1, keepdims=True) + EPS) * w


def _dot(a, b):
    return jnp.dot(a, b, preferred_element_type=F32)


def _cumsum_rows(x):
    n = x.shape[0]
    row = lax.broadcasted_iota(jnp.int32, x.shape, 0)
    s = 1
    while s < n:
        x = x + jnp.where(row >= s, pltpu.roll(x, s, axis=0), 0.0)
        s *= 2
    return x


def _norm_matmul_kernel(x_ref, nw_ref, w_ref, o_ref, xn_ref):
    @pl.when(pl.program_id(1) == 0)
    def _():
        xn_ref[...] = _rms(x_ref[...], nw_ref[...]).astype(BF16)

    o_ref[...] = _dot(xn_ref[...], w_ref[...])


def _norm_matmul(x, nw, w, tn):
    m, d = x.shape
    n = w.shape[1]
    assert n % tn == 0
    tm = min(1024, m)
    return pl.pallas_call(
        _norm_matmul_kernel,
        grid=(m // tm, n // tn),
        in_specs=[
            pl.BlockSpec((tm, d), lambda i, j: (i, 0)),
            pl.BlockSpec((1, d), lambda i, j: (0, 0)),
            pl.BlockSpec((d, tn), lambda i, j: (0, j)),
        ],
        out_specs=[pl.BlockSpec((tm, tn), lambda i, j: (i, j)), pl.BlockSpec((tm, d), lambda i, j: (i, 0))],
        out_shape=[jax.ShapeDtypeStruct((m, n), F32), jax.ShapeDtypeStruct((m, d), BF16)],
        compiler_params=_cparams("parallel", "arbitrary"),
        name="norm_in_proj",
    )(x, nw.reshape(1, d), w)


def _head_proj_kernel(xn_ref, w_ref, *o_refs):
    n_out = len(o_refs) // 2
    j = pl.program_id(1)
    for k in range(n_out):
        @pl.when(j == k)
        def _(k=k):
            o_ref, ob_ref = o_refs[k], o_refs[n_out + k]
            res = _dot(xn_ref[...], w_ref[...])
            for h in range(o_ref.shape[1]):
                o_ref[:, h, :] = res[:, h * LANES:(h + 1) * LANES]
            ob_ref[...] = res.astype(BF16)


def _head_proj(xn, w, n_heads):
    m, d = xn.shape
    width = n_heads * LANES
    n_out = w.shape[1] // width
    tm = min(1024, m)
    return pl.pallas_call(
        _head_proj_kernel,
        grid=(m // tm, n_out),
        in_specs=[
            pl.BlockSpec((tm, d), lambda i, j: (i, 0)),
            pl.BlockSpec((d, width), lambda i, j: (0, j)),
        ],
        out_specs=([pl.BlockSpec((tm, n_heads, LANES), lambda i, j: (i, 0, 0)) for _ in range(n_out)]
                   + [pl.BlockSpec((tm, width), lambda i, j: (i, 0)) for _ in range(n_out)]),
        out_shape=([jax.ShapeDtypeStruct((m, n_heads, LANES), F32) for _ in range(n_out)]
                   + [jax.ShapeDtypeStruct((m, width), BF16) for _ in range(n_out)]),
        compiler_params=_cparams("parallel", "arbitrary"),
        name="head_proj",
    )(xn, w)


def _out_proj_kernel(ya_ref, yb_ref, wa_ref, wb_ref, x_ref, nw_ref, nwn_ref, o_ref, hn_ref):
    tm = x_ref.shape[0]
    grp = max(tm // 4, LANES)
    for r0 in range(0, tm, grp):
        rows = slice(r0, r0 + grp)
        m = _dot(ya_ref[rows, :], wa_ref[...]) + _dot(yb_ref[rows, :], wb_ref[...])
        o = x_ref[rows, :] + _rms(m, nw_ref[...])
        o_ref[rows, :] = o
        hn_ref[rows, :] = _rms(o, nwn_ref[...]).astype(BF16)


def _out_proj(ya, yb, w_out, x, nw, nw_next):
    m, d = x.shape
    da, db = ya.shape[1], yb.shape[1]
    assert da == db and w_out.shape[0] == da + db
    tm = min(512, m)
    row_spec = pl.BlockSpec((tm, d), lambda i: (i, 0))
    vec_spec = pl.BlockSpec((1, d), lambda i: (0, 0))
    return pl.pallas_call(
        _out_proj_kernel,
        grid=(m // tm,),
        in_specs=[
            pl.BlockSpec((tm, da), lambda i: (i, 0)),
            pl.BlockSpec((tm, db), lambda i: (i, 0)),
            pl.BlockSpec((da, d), lambda i: (0, 0)),
            pl.BlockSpec((db, d), lambda i: (1, 0)),
            row_spec, vec_spec, vec_spec,
        ],
        out_specs=[row_spec, row_spec],
        out_shape=[jax.ShapeDtypeStruct((m, d), F32), jax.ShapeDtypeStruct((m, d), BF16)],
        compiler_params=_cparams("parallel"),
        name="out_proj",
    )(ya, yb, w_out, w_out, x, nw.reshape(1, d), nw_next.reshape(1, d))


def _ffn_kernel(x_ref, hn_ref, nwb_ref, nwn_ref, wg_ref, wu_ref, wd_ref, o_ref, *rest, emit_next):
    acc_ref = rest[-1]
    f = pl.program_id(1)

    @pl.when(f == 0)
    def _():
        acc_ref[...] = jnp.zeros_like(acc_ref)

    hn = hn_ref[...]
    g = _dot(hn, wg_ref[...])
    u = _dot(hn, wu_ref[...])
    a = (_silu(g) * u).astype(BF16)
    acc_ref[...] += _dot(a, wd_ref[...])

    @pl.when(f == pl.num_programs(1) - 1)
    def _():
        o = x_ref[...] + _rms(acc_ref[...], nwb_ref[...])
        o_ref[...] = o
        if emit_next:
            rest[0][...] = _rms(o, nwn_ref[...]).astype(BF16)


def _ffn(x, hn, nw_post, nw_next, wg, wu, wd, layer):
    emit_next = nw_next is not None
    nw_next = nw_post if nw_next is None else nw_next
    m, d = x.shape
    dff = wg.shape[2]
    tm = min(512, m)
    tf = 512
    row_spec = pl.BlockSpec((tm, d), lambda i, f: (i, 0))
    vec_spec = pl.BlockSpec((1, d), lambda i, f: (0, 0))
    out_specs = [row_spec]
    out_shape = [jax.ShapeDtypeStruct((m, d), F32)]
    if emit_next:
        out_specs.append(row_spec)
        out_shape.append(jax.ShapeDtypeStruct((m, d), BF16))
    return pl.pallas_call(
        functools.partial(_ffn_kernel, emit_next=emit_next),
        grid=(m // tm, dff // tf),
        in_specs=[
            row_spec, row_spec, vec_spec, vec_spec,
            pl.BlockSpec((None, d, tf), lambda i, f: (layer, 0, f)),
            pl.BlockSpec((None, d, tf), lambda i, f: (layer, 0, f)),
            pl.BlockSpec((None, tf, d), lambda i, f: (layer, f, 0)),
        ],
        out_specs=out_specs,
        out_shape=out_shape,
        scratch_shapes=[pltpu.VMEM((tm, d), F32)],
        compiler_params=_cparams("parallel", "arbitrary"),
        name="ffn",
    )(x, hn, nw_post.reshape(1, d), nw_next.reshape(1, d), wg, wu, wd)


def _matmul_kernel(x_ref, w_ref, o_ref):
    o_ref[...] = _dot(x_ref[...], w_ref[...])


def _matmul(x, w, tn):
    m, d = x.shape
    n = w.shape[1]
    assert n % tn == 0
    tm = min(1024, m)
    return pl.pallas_call(
        _matmul_kernel,
        grid=(m // tm, n // tn),
        in_specs=[pl.BlockSpec((tm, d), lambda i, j: (i, 0)), pl.BlockSpec((d, tn), lambda i, j: (0, j))],
        out_specs=pl.BlockSpec((tm, tn), lambda i, j: (i, j)),
        out_shape=jax.ShapeDtypeStruct((m, n), F32),
        compiler_params=_cparams("parallel", "parallel"),
        name="in_proj",
    )(x, w)


def _conv_init(prev_ref, conv_buf):
    prev_ref[...] = jnp.zeros_like(prev_ref)
    prev_ref[SUBLANES - (CONV_W - 1):SUBLANES, :] = conv_buf


def _conv_rep(conv_w, conv_b):
    c = conv_w.shape[1]
    return jnp.repeat(conv_w, SUBLANES, axis=0), jnp.broadcast_to(conv_b.reshape(1, c), (SUBLANES, c))


def _conv_tile(prev_ref, x, cw_ref, cb_ref):
    tl, c = x.shape
    prev = prev_ref[...]
    row = lax.broadcasted_iota(jnp.int32, prev.shape, 0)
    tap = lambda i: cw_ref[i * SUBLANES:(i + 1) * SUBLANES, :][None]
    tiles = lambda v: v.reshape(tl // SUBLANES, SUBLANES, c)
    y = cb_ref[...][None] + tiles(x) * tap(CONV_W - 1)
    for k in range(1, CONV_W):
        xr = pltpu.roll(x, k, axis=0)
        head = jnp.where(row < k, pltpu.roll(prev, k, axis=0), xr[0:SUBLANES])
        xk = head if tl == SUBLANES else jnp.concatenate([head, xr[SUBLANES:]], axis=0)
        y = y + tiles(xk) * tap(CONV_W - 1 - k)
    prev_ref[...] = x[tl - SUBLANES:tl]
    return y.reshape(tl, c)


def _rglru_kernel(xa_ref, ga_ref, cbuf_ref, h0_ref, cw_ref, cb_ref, wr_ref, br_ref, wi_ref, bi_ref,
                  lam_ref, ya_ref, hlast_ref, xs_ref, a_ref, u_ref, h_ref, *, tl, n_blk):
    t = pl.program_id(1)

    @pl.when(t == 0)
    def _():
        _conv_init(xs_ref, cbuf_ref[...])
        h_ref[...] = h0_ref[...]

    xc = _conv_tile(xs_ref, xa_ref[...], cw_ref, cb_ref)
    xcb = xc.astype(BF16)
    r_pre = jnp.concatenate(
        [_dot(xcb[:, hb * LANES:(hb + 1) * LANES], wr_ref[hb]) for hb in range(n_blk)], axis=1)
    i_pre = jnp.concatenate(
        [_dot(xcb[:, hb * LANES:(hb + 1) * LANES], wi_ref[hb]) for hb in range(n_blk)], axis=1)
    r = _sigmoid(r_pre + br_ref[...])
    gi = _sigmoid(i_pre + bi_ref[...])
    nla = RG_C * r * _softplus(-lam_ref[...])
    a = jnp.exp(-nla)
    a_ref[...] = a
    u_ref[...] = jnp.sqrt(jnp.tanh(nla) * (a * a + 1.0)) * (gi * xc)

    row = lax.broadcasted_iota(jnp.int32, (SUBLANES, a_ref.shape[1]), 0)

    def group(gidx, h_prev):
        r0 = pl.multiple_of(gidx * SUBLANES, SUBLANES)
        ag = a_ref[pl.ds(r0, SUBLANES), :]
        ug = u_ref[pl.ds(r0, SUBLANES), :]
        s = 1
        while s < SUBLANES:
            a_sh = jnp.where(row >= s, pltpu.roll(ag, s, axis=0), 1.0)
            u_sh = jnp.where(row >= s, pltpu.roll(ug, s, axis=0), 0.0)
            ug = ag * u_sh + ug
            ag = ag * a_sh
            s *= 2
        hg = ag * h_prev + ug
        u_ref[pl.ds(r0, SUBLANES), :] = hg
        return hg[SUBLANES - 1:SUBLANES, :]

    h_last = lax.fori_loop(0, tl // SUBLANES, group, h_ref[...])
    h_ref[...] = h_last
    hlast_ref[...] = h_last
    ya_ref[...] = (u_ref[...] * _gelu_tanh(ga_ref[...])).astype(BF16)


def _rglru(proj, conv_buf, h0, conv_w, conv_b, w_r, b_r, w_i, b_i, lam, d_rnn):
    bn, l, _ = proj.shape
    tl = min(256, l)
    n_blk = w_r.shape[0]
    vec = lambda v: v.reshape(1, d_rnn)
    const2 = lambda b, t: (0, 0)
    kern = functools.partial(_rglru_kernel, tl=tl, n_blk=n_blk)
    ya, h_last = pl.pallas_call(
        kern,
        grid=(bn, l // tl),
        in_specs=[
            pl.BlockSpec((None, tl, d_rnn), lambda b, t: (b, t, 0)),
            pl.BlockSpec((None, tl, d_rnn), lambda b, t: (b, t, 1)),
            pl.BlockSpec((None, CONV_W - 1, d_rnn), lambda b, t: (b, 0, 0)),
            pl.BlockSpec((None, 1, d_rnn), lambda b, t: (b, 0, 0)),
            pl.BlockSpec((CONV_W * SUBLANES, d_rnn), const2),
            pl.BlockSpec((SUBLANES, d_rnn), const2),
            pl.BlockSpec(w_r.shape, lambda b, t: (0, 0, 0)),
            pl.BlockSpec((1, d_rnn), const2),
            pl.BlockSpec(w_i.shape, lambda b, t: (0, 0, 0)),
            pl.BlockSpec((1, d_rnn), const2),
            pl.BlockSpec((1, d_rnn), const2),
        ],
        out_specs=[
            pl.BlockSpec((None, tl, d_rnn), lambda b, t: (b, t, 0)),
            pl.BlockSpec((None, 1, d_rnn), lambda b, t: (b, 0, 0)),
        ],
        out_shape=[
            jax.ShapeDtypeStruct((bn, l, d_rnn), BF16),
            jax.ShapeDtypeStruct((bn, 1, d_rnn), F32),
        ],
        scratch_shapes=[
            pltpu.VMEM((SUBLANES, d_rnn), F32),
            pltpu.VMEM((tl, d_rnn), F32),
            pltpu.VMEM((tl, d_rnn), F32),
            pltpu.VMEM((1, d_rnn), F32),
        ],
        compiler_params=_cparams("parallel", "arbitrary"),
        name="rglru",
    )(proj, proj, conv_buf, h0.reshape(bn, 1, d_rnn), *_conv_rep(conv_w, conv_b), w_r, vec(b_r), w_i, vec(b_i),
      vec(lam))
    return ya, h_last.reshape(bn, d_rnn)


def _chunk_mask(kpos, qpos):
    shift = CHUNK.bit_length() - 1
    return jnp.right_shift(kpos, shift) <= jnp.right_shift(qpos, shift)


def _diff_lambda(lq1_ref, lk1_ref, lq2_ref, lk2_ref, lam_init):
    return (jnp.exp(jnp.sum(lq1_ref[...] * lk1_ref[...], axis=-1, keepdims=True))
            - jnp.exp(jnp.sum(lq2_ref[...] * lk2_ref[...], axis=-1, keepdims=True)) + lam_init)


ONES_ROWS = 16


def _attn_prompt_kernel(q_ref, k_ref, v_ref, lq1_ref, lk1_ref, lq2_ref, lk2_ref, sw_ref, o_ref,
                        vt_ref, sa_ref, sb_ref, m_ref, acc_ref, *, t, nblk, dh, scale, lam_init):
    i = pl.program_id(2)
    dv = 2 * dh

    @pl.when(i == 0)
    def _():
        ones = jnp.ones((ONES_ROWS, t), BF16)
        for jb in range(nblk):
            vt_ref[jb, :dv, :] = jnp.transpose(v_ref[jb * t:(jb + 1) * t, :].astype(F32)).astype(BF16)
            vt_ref[jb, dv:, :] = ones

    qt = jnp.transpose(q_ref[...] * (scale * math.log2(math.e)))
    sub = lax.broadcasted_iota(jnp.int32, qt.shape, 0)
    qt2 = jnp.concatenate([jnp.where(sub < dh, qt, 0.0), jnp.where(sub >= dh, qt, 0.0)],
                          axis=1).astype(BF16)
    m_ref[...] = jnp.full_like(m_ref, NEG)
    acc_ref[...] = jnp.zeros_like(acc_ref)

    def scores(j, s_ref):
        s_ref[...] = _dot(k_ref[pl.ds(pl.multiple_of(j * t, t), t), :], qt2)

    def update(j, s_ref, diagonal):
        if diagonal:
            qoff = lax.broadcasted_iota(jnp.int32, (1, 2 * t), 1)
            qoff = jnp.where(qoff >= t, qoff - t, qoff)
            s = jnp.concatenate(
                [s_ref[c * CHUNK:(c + 1) * CHUNK, :] + jnp.where(qoff >= c * CHUNK, 0.0, NEG)
                 for c in range(t // CHUNK)], axis=0)
        else:
            s = s_ref[...]
        m_prev = m_ref[...]
        m_new = jnp.maximum(m_prev, jnp.max(s, axis=0, keepdims=True))
        alpha = jnp.exp2(m_prev - m_new)
        p = jnp.exp2(s - m_new).astype(BF16)
        acc_ref[...] = alpha * acc_ref[...] + _dot(vt_ref[j], p)
        m_ref[...] = m_new

    scores(0, sa_ref)

    def pair(pidx, carry):
        j = 2 * pidx
        scores(j + 1, sb_ref)
        update(j, sa_ref, False)
        scores(j + 2, sa_ref)
        update(j + 1, sb_ref, False)
        return carry

    lax.fori_loop(0, lax.shift_right_logical(i, 1), pair, 0)

    @pl.when((i & 1) == 1)
    def _():
        scores(i, sb_ref)
        update(i - 1, sa_ref, False)
        sa_ref[...] = sb_ref[...]

    update(i, sa_ref, True)

    lam = _diff_lambda(lq1_ref, lk1_ref, lq2_ref, lk2_ref, lam_init)
    acc = acc_ref[0:dv, :]
    l = acc_ref[dv:dv + 1, :]
    o_t = acc[:, :t] / l[:, :t] - lam * (acc[:, t:] / l[:, t:])
    o_ref[...] = (_rms(jnp.transpose(o_t), sw_ref[...]) * (1.0 - lam_init)).astype(BF16)


def _attn_cached_kernel(q_ref, kc_hbm, vc_hbm, kn_ref, vn_ref, lq1_ref, lk1_ref, lq2_ref, lk2_ref, sw_ref,
                        o_ref, qt_ref, m_ref, acc_ref, kbuf, vbuf, sem, *, n_heads, lq, n_cache, tkc, pos0,
                        dh, scale, lam_init):
    b = pl.program_id(0)
    j = pl.program_id(1)
    dv = 2 * dh
    hsl = lambda h: slice(h * LANES, (h + 1) * LANES)

    def cache_copies(bi, step, slot):
        rows = pl.ds(pl.multiple_of(step * tkc, tkc), tkc)
        return ([pltpu.make_async_copy(kc_hbm.at[bi, rows, h, :], kbuf.at[slot, h], sem.at[slot, 0])
                 for h in range(n_heads)]
                + [pltpu.make_async_copy(vc_hbm.at[bi, rows, h, :], vbuf.at[slot, h], sem.at[slot, 1])
                   for h in range(n_heads)])

    def fetch(bi, step, slot):
        for cp in cache_copies(bi, step, slot):
            cp.start()

    @pl.when(jnp.logical_and(b == 0, j == 0))
    def _():
        fetch(0, 0, 0)

    @pl.when(j == 0)
    def _():
        sub = lax.broadcasted_iota(jnp.int32, (dv, lq), 0)
        for h in range(n_heads):
            qt = jnp.transpose(q_ref[:, hsl(h)] * (scale * math.log2(math.e)))
            qt_ref[h] = jnp.concatenate([jnp.where(sub < dh, qt, 0.0), jnp.where(sub >= dh, qt, 0.0)],
                                        axis=1).astype(BF16)
        m_ref[...] = jnp.full_like(m_ref, NEG)
        acc_ref[...] = jnp.zeros_like(acc_ref)

    def update(k_of, v_of, tk, bias):
        ones = jnp.ones((ONES_ROWS, tk), BF16)
        for h in range(n_heads):
            s = _dot(k_of(h).astype(BF16), qt_ref[h])
            if bias is not None:
                s = s + bias
            m_prev = m_ref[h]
            m_new = jnp.maximum(m_prev, jnp.max(s, axis=0, keepdims=True))
            alpha = jnp.exp2(m_prev - m_new)
            p = jnp.exp2(s - m_new).astype(BF16)
            vt = jnp.transpose(v_of(h).astype(F32)).astype(BF16)
            acc_ref[h] = alpha * acc_ref[h] + _dot(jnp.concatenate([vt, ones], axis=0), p)
            m_ref[h] = m_new

    @pl.when(j < n_cache)
    def _():
        slot = j & 1

        @pl.when(j + 1 < n_cache)
        def _():
            fetch(b, j + 1, 1 - slot)

        for cp in cache_copies(b, j, slot):
            cp.wait()
        update(lambda h: kbuf[slot, h], lambda h: vbuf[slot, h], tkc, None)

    @pl.when(j == n_cache)
    def _():
        @pl.when(b + 1 < pl.num_programs(0))
        def _():
            fetch(b + 1, 0, 0)

        col = lax.broadcasted_iota(jnp.int32, (lq, 2 * lq), 1)
        qpos = pos0 + jnp.where(col >= lq, col - lq, col)
        kpos = pos0 + lax.broadcasted_iota(jnp.int32, (lq, 2 * lq), 0)
        update(lambda h: kn_ref[:, hsl(h)], lambda h: vn_ref[:, hsl(h)], lq,
               jnp.where(_chunk_mask(kpos, qpos), 0.0, NEG))
        lam = _diff_lambda(lq1_ref, lk1_ref, lq2_ref, lk2_ref, lam_init)
        outs = []
        for h in range(n_heads):
            acc = acc_ref[h, 0:dv, :]
            l = acc_ref[h, dv:dv + 1, :]
            o_t = acc[:, :lq] / l[:, :lq] - lam * (acc[:, lq:] / l[:, lq:])
            outs.append(_rms(jnp.transpose(o_t), sw_ref[...]) * (1.0 - lam_init))
        o_ref[...] = jnp.concatenate(outs, axis=1).astype(BF16)


def _diff_attention(qsrc, q_col, k, v, k_cache, v_cache, lq1, lk1, lq2, lk2, subln_w, n_heads, layer):
    bn, l = k.shape[:2]
    dv = LANES
    dh = dv // 2
    d_att = n_heads * dv
    lam_init = 0.8 - 0.6 * math.exp(-0.3 * layer)
    scale = dh ** -0.5
    params = [lq1.reshape(1, dh), lk1.reshape(1, dh), lq2.reshape(1, dh), lk2.reshape(1, dh),
              subln_w.reshape(1, dv)]
    if k_cache is None:
        t = min(512, l)
        nblk = l // t
        assert t % CHUNK == 0 and l % t == 0
        small = lambda n: pl.BlockSpec((1, n), lambda b, h, i: (0, 0))
        kern = functools.partial(_attn_prompt_kernel, t=t, nblk=nblk, dh=dh, scale=scale, lam_init=lam_init)
        return pl.pallas_call(
            kern,
            grid=(bn, n_heads, nblk),
            in_specs=[
                pl.BlockSpec((None, t, dv), lambda b, h, i: (b, i, q_col + h)),
                pl.BlockSpec((None, l, dv), lambda b, h, i: (b, 0, h)),
                pl.BlockSpec((None, l, dv), lambda b, h, i: (b, 0, h)),
                small(dh), small(dh), small(dh), small(dh), small(dv),
            ],
            out_specs=pl.BlockSpec((None, t, dv), lambda b, h, i: (b, i, h)),
            out_shape=jax.ShapeDtypeStruct((bn, l, d_att), BF16),
            scratch_shapes=[
                pltpu.VMEM((nblk, dv + ONES_ROWS, t), BF16),
                pltpu.VMEM((t, 2 * t), F32),
                pltpu.VMEM((t, 2 * t), F32),
                pltpu.VMEM((1, 2 * t), F32),
                pltpu.VMEM((dv + ONES_ROWS, 2 * t), F32),
            ],
            compiler_params=_cparams("parallel", "parallel", "arbitrary"),
            name="diff_attention_prompt",
        )(qsrc, k, v, *params)

    pos0 = k_cache.shape[1]
    tkc = min(512, pos0)
    assert pos0 % CHUNK == 0 and pos0 % tkc == 0 and l % SUBLANES == 0
    n_cache = pos0 // tkc
    small = lambda n: pl.BlockSpec((1, n), lambda b, j: (0, 0))
    kern = functools.partial(_attn_cached_kernel, n_heads=n_heads, lq=l, n_cache=n_cache, tkc=tkc, pos0=pos0,
                             dh=dh, scale=scale, lam_init=lam_init)
    cache_spec = pl.BlockSpec(memory_space=pl.ANY)
    new_spec = pl.BlockSpec((None, l, d_att), lambda b, j: (b, 0, 0))
    return pl.pallas_call(
        kern,
        grid=(bn, n_cache + 1),
        in_specs=[
            pl.BlockSpec((None, l, d_att), lambda b, j: (b, 0, q_col * dv // d_att)),
            cache_spec, cache_spec, new_spec, new_spec,
            small(dh), small(dh), small(dh), small(dh), small(dv),
        ],
        out_specs=pl.BlockSpec((None, l, d_att), lambda b, j: (b, 0, 0)),
        out_shape=jax.ShapeDtypeStruct((bn, l, d_att), BF16),
        scratch_shapes=[
            pltpu.VMEM((n_heads, dv, 2 * l), BF16),
            pltpu.VMEM((n_heads, 1, 2 * l), F32),
            pltpu.VMEM((n_heads, dv + ONES_ROWS, 2 * l), F32),
            pltpu.VMEM((2, n_heads, tkc, dv), F32),
            pltpu.VMEM((2, n_heads, tkc, dv), F32),
            pltpu.SemaphoreType.DMA((2, 2)),
        ],
        compiler_params=_cparams("arbitrary", "arbitrary"),
        name="diff_attention_cached",
    )(qsrc, k_cache, v_cache, k, v, *params)


def _ssd_kernel(z_ref, x_ref, bc_ref, dt_ref, cbx_ref, cbbc_ref, s0_ref, cwx_ref, cbiasx_ref, cwbc_ref,
                cbiasbc_ref, dtb_ref, alog_ref, dskip_ref, nw_ref, expand_ref,
                y_ref, s_ref, xsx_ref, xsbc_ref, *, n_grp, hpg, p_dim, n_dim):
    t = pl.program_id(1)
    q = CHUNK
    d_ssd = n_grp * hpg * p_dim
    gw = hpg * p_dim

    @pl.when(t == 0)
    def _():
        _conv_init(xsx_ref, cbx_ref[...])
        _conv_init(xsbc_ref, cbbc_ref[...])
        s_ref[...] = s0_ref[...]

    xs_all = _silu(_conv_tile(xsx_ref, x_ref[...], cwx_ref, cbiasx_ref))
    bcs_all = _silu(_conv_tile(xsbc_ref, bc_ref[...], cwbc_ref, cbiasbc_ref))

    a = -jnp.exp(alog_ref[...])
    expand = expand_ref[...]
    nw = nw_ref[...]
    row = lax.broadcasted_iota(jnp.int32, (q, q), 0)
    col = lax.broadcasted_iota(jnp.int32, (q, q), 1)
    causal = col <= row
    lane = lax.broadcasted_iota(jnp.int32, (q, LANES), 1)
    heads_per_tile = LANES // p_dim
    grp = lambda v, g: v[:, g * gw:(g + 1) * gw]

    def prepare(rows):
        xs, bcs = xs_all[rows], bcs_all[rows]
        dt = _softplus(dt_ref[rows, :] + dtb_ref[...])
        acum = _cumsum_rows(dt * a)
        a_last = acum[q - 1:q, :]
        wst = jnp.exp(a_last - acum) * dt
        eac = jnp.exp(acum)
        acum_t = jnp.transpose(acum)
        stacked = jnp.concatenate([dt, wst, eac], axis=0)
        hi = stacked.astype(BF16)
        rem = stacked - hi.astype(F32)
        mid = rem.astype(BF16)
        lo = (rem - mid.astype(F32)).astype(BF16)
        expd = _dot(hi, expand) + _dot(mid, expand) + _dot(lo, expand)
        dt_e, wst_e, eac_e = expd[0:q], expd[q:2 * q], expd[2 * q:3 * q]
        xdt = (xs * dt_e).astype(BF16)
        xw = (xs * wst_e).astype(BF16)
        bgs = [bcs[:, g * n_dim:(g + 1) * n_dim] for g in range(n_grp)]
        cgs = [bcs[:, (n_grp + g) * n_dim:(n_grp + g + 1) * n_dim].astype(BF16) for g in range(n_grp)]
        cbs = [lax.dot_general(cgs[g], bgs[g].astype(BF16), _NT, preferred_element_type=F32)
               for g in range(n_grp)]
        bg_ts = [jnp.transpose(bgs[g]).astype(BF16) for g in range(n_grp)]
        wdec = []
        for h in range(n_grp * hpg):
            dec = jnp.exp(jnp.where(causal, acum[:, h:h + 1] - acum_t[h:h + 1, :], -jnp.inf))
            wdec.append((cbs[h // hpg] * dec).astype(BF16))
        y_diag = []
        for tile in range(d_ssd // LANES):
            c0 = tile * LANES
            x_tile = xdt[:, c0:c0 + LANES]
            acc = None
            for hh in range(heads_per_tile):
                yh = _dot(wdec[c0 // p_dim + hh], x_tile)
                sel = jnp.logical_and(lane >= hh * p_dim, lane < (hh + 1) * p_dim)
                acc = jnp.where(sel, yh, 0.0) if acc is None else jnp.where(sel, yh, acc)
            y_diag.append(acc)
        y_diag = jnp.concatenate(y_diag, axis=1) + dskip_ref[...] * xs
        return y_diag, _silu(z_ref[rows, :]), cgs, bg_ts, xw, eac_e

    def advance(rows, prep):
        y_diag, gate, cgs, bg_ts, xw, eac_e = prep
        y_parts = []
        for g in range(n_grp):
            s_prev = s_ref[g]
            y_parts.append(_dot(cgs[g], s_prev.astype(BF16)) * grp(eac_e, g))
            s_ref[g] = grp(eac_e, g)[q - 1:q, :] * s_prev + _dot(bg_ts[g], grp(xw, g))
        y = (y_diag + jnp.concatenate(y_parts, axis=1)) * gate
        y_ref[rows, :] = jnp.concatenate([_rms(grp(y, g), grp(nw, g)) for g in range(n_grp)],
                                         axis=1).astype(BF16)

    chunks = [slice(r0, r0 + q) for r0 in range(0, x_ref.shape[0], q)]
    preps = [prepare(rows) for rows in chunks]
    for rows, prep in zip(chunks, preps):
        advance(rows, prep)


def _ssd(proj, cols, conv_buf, s0, conv_w, conv_b, dt_bias, a_log, d_skip, norm_w, n_grp, n_dim, p_dim):
    bn, l, _ = proj.shape
    n_heads = dt_bias.shape[0]
    hpg = n_heads // n_grp
    d_ssd = n_heads * p_dim
    d_bc = 2 * n_grp * n_dim
    gw = hpg * p_dim
    q = 4 * CHUNK if l % (4 * CHUNK) == 0 else CHUNK
    assert l % q == 0 and n_heads <= LANES and LANES % p_dim == 0
    z_col, x_col, bc_col, dt_col = cols
    pad = lambda v: jnp.pad(v.astype(F32), (0, LANES - n_heads)).reshape(1, LANES)
    expand = (jnp.arange(LANES)[:, None] == (jnp.arange(d_ssd)[None, :] // p_dim)).astype(BF16)
    s0_t = s0.reshape(bn, n_grp, hpg, p_dim, n_dim).transpose(0, 1, 4, 2, 3).reshape(bn, n_grp, n_dim, gw)
    const2 = lambda b, t: (0, 0)
    kern = functools.partial(_ssd_kernel, n_grp=n_grp, hpg=hpg, p_dim=p_dim, n_dim=n_dim)
    y, s_last = pl.pallas_call(
        kern,
        grid=(bn, l // q),
        in_specs=[
            pl.BlockSpec((None, q, d_ssd), lambda b, t: (b, t, z_col)),
            pl.BlockSpec((None, q, d_ssd), lambda b, t: (b, t, x_col)),
            pl.BlockSpec((None, q, d_bc), lambda b, t: (b, t, bc_col)),
            pl.BlockSpec((None, q, LANES), lambda b, t: (b, t, dt_col)),
            pl.BlockSpec((None, CONV_W - 1, d_ssd), lambda b, t: (b, 0, 0)),
            pl.BlockSpec((None, CONV_W - 1, d_bc), lambda b, t: (b, 0, 0)),
            pl.BlockSpec((None, n_grp, n_dim, gw), lambda b, t: (b, 0, 0, 0)),
            pl.BlockSpec((CONV_W * SUBLANES, d_ssd), const2),
            pl.BlockSpec((SUBLANES, d_ssd), const2),
            pl.BlockSpec((CONV_W * SUBLANES, d_bc), const2),
            pl.BlockSpec((SUBLANES, d_bc), const2),
            pl.BlockSpec((1, LANES), const2),
            pl.BlockSpec((1, LANES), const2),
            pl.BlockSpec((1, d_ssd), const2),
            pl.BlockSpec((1, d_ssd), const2),
            pl.BlockSpec((LANES, d_ssd), const2),
        ],
        out_specs=[
            pl.BlockSpec((None, q, d_ssd), lambda b, t: (b, t, 0)),
            pl.BlockSpec((None, n_grp, n_dim, gw), lambda b, t: (b, 0, 0, 0)),
        ],
        out_shape=[
            jax.ShapeDtypeStruct((bn, l, d_ssd), BF16),
            jax.ShapeDtypeStruct((bn, n_grp, n_dim, gw), F32),
        ],
        scratch_shapes=[
            pltpu.VMEM((SUBLANES, d_ssd), F32),
            pltpu.VMEM((SUBLANES, d_bc), F32),
        ],
        compiler_params=_cparams("parallel", "arbitrary"),
        name="ssd",
    )(proj, proj, proj, proj, conv_buf[:, :, :d_ssd], conv_buf[:, :, d_ssd:], s0_t,
      *_conv_rep(conv_w[:, :d_ssd], conv_b[:d_ssd]), *_conv_rep(conv_w[:, d_ssd:], conv_b[d_ssd:]),
      pad(dt_bias), pad(a_log), jnp.repeat(d_skip.astype(F32), p_dim).reshape(1, d_ssd),
      norm_w.reshape(1, d_ssd), expand)
    s_last = s_last.reshape(bn, n_grp, n_dim, hpg, p_dim).transpose(0, 1, 3, 4, 2).reshape(bn, n_heads, p_dim, n_dim)
    return y, s_last


def _hgrn_kernel(q_ref, f_ref, i_ref, lb_ref, nw_ref, s0_ref, y_ref, slast_ref, *, n_heads):
    t = pl.program_id(1)
    c = CHUNK
    nsb = c // HG_SUB

    @pl.when(t == 0)
    def _():
        slast_ref[...] = s0_ref[...]

    lbp = lb_ref[...]
    e = jnp.exp(lbp - jnp.max(lbp, axis=0, keepdims=True))
    lb = e[1:2, :] / jnp.sum(e, axis=0, keepdims=True)
    zero_blk = jnp.zeros((HG_SUB, LANES), F32)
    r2 = lax.broadcasted_iota(jnp.int32, (c, c), 0)
    c2 = lax.broadcasted_iota(jnp.int32, (c, c), 1)
    causal = c2 <= r2
    nw = nw_ref[...]

    def prepare(rows):
        g = lb + (1.0 - lb) * _sigmoid(f_ref[rows, :])
        logg = jnp.log(g)
        kk = 1.0 - g
        qq = _silu(q_ref[rows, :])
        b = _cumsum_rows(logg)
        iv = i_ref[rows, :]
        b_last = b[c - 1:c, :]
        qe = (qq * jnp.exp(b)).astype(BF16)
        kd = (kk * jnp.exp(b_last - b)).astype(BF16)
        s_decay = jnp.exp(b_last)
        ivb = iv.astype(BF16)
        sb = lambda x, r: x[r * HG_SUB:(r + 1) * HG_SUB, :]
        m_off = [b[a * HG_SUB - 1:a * HG_SUB, :] for a in range(1, nsb)]
        m_mid = [b[r * HG_SUB + HG_SUB // 2 - 1:r * HG_SUB + HG_SUB // 2, :] for r in range(nsb)]
        q_off = [None] + [sb(qq, a) * jnp.exp(sb(b, a) - m_off[a - 1]) for a in range(1, nsb)]
        q_mid = [sb(qq, r) * jnp.exp(sb(b, r) - m_mid[r]) for r in range(nsb)]
        k_mid = [sb(kk, r) * jnp.exp(m_mid[r] - sb(b, r)) for r in range(nsb)]
        k_off = [[sb(kk, r) * jnp.exp(m_off[a - 1] - sb(b, r)) if a > r else None for a in range(1, nsb)]
                 for r in range(nsb)]
        atts = []
        for h in range(n_heads):
            sl = slice(h * LANES, (h + 1) * LANES)
            q_rows, k_rows = [], []
            for r in range(nsb):
                q_parts = [q_off[a][:, sl] if a == r else zero_blk for a in range(1, nsb)]
                k_parts = [k_off[r][a - 1][:, sl] if a > r else zero_blk for a in range(1, nsb)]
                q_parts += [q_mid[r][:, sl] if d == r else zero_blk for d in range(nsb)]
                k_parts += [k_mid[r][:, sl] if d == r else zero_blk for d in range(nsb)]
                q_rows.append(jnp.concatenate(q_parts, axis=1))
                k_rows.append(jnp.concatenate(k_parts, axis=1))
            qcat = jnp.concatenate(q_rows, axis=0).astype(BF16)
            kcat = jnp.concatenate(k_rows, axis=0).astype(BF16)
            att = lax.dot_general(qcat, kcat, _NT, preferred_element_type=F32)
            atts.append(jnp.where(causal, att, 0.0).astype(BF16))
        return atts, ivb, iv, qe, kd, s_decay

    def advance(rows, prep):
        atts, ivb, iv, qe, kd, s_decay = prep
        outs = []
        for h in range(n_heads):
            sl = slice(h * LANES, (h + 1) * LANES)
            s_t = slast_ref[h]
            o = _dot(atts[h], ivb[:, sl])
            o = o + lax.dot_general(qe[:, sl], s_t.astype(BF16), _NT, preferred_element_type=F32)
            slast_ref[h] = s_decay[:, sl] * s_t + _dot(jnp.transpose(iv[:, sl]).astype(BF16), kd[:, sl])
            outs.append(_rms(o, nw))
        y_ref[rows, :] = jnp.concatenate(outs, axis=1).astype(BF16)

    chunks = [slice(r0, r0 + c) for r0 in range(0, q_ref.shape[0], c)]
    preps = [prepare(rows) for rows in chunks]
    for rows, prep in zip(chunks, preps):
        advance(rows, prep)


def _hgrn(proj, cols, s0, lb_param, norm_w, n_heads):
    bn, l, _ = proj.shape
    d = n_heads * LANES
    c = 4 * CHUNK if l % (4 * CHUNK) == 0 else CHUNK
    assert l % c == 0
    q_col, f_col, i_col = cols
    depth = lb_param.shape[0]
    assert depth == 2
    s0_t = jnp.swapaxes(s0, -1, -2)
    kern = functools.partial(_hgrn_kernel, n_heads=n_heads)
    y, s_last = pl.pallas_call(
        kern,
        grid=(bn, l // c),
        in_specs=[
            pl.BlockSpec((None, c, d), lambda b, t: (b, t, q_col)),
            pl.BlockSpec((None, c, d), lambda b, t: (b, t, f_col)),
            pl.BlockSpec((None, c, d), lambda b, t: (b, t, i_col)),
            pl.BlockSpec((depth, d), lambda b, t: (0, 0)),
            pl.BlockSpec((1, LANES), lambda b, t: (0, 0)),
            pl.BlockSpec((None, n_heads, LANES, LANES), lambda b, t: (b, 0, 0, 0)),
        ],
        out_specs=[
            pl.BlockSpec((None, c, d), lambda b, t: (b, t, 0)),
            pl.BlockSpec((None, n_heads, LANES, LANES), lambda b, t: (b, 0, 0, 0)),
        ],
        out_shape=[
            jax.ShapeDtypeStruct((bn, l, d), BF16),
            jax.ShapeDtypeStruct((bn, n_heads, LANES, LANES), F32),
        ],
        compiler_params=_cparams("parallel", "arbitrary"),
        name="hgrn2",
    )(proj, proj, proj, lb_param, norm_w.reshape(1, LANES), s0_t)
    return y, jnp.swapaxes(s_last, -1, -2)


def kernel(x_prompt, x_sample, cache_diff_k, cache_diff_v, state_rglru_conv, state_rglru_h, state_ssd_conv,
           state_ssd, state_hgrn, norm_w, l0_w_in, l0_conv_w, l0_conv_b, l0_rg_w_r, l0_rg_b_r, l0_rg_w_i,
           l0_rg_b_i, l0_rg_lambda, l0_lq1, l0_lk1, l0_lq2, l0_lk2, l0_subln_w, l0_w_out, l1_w_in, l1_conv_w,
           l1_conv_b, l1_dt_bias, l1_a_log, l1_d_skip, l1_ssd_norm_w, l1_hg_lower_bound, l1_hg_norm_w, l1_w_out,
           ffn_w_gate, ffn_w_up, ffn_w_down):
    d_model = x_prompt.shape[-1]
    d_rnn = l0_conv_w.shape[1]
    n_heads_b, dv_b = cache_diff_k.shape[2], cache_diff_k.shape[3]
    d_att = n_heads_b * dv_b
    n_heads_c, p_c, n_c = state_ssd.shape[1:]
    d_ssd = n_heads_c * p_c
    d_xbc = l1_conv_w.shape[1]
    d_bc = d_xbc - d_ssd
    n_grp = d_bc // (2 * n_c)
    n_heads_d, dk_d, dv_d = state_hgrn.shape[1:]
    d_hg = n_heads_d * dk_d
    assert dv_b == LANES and dk_d == LANES and dv_d == LANES and d_rnn == d_att == d_ssd == d_hg

    w_in0_main = l0_w_in[:, :2 * d_rnn + d_att].astype(BF16)
    w_in0_kv = l0_w_in[:, 2 * d_rnn + d_att:].astype(BF16)
    o1 = d_ssd
    o2 = o1 + d_xbc
    o3 = o2 + n_heads_c
    w1 = l1_w_in
    w_in1 = jnp.concatenate(
        [w1[:, :o1], w1[:, o1:o1 + d_ssd], w1[:, o3:], w1[:, o1 + d_ssd:o2],
         jnp.pad(w1[:, o2:o3], ((0, 0), (0, LANES - n_heads_c)))], axis=1).astype(BF16)
    w_out0 = l0_w_out.astype(BF16)
    w_out1 = l1_w_out.astype(BF16)
    wg = ffn_w_gate.astype(BF16)
    wu = ffn_w_up.astype(BF16)
    wd = ffn_w_down.astype(BF16)
    w_r = l0_rg_w_r.astype(BF16)
    w_i = l0_rg_w_i.astype(BF16)
    tn0 = 1024
    tn1 = 1152
    assert w_in0_main.shape[1] % tn0 == 0 and w_in1.shape[1] % tn1 == 0
    ssd_cols = (0, 1, (2 * d_ssd + 3 * d_hg) // d_bc, (2 * d_ssd + 3 * d_hg + d_bc) // LANES)
    hg_cols = (2, 3, 4)

    def trunk(x, k_cache, v_cache, rg_conv, rg_h, ssd_conv, ssd_s, hg_s):
        bn, l, _ = x.shape
        m = bn * l
        x0 = x.reshape(m, d_model)
        proj0, xn0 = _norm_matmul(x0, norm_w[0, 0], w_in0_main, tn0)
        proj0 = proj0.reshape(bn, l, -1)
        k_new, v_new, k_bf, v_bf = _head_proj(xn0, w_in0_kv, n_heads_b)
        k_new = k_new.reshape(bn, l, n_heads_b, dv_b)
        v_new = v_new.reshape(bn, l, n_heads_b, dv_b)
        ya, rg_h_new = _rglru(proj0, rg_conv, rg_h, l0_conv_w, l0_conv_b, w_r, l0_rg_b_r, w_i, l0_rg_b_i,
                              l0_rg_lambda, d_rnn)
        yb = _diff_attention(proj0, 2 * d_rnn // LANES, k_bf.reshape(bn, l, d_att), v_bf.reshape(bn, l, d_att),
                             k_cache, v_cache, l0_lq1, l0_lk1, l0_lq2, l0_lk2, l0_subln_w, n_heads_b, 0)
        rg_conv_new = proj0[:, l - (CONV_W - 1):, :d_rnn]
        x1, hn1 = _out_proj(ya.reshape(m, d_rnn), yb.reshape(m, d_att), w_out0, x0, norm_w[0, 1], norm_w[0, 2])
        x2, xn2 = _ffn(x1, hn1, norm_w[0, 3], norm_w[1, 0], wg, wu, wd, 0)
        proj1 = _matmul(xn2, w_in1, tn1).reshape(bn, l, -1)
        ys, ssd_s_new = _ssd(proj1, ssd_cols, ssd_conv, ssd_s, l1_conv_w, l1_conv_b, l1_dt_bias, l1_a_log,
                             l1_d_skip, l1_ssd_norm_w, n_grp, n_c, p_c)
        yh, hg_s_new = _hgrn(proj1, hg_cols, hg_s, l1_hg_lower_bound, l1_hg_norm_w, n_heads_d)
        tail = proj1[:, l - (CONV_W - 1):, :]
        ssd_conv_new = jnp.concatenate(
            [tail[:, :, d_ssd:2 * d_ssd], tail[:, :, 2 * d_ssd + 3 * d_hg:2 * d_ssd + 3 * d_hg + d_bc]], axis=-1)
        x3, hn3 = _out_proj(ys.reshape(m, d_ssd), yh.reshape(m, d_hg), w_out1, x2, norm_w[1, 1], norm_w[1, 2])
        x4 = _ffn(x3, hn3, norm_w[1, 3], None, wg, wu, wd, 1)[0]
        return (x4.reshape(bn, l, d_model), k_new, v_new, rg_conv_new, rg_h_new, ssd_conv_new, ssd_s_new, hg_s_new)

    bp = x_prompt.shape[0]
    zeros = lambda *s: jnp.zeros(s, F32)
    outs_p = trunk(x_prompt, None, None, zeros(bp, CONV_W - 1, d_rnn), zeros(bp, d_rnn),
                   zeros(bp, CONV_W - 1, d_xbc), zeros(bp, n_heads_c, p_c, n_c), zeros(bp, n_heads_d, dk_d, dv_d))
    outs_s = trunk(x_sample, cache_diff_k, cache_diff_v, state_rglru_conv, state_rglru_h, state_ssd_conv,
                   state_ssd, state_hgrn)
    return (outs_p[0], outs_s[0]) + tuple(outs_p[1:]) + tuple(outs_s[1:])
```

```python
import functools
import math

import jax
import jax.numpy as jnp
from jax import lax
from jax.experimental import pallas as pl
from jax.experimental.pallas import tpu as pltpu

F32 = jnp.float32
BF16 = jnp.bfloat16

EPS = 1e-6
CHUNK = 64
CONV_W = 4
RG_C = 8.0
LANES = 128
SUBLANES = 8
HG_SUB = 16
NEG = -1e30
VMEM_LIMIT_BYTES = 56 * 1024 * 1024

_NT = (((1,), (1,)), ((), ()))
_TN = (((0,), (0,)), ((), ()))


def _cparams(*sem):
    return pltpu.CompilerParams(dimension_semantics=sem, vmem_limit_bytes=VMEM_LIMIT_BYTES)


def _sigmoid(x):
    return 0.5 * jnp.tanh(0.5 * x) + 0.5


def _silu(x):
    h = 0.5 * x
    return h * jnp.tanh(h) + h


def _softplus(x):
    return jnp.maximum(x, 0.0) + jnp.log1p(jnp.exp(-jnp.abs(x)))


def _gelu_tanh(x):
    c = math.sqrt(2.0 / math.pi)
    return x * (0.5 * (1.0 + jnp.tanh(c * (x + 0.044715 * (x * x * x)))))


def _rms(x, w):
    return x * lax.rsqrt(jnp.mean(x * x, axis=-1, keepdims=True) + EPS) * w


def _dot(a, b):
    return jnp.dot(a, b, preferred_element_type=F32)


def _cumsum_rows(x):
    n = x.shape[0]
    row = lax.broadcasted_iota(jnp.int32, x.shape, 0)
    s = 1
    while s < n:
        x = x + jnp.where(row >= s, pltpu.roll(x, s, axis=0), 0.0)
        s *= 2
    return x


def _norm_matmul_kernel(x_ref, nw_ref, w_ref, o_ref, xn_ref):
    @pl.when(pl.program_id(1) == 0)
    def _():
        xn_ref[...] = _rms(x_ref[...], nw_ref[...]).astype(BF16)

    o_ref[...] = _dot(xn_ref[...], w_ref[...])


def _norm_matmul(x, nw, w, tn):
    m, d = x.shape
    n = w.shape[1]
    assert n % tn == 0
    tm = min(1024, m)
    return pl.pallas_call(
        _norm_matmul_kernel,
        grid=(m // tm, n // tn),
        in_specs=[
            pl.BlockSpec((tm, d), lambda i, j: (i, 0)),
            pl.BlockSpec((1, d), lambda i, j: (0, 0)),
            pl.BlockSpec((d, tn), lambda i, j: (0, j)),
        ],
        out_specs=[pl.BlockSpec((tm, tn), lambda i, j: (i, j)), pl.BlockSpec((tm, d), lambda i, j: (i, 0))],
        out_shape=[jax.ShapeDtypeStruct((m, n), F32), jax.ShapeDtypeStruct((m, d), BF16)],
        compiler_params=_cparams("parallel", "arbitrary"),
        name="norm_in_proj",
    )(x, nw.reshape(1, d), w)


def _head_proj_kernel(xn_ref, w_ref, *o_refs):
    n_out = len(o_refs) // 2
    j = pl.program_id(1)
    for k in range(n_out):
        @pl.when(j == k)
        def _(k=k):
            o_ref, ob_ref = o_refs[k], o_refs[n_out + k]
            res = _dot(xn_ref[...], w_ref[...])
            for h in range(o_ref.shape[1]):
                o_ref[:, h, :] = res[:, h * LANES:(h + 1) * LANES]
            ob_ref[...] = res.astype(BF16)


def _head_proj(xn, w, n_heads):
    m, d = xn.shape
    width = n_heads * LANES
    n_out = w.shape[1] // width
    tm = min(1024, m)
    return pl.pallas_call(
        _head_proj_kernel,
        grid=(m // tm, n_out),
        in_specs=[
            pl.BlockSpec((tm, d), lambda i, j: (i, 0)),
            pl.BlockSpec((d, width), lambda i, j: (0, j)),
        ],
        out_specs=([pl.BlockSpec((tm, n_heads, LANES), lambda i, j: (i, 0, 0)) for _ in range(n_out)]
                   + [pl.BlockSpec((tm, width), lambda i, j: (i, 0)) for _ in range(n_out)]),
        out_shape=([jax.ShapeDtypeStruct((m, n_heads, LANES), F32) for _ in range(n_out)]
                   + [jax.ShapeDtypeStruct((m, width), BF16) for _ in range(n_out)]),
        compiler_params=_cparams("parallel", "arbitrary"),
        name="head_proj",
    )(xn, w)


def _out_proj_kernel(ya_ref, yb_ref, wa_ref, wb_ref, x_ref, nw_ref, nwn_ref, o_ref, hn_ref):
    tm = x_ref.shape[0]
    grp = max(tm // 4, LANES)
    for r0 in range(0, tm, grp):
        rows = slice(r0, r0 + grp)
        m = _dot(ya_ref[rows, :], wa_ref[...]) + _dot(yb_ref[rows, :], wb_ref[...])
        o = x_ref[rows, :] + _rms(m, nw_ref[...])
        o_ref[rows, :] = o
        hn_ref[rows, :] = _rms(o, nwn_ref[...]).astype(BF16)


def _out_proj(ya, yb, w_out, x, nw, nw_next):
    m, d = x.shape
    da, db = ya.shape[1], yb.shape[1]
    assert da == db and w_out.shape[0] == da + db
    tm = min(512, m)
    row_spec = pl.BlockSpec((tm, d), lambda i: (i, 0))
    vec_spec = pl.BlockSpec((1, d), lambda i: (0, 0))
    return pl.pallas_call(
        _out_proj_kernel,
        grid=(m // tm,),
        in_specs=[
            pl.BlockSpec((tm, da), lambda i: (i, 0)),
            pl.BlockSpec((tm, db), lambda i: (i, 0)),
            pl.BlockSpec((da, d), lambda i: (0, 0)),
            pl.BlockSpec((db, d), lambda i: (1, 0)),
            row_spec, vec_spec, vec_spec,
        ],
        out_specs=[row_spec, row_spec],
        out_shape=[jax.ShapeDtypeStruct((m, d), F32), jax.ShapeDtypeStruct((m, d), BF16)],
        compiler_params=_cparams("parallel"),
        name="out_proj",
    )(ya, yb, w_out, w_out, x, nw.reshape(1, d), nw_next.reshape(1, d))


def _ffn_kernel(x_ref, hn_ref, nwb_ref, nwn_ref, wg_ref, wu_ref, wd_ref, o_ref, *rest, emit_next):
    acc_ref = rest[-1]
    f = pl.program_id(1)

    @pl.when(f == 0)
    def _():
        acc_ref[...] = jnp.zeros_like(acc_ref)

    hn = hn_ref[...]
    g = _dot(hn, wg_ref[...])
    u = _dot(hn, wu_ref[...])
    a = (_silu(g) * u).astype(BF16)
    acc_ref[...] += _dot(a, wd_ref[...])

    @pl.when(f == pl.num_programs(1) - 1)
    def _():
        o = x_ref[...] + _rms(acc_ref[...], nwb_ref[...])
        o_ref[...] = o
        if emit_next:
            rest[0][...] = _rms(o, nwn_ref[...]).astype(BF16)


def _ffn(x, hn, nw_post, nw_next, wg, wu, wd, layer):
    emit_next = nw_next is not None
    nw_next = nw_post if nw_next is None else nw_next
    m, d = x.shape
    dff = wg.shape[2]
    tm = min(512, m)
    tf = 512
    row_spec = pl.BlockSpec((tm, d), lambda i, f: (i, 0))
    vec_spec = pl.BlockSpec((1, d), lambda i, f: (0, 0))
    out_specs = [row_spec]
    out_shape = [jax.ShapeDtypeStruct((m, d), F32)]
    if emit_next:
        out_specs.append(row_spec)
        out_shape.append(jax.ShapeDtypeStruct((m, d), BF16))
    return pl.pallas_call(
        functools.partial(_ffn_kernel, emit_next=emit_next),
        grid=(m // tm, dff // tf),
        in_specs=[
            row_spec, row_spec, vec_spec, vec_spec,
            pl.BlockSpec((None, d, tf), lambda i, f: (layer, 0, f)),
            pl.BlockSpec((None, d, tf), lambda i, f: (layer, 0, f)),
            pl.BlockSpec((None, tf, d), lambda i, f: (layer, f, 0)),
        ],
        out_specs=out_specs,
        out_shape=out_shape,
        scratch_shapes=[pltpu.VMEM((tm, d), F32)],
        compiler_params=_cparams("parallel", "arbitrary"),
        name="ffn",
    )(x, hn, nw_post.reshape(1, d), nw_next.reshape(1, d), wg, wu, wd)


def _matmul_kernel(x_ref, w_ref, o_ref):
    o_ref[...] = _dot(x_ref[...], w_ref[...])


def _matmul(x, w, tn):
    m, d = x.shape
    n = w.shape[1]
    assert n % tn == 0
    tm = min(1024, m)
    return pl.pallas_call(
        _matmul_kernel,
        grid=(m // tm, n // tn),
        in_specs=[pl.BlockSpec((tm, d), lambda i, j: (i, 0)), pl.BlockSpec((d, tn), lambda i, j: (0, j))],
        out_specs=pl.BlockSpec((tm, tn), lambda i, j: (i, j)),
        out_shape=jax.ShapeDtypeStruct((m, n), F32),
        compiler_params=_cparams("parallel", "parallel"),
        name="in_proj",
    )(x, w)


def _conv_init(prev_ref, conv_buf):
    prev_ref[...] = jnp.zeros_like(prev_ref)
    prev_ref[SUBLANES - (CONV_W - 1):SUBLANES, :] = conv_buf


def _conv_rep(conv_w, conv_b):
    c = conv_w.shape[1]
    return jnp.repeat(conv_w, SUBLANES, axis=0), jnp.broadcast_to(conv_b.reshape(1, c), (SUBLANES, c))


def _conv_tile(prev_ref, x, cw_ref, cb_ref):
    tl, c = x.shape
    prev = prev_ref[...]
    row = lax.broadcasted_iota(jnp.int32, prev.shape, 0)
    tap = lambda i: cw_ref[i * SUBLANES:(i + 1) * SUBLANES, :][None]
    tiles = lambda v: v.reshape(tl // SUBLANES, SUBLANES, c)
    y = cb_ref[...][None] + tiles(x) * tap(CONV_W - 1)
    for k in range(1, CONV_W):
        xr = pltpu.roll(x, k, axis=0)
        head = jnp.where(row < k, pltpu.roll(prev, k, axis=0), xr[0:SUBLANES])
        xk = head if tl == SUBLANES else jnp.concatenate([head, xr[SUBLANES:]], axis=0)
        y = y + tiles(xk) * tap(CONV_W - 1 - k)
    prev_ref[...] = x[tl - SUBLANES:tl]
    return y.reshape(tl, c)


def _rglru_kernel(xa_ref, ga_ref, cbuf_ref, h0_ref, cw_ref, cb_ref, wr_ref, br_ref, wi_ref, bi_ref,
                  lam_ref, ya_ref, hlast_ref, xs_ref, a_ref, u_ref, h_ref, *, tl, n_blk):
    t = pl.program_id(1)

    @pl.when(t == 0)
    def _():
        _conv_init(xs_ref, cbuf_ref[...])
        h_ref[...] = h0_ref[...]

    xc = _conv_tile(xs_ref, xa_ref[...], cw_ref, cb_ref)
    xcb = xc.astype(BF16)
    r_pre = jnp.concatenate(
        [_dot(xcb[:, hb * LANES:(hb + 1) * LANES], wr_ref[hb]) for hb in range(n_blk)], axis=1)
    i_pre = jnp.concatenate(
        [_dot(xcb[:, hb * LANES:(hb + 1) * LANES], wi_ref[hb]) for hb in range(n_blk)], axis=1)
    r = _sigmoid(r_pre + br_ref[...])
    gi = _sigmoid(i_pre + bi_ref[...])
    nla = RG_C * r * _softplus(-lam_ref[...])
    a = jnp.exp(-nla)
    a_ref[...] = a
    u_ref[...] = jnp.sqrt(jnp.tanh(nla) * (a * a + 1.0)) * (gi * xc)

    row = lax.broadcasted_iota(jnp.int32, (SUBLANES, a_ref.shape[1]), 0)

    def group(gidx, h_prev):
        r0 = pl.multiple_of(gidx * SUBLANES, SUBLANES)
        ag = a_ref[pl.ds(r0, SUBLANES), :]
        ug = u_ref[pl.ds(r0, SUBLANES), :]
        s = 1
        while s < SUBLANES:
            a_sh = jnp.where(row >= s, pltpu.roll(ag, s, axis=0), 1.0)
            u_sh = jnp.where(row >= s, pltpu.roll(ug, s, axis=0), 0.0)
            ug = ag * u_sh + ug
            ag = ag * a_sh
            s *= 2
        hg = ag * h_prev + ug
        u_ref[pl.ds(r0, SUBLANES), :] = hg
        return hg[SUBLANES - 1:SUBLANES, :]

    h_last = lax.fori_loop(0, tl // SUBLANES, group, h_ref[...])
    h_ref[...] = h_last
    hlast_ref[...] = h_last
    ya_ref[...] = (u_ref[...] * _gelu_tanh(ga_ref[...])).astype(BF16)


def _rglru(proj, conv_buf, h0, conv_w, conv_b, w_r, b_r, w_i, b_i, lam, d_rnn):
    bn, l, _ = proj.shape
    tl = min(256, l)
    n_blk = w_r.shape[0]
    vec = lambda v: v.reshape(1, d_rnn)
    const2 = lambda b, t: (0, 0)
    kern = functools.partial(_rglru_kernel, tl=tl, n_blk=n_blk)
    ya, h_last = pl.pallas_call(
        kern,
        grid=(bn, l // tl),
        in_specs=[
            pl.BlockSpec((None, tl, d_rnn), lambda b, t: (b, t, 0)),
            pl.BlockSpec((None, tl, d_rnn), lambda b, t: (b, t, 1)),
            pl.BlockSpec((None, CONV_W - 1, d_rnn), lambda b, t: (b, 0, 0)),
            pl.BlockSpec((None, 1, d_rnn), lambda b, t: (b, 0, 0)),
            pl.BlockSpec((CONV_W * SUBLANES, d_rnn), const2),
            pl.BlockSpec((SUBLANES, d_rnn), const2),
            pl.BlockSpec(w_r.shape, lambda b, t: (0, 0, 0)),
            pl.BlockSpec((1, d_rnn), const2),
            pl.BlockSpec(w_i.shape, lambda b, t: (0, 0, 0)),
            pl.BlockSpec((1, d_rnn), const2),
            pl.BlockSpec((1, d_rnn), const2),
        ],
        out_specs=[
            pl.BlockSpec((None, tl, d_rnn), lambda b, t: (b, t, 0)),
            pl.BlockSpec((None, 1, d_rnn), lambda b, t: (b, 0, 0)),
        ],
        out_shape=[
            jax.ShapeDtypeStruct((bn, l, d_rnn), BF16),
            jax.ShapeDtypeStruct((bn, 1, d_rnn), F32),
        ],
        scratch_shapes=[
            pltpu.VMEM((SUBLANES, d_rnn), F32),
            pltpu.VMEM((tl, d_rnn), F32),
            pltpu.VMEM((tl, d_rnn), F32),
            pltpu.VMEM((1, d_rnn), F32),
        ],
        compiler_params=_cparams("parallel", "arbitrary"),
        name="rglru",
    )(proj, proj, conv_buf, h0.reshape(bn, 1, d_rnn), *_conv_rep(conv_w, conv_b), w_r, vec(b_r), w_i, vec(b_i),
      vec(lam))
    return ya, h_last.reshape(bn, d_rnn)


def _chunk_mask(kpos, qpos):
    shift = CHUNK.bit_length() - 1
    return jnp.right_shift(kpos, shift) <= jnp.right_shift(qpos, shift)


def _diff_lambda(lq1_ref, lk1_ref, lq2_ref, lk2_ref, lam_init):
    return (jnp.exp(jnp.sum(lq1_ref[...] * lk1_ref[...], axis=-1, keepdims=True))
            - jnp.exp(jnp.sum(lq2_ref[...] * lk2_ref[...], axis=-1, keepdims=True)) + lam_init)


ONES_ROWS = 16


def _attn_prompt_kernel(q_ref, k_ref, v_ref, lq1_ref, lk1_ref, lq2_ref, lk2_ref, sw_ref, o_ref,
                        vt_ref, sa_ref, sb_ref, m_ref, acc_ref, *, t, nblk, dh, scale, lam_init):
    i = pl.program_id(2)
    dv = 2 * dh

    @pl.when(i == 0)
    def _():
        ones = jnp.ones((ONES_ROWS, t), BF16)
        for jb in range(nblk):
            vt_ref[jb, :dv, :] = jnp.transpose(v_ref[jb * t:(jb + 1) * t, :].astype(F32)).astype(BF16)
            vt_ref[jb, dv:, :] = ones

    qt = jnp.transpose(q_ref[...] * (scale * math.log2(math.e)))
    sub = lax.broadcasted_iota(jnp.int32, qt.shape, 0)
    qt2 = jnp.concatenate([jnp.where(sub < dh, qt, 0.0), jnp.where(sub >= dh, qt, 0.0)],
                          axis=1).astype(BF16)
    m_ref[...] = jnp.full_like(m_ref, NEG)
    acc_ref[...] = jnp.zeros_like(acc_ref)

    def scores(j, s_ref):
        s_ref[...] = _dot(k_ref[pl.ds(pl.multiple_of(j * t, t), t), :], qt2)

    def update(j, s_ref, diagonal):
        if diagonal:
            qoff = lax.broadcasted_iota(jnp.int32, (1, 2 * t), 1)
            qoff = jnp.where(qoff >= t, qoff - t, qoff)
            s = jnp.concatenate(
                [s_ref[c * CHUNK:(c + 1) * CHUNK, :] + jnp.where(qoff >= c * CHUNK, 0.0, NEG)
                 for c in range(t // CHUNK)], axis=0)
        else:
            s = s_ref[...]
        m_prev = m_ref[...]
        m_new = jnp.maximum(m_prev, jnp.max(s, axis=0, keepdims=True))
        alpha = jnp.exp2(m_prev - m_new)
        p = jnp.exp2(s - m_new).astype(BF16)
        acc_ref[...] = alpha * acc_ref[...] + _dot(vt_ref[j], p)
        m_ref[...] = m_new

    scores(0, sa_ref)

    def pair(pidx, carry):
        j = 2 * pidx
        scores(j + 1, sb_ref)
        update(j, sa_ref, False)
        scores(j + 2, sa_ref)
        update(j + 1, sb_ref, False)
        return carry

    lax.fori_loop(0, lax.shift_right_logical(i, 1), pair, 0)

    @pl.when((i & 1) == 1)
    def _():
        scores(i, sb_ref)
        update(i - 1, sa_ref, False)
        sa_ref[...] = sb_ref[...]

    update(i, sa_ref, True)

    lam = _diff_lambda(lq1_ref, lk1_ref, lq2_ref, lk2_ref, lam_init)
    acc = acc_ref[0:dv, :]
    l = acc_ref[dv:dv + 1, :]
    o_t = acc[:, :t] / l[:, :t] - lam * (acc[:, t:] / l[:, t:])
    o_ref[...] = (_rms(jnp.transpose(o_t), sw_ref[...]) * (1.0 - lam_init)).astype(BF16)


def _attn_cached_kernel(q_ref, kc_hbm, vc_hbm, kn_ref, vn_ref, lq1_ref, lk1_ref, lq2_ref, lk2_ref, sw_ref,
                        o_ref, qt_ref, m_ref, acc_ref, kbuf, vbuf, sem, *, n_heads, lq, n_cache, tkc, pos0,
                        dh, scale, lam_init):
    b = pl.program_id(0)
    j = pl.program_id(1)
    dv = 2 * dh
    hsl = lambda h: slice(h * LANES, (h + 1) * LANES)

    def cache_copies(bi, step, slot):
        rows = pl.ds(pl.multiple_of(step * tkc, tkc), tkc)
        return ([pltpu.make_async_copy(kc_hbm.at[bi, rows, h, :], kbuf.at[slot, h], sem.at[slot, 0])
                 for h in range(n_heads)]
                + [pltpu.make_async_copy(vc_hbm.at[bi, rows, h, :], vbuf.at[slot, h], sem.at[slot, 1])
                   for h in range(n_heads)])

    def fetch(bi, step, slot):
        for cp in cache_copies(bi, step, slot):
            cp.start()

    @pl.when(jnp.logical_and(b == 0, j == 0))
    def _():
        fetch(0, 0, 0)

    @pl.when(j == 0)
    def _():
        sub = lax.broadcasted_iota(jnp.int32, (dv, lq), 0)
        for h in range(n_heads):
            qt = jnp.transpose(q_ref[:, hsl(h)] * (scale * math.log2(math.e)))
            qt_ref[h] = jnp.concatenate([jnp.where(sub < dh, qt, 0.0), jnp.where(sub >= dh, qt, 0.0)],
                                        axis=1).astype(BF16)
        m_ref[...] = jnp.full_like(m_ref, NEG)
        acc_ref[...] = jnp.zeros_like(acc_ref)

    def update(k_of, v_of, tk, bias):
        ones = jnp.ones((ONES_ROWS, tk), BF16)
        scores = [_dot(k_of(h).astype(BF16), qt_ref[h]) for h in range(n_heads)]
        vts = [jnp.concatenate([jnp.transpose(v_of(h).astype(F32)).astype(BF16), ones], axis=0)
               for h in range(n_heads)]
        for h in range(n_heads):
            s = scores[h] if bias is None else scores[h] + bias
            m_prev = m_ref[h]
            m_new = jnp.maximum(m_prev, jnp.max(s, axis=0, keepdims=True))
            alpha = jnp.exp2(m_prev - m_new)
            p = jnp.exp2(s - m_new).astype(BF16)
            acc_ref[h] = alpha * acc_ref[h] + _dot(vts[h], p)
            m_ref[h] = m_new

    @pl.when(j < n_cache)
    def _():
        slot = j & 1

        @pl.when(j + 1 < n_cache)
        def _():
            fetch(b, j + 1, 1 - slot)

        for cp in cache_copies(b, j, slot):
            cp.wait()
        update(lambda h: kbuf[slot, h], lambda h: vbuf[slot, h], tkc, None)

    @pl.when(j == n_cache)
    def _():
        @pl.when(b + 1 < pl.num_programs(0))
        def _():
            fetch(b + 1, 0, 0)

        col = lax.broadcasted_iota(jnp.int32, (lq, 2 * lq), 1)
        qpos = pos0 + jnp.where(col >= lq, col - lq, col)
        kpos = pos0 + lax.broadcasted_iota(jnp.int32, (lq, 2 * lq), 0)
        update(lambda h: kn_ref[:, hsl(h)], lambda h: vn_ref[:, hsl(h)], lq,
               jnp.where(_chunk_mask(kpos, qpos), 0.0, NEG))
        lam = _diff_lambda(lq1_ref, lk1_ref, lq2_ref, lk2_ref, lam_init)
        outs = []
        for h in range(n_heads):
            acc = acc_ref[h, 0:dv, :]
            l = acc_ref[h, dv:dv + 1, :]
            o_t = acc[:, :lq] / l[:, :lq] - lam * (acc[:, lq:] / l[:, lq:])
            outs.append(_rms(jnp.transpose(o_t), sw_ref[...]) * (1.0 - lam_init))
        o_ref[...] = jnp.concatenate(outs, axis=1).astype(BF16)


def _diff_attention(qsrc, q_col, k, v, k_cache, v_cache, lq1, lk1, lq2, lk2, subln_w, n_heads, layer):
    bn, l = k.shape[:2]
    dv = LANES
    dh = dv // 2
    d_att = n_heads * dv
    lam_init = 0.8 - 0.6 * math.exp(-0.3 * layer)
    scale = dh ** -0.5
    params = [lq1.reshape(1, dh), lk1.reshape(1, dh), lq2.reshape(1, dh), lk2.reshape(1, dh),
              subln_w.reshape(1, dv)]
    if k_cache is None:
        t = min(512, l)
        nblk = l // t
        assert t % CHUNK == 0 and l % t == 0
        small = lambda n: pl.BlockSpec((1, n), lambda b, h, i: (0, 0))
        kern = functools.partial(_attn_prompt_kernel, t=t, nblk=nblk, dh=dh, scale=scale, lam_init=lam_init)
        return pl.pallas_call(
            kern,
            grid=(bn, n_heads, nblk),
            in_specs=[
                pl.BlockSpec((None, t, dv), lambda b, h, i: (b, i, q_col + h)),
                pl.BlockSpec((None, l, dv), lambda b, h, i: (b, 0, h)),
                pl.BlockSpec((None, l, dv), lambda b, h, i: (b, 0, h)),
                small(dh), small(dh), small(dh), small(dh), small(dv),
            ],
            out_specs=pl.BlockSpec((None, t, dv), lambda b, h, i: (b, i, h)),
            out_shape=jax.ShapeDtypeStruct((bn, l, d_att), BF16),
            scratch_shapes=[
                pltpu.VMEM((nblk, dv + ONES_ROWS, t), BF16),
                pltpu.VMEM((t, 2 * t), F32),
                pltpu.VMEM((t, 2 * t), F32),
                pltpu.VMEM((1, 2 * t), F32),
                pltpu.VMEM((dv + ONES_ROWS, 2 * t), F32),
            ],
            compiler_params=_cparams("parallel", "parallel", "arbitrary"),
            name="diff_attention_prompt",
        )(qsrc, k, v, *params)

    pos0 = k_cache.shape[1]
    tkc = min(512, pos0)
    assert pos0 % CHUNK == 0 and pos0 % tkc == 0 and l % SUBLANES == 0
    n_cache = pos0 // tkc
    small = lambda n: pl.BlockSpec((1, n), lambda b, j: (0, 0))
    kern = functools.partial(_attn_cached_kernel, n_heads=n_heads, lq=l, n_cache=n_cache, tkc=tkc, pos0=pos0,
                             dh=dh, scale=scale, lam_init=lam_init)
    cache_spec = pl.BlockSpec(memory_space=pl.ANY)
    new_spec = pl.BlockSpec((None, l, d_att), lambda b, j: (b, 0, 0))
    return pl.pallas_call(
        kern,
        grid=(bn, n_cache + 1),
        in_specs=[
            pl.BlockSpec((None, l, d_att), lambda b, j: (b, 0, q_col * dv // d_att)),
            cache_spec, cache_spec, new_spec, new_spec,
            small(dh), small(dh), small(dh), small(dh), small(dv),
        ],
        out_specs=pl.BlockSpec((None, l, d_att), lambda b, j: (b, 0, 0)),
        out_shape=jax.ShapeDtypeStruct((bn, l, d_att), BF16),
        scratch_shapes=[
            pltpu.VMEM((n_heads, dv, 2 * l), BF16),
            pltpu.VMEM((n_heads, 1, 2 * l), F32),
            pltpu.VMEM((n_heads, dv + ONES_ROWS, 2 * l), F32),
            pltpu.VMEM((2, n_heads, tkc, dv), F32),
            pltpu.VMEM((2, n_heads, tkc, dv), F32),
            pltpu.SemaphoreType.DMA((2, 2)),
        ],
        compiler_params=_cparams("arbitrary", "arbitrary"),
        name="diff_attention_cached",
    )(qsrc, k_cache, v_cache, k, v, *params)


def _ssd_kernel(z_ref, x_ref, bc_ref, dt_ref, cbx_ref, cbbc_ref, s0_ref, cwx_ref, cbiasx_ref, cwbc_ref,
                cbiasbc_ref, dtb_ref, alog_ref, dskip_ref, nw_ref, expand_ref,
                y_ref, s_ref, xsx_ref, xsbc_ref, *, n_grp, hpg, p_dim, n_dim):
    t = pl.program_id(1)
    q = CHUNK
    d_ssd = n_grp * hpg * p_dim
    gw = hpg * p_dim

    @pl.when(t == 0)
    def _():
        _conv_init(xsx_ref, cbx_ref[...])
        _conv_init(xsbc_ref, cbbc_ref[...])
        s_ref[...] = s0_ref[...]

    xs_all = _silu(_conv_tile(xsx_ref, x_ref[...], cwx_ref, cbiasx_ref))
    bcs_all = _silu(_conv_tile(xsbc_ref, bc_ref[...], cwbc_ref, cbiasbc_ref))

    a = -jnp.exp(alog_ref[...])
    expand = expand_ref[...]
    nw = nw_ref[...]
    row = lax.broadcasted_iota(jnp.int32, (q, q), 0)
    col = lax.broadcasted_iota(jnp.int32, (q, q), 1)
    causal = col <= row
    lane = lax.broadcasted_iota(jnp.int32, (q, LANES), 1)
    heads_per_tile = LANES // p_dim
    grp = lambda v, g: v[:, g * gw:(g + 1) * gw]

    def prepare(rows):
        xs, bcs = xs_all[rows], bcs_all[rows]
        dt = _softplus(dt_ref[rows, :] + dtb_ref[...])
        acum = _cumsum_rows(dt * a)
        a_last = acum[q - 1:q, :]
        wst = jnp.exp(a_last - acum) * dt
        eac = jnp.exp(acum)
        acum_t = jnp.transpose(acum)
        stacked = jnp.concatenate([dt, wst, eac], axis=0)
        hi = stacked.astype(BF16)
        rem = stacked - hi.astype(F32)
        mid = rem.astype(BF16)
        lo = (rem - mid.astype(F32)).astype(BF16)
        expd = _dot(hi, expand) + _dot(mid, expand) + _dot(lo, expand)
        dt_e, wst_e, eac_e = expd[0:q], expd[q:2 * q], expd[2 * q:3 * q]
        xdt = (xs * dt_e).astype(BF16)
        xw = (xs * wst_e).astype(BF16)
        bgs = [bcs[:, g * n_dim:(g + 1) * n_dim] for g in range(n_grp)]
        cgs = [bcs[:, (n_grp + g) * n_dim:(n_grp + g + 1) * n_dim].astype(BF16) for g in range(n_grp)]
        cbs = [lax.dot_general(cgs[g], bgs[g].astype(BF16), _NT, preferred_element_type=F32)
               for g in range(n_grp)]
        bg_ts = [jnp.transpose(bgs[g]).astype(BF16) for g in range(n_grp)]
        wdec = []
        for h in range(n_grp * hpg):
            dec = jnp.exp(jnp.where(causal, acum[:, h:h + 1] - acum_t[h:h + 1, :], -jnp.inf))
            wdec.append((cbs[h // hpg] * dec).astype(BF16))
        y_diag = []
        for tile in range(d_ssd // LANES):
            c0 = tile * LANES
            x_tile = xdt[:, c0:c0 + LANES]
            acc = None
            for hh in range(heads_per_tile):
                yh = _dot(wdec[c0 // p_dim + hh], x_tile)
                sel = jnp.logical_and(lane >= hh * p_dim, lane < (hh + 1) * p_dim)
                acc = jnp.where(sel, yh, 0.0) if acc is None else jnp.where(sel, yh, acc)
            y_diag.append(acc)
        y_diag = jnp.concatenate(y_diag, axis=1) + dskip_ref[...] * xs
        return y_diag, _silu(z_ref[rows, :]), cgs, bg_ts, xw, eac_e

    def advance(rows, prep):
        y_diag, gate, cgs, bg_ts, xw, eac_e = prep
        y_parts = []
        for g in range(n_grp):
            s_prev = s_ref[g]
            y_parts.append(_dot(cgs[g], s_prev.astype(BF16)) * grp(eac_e, g))
            s_ref[g] = grp(eac_e, g)[q - 1:q, :] * s_prev + _dot(bg_ts[g], grp(xw, g))
        y = (y_diag + jnp.concatenate(y_parts, axis=1)) * gate
        y_ref[rows, :] = jnp.concatenate([_rms(grp(y, g), grp(nw, g)) for g in range(n_grp)],
                                         axis=1).astype(BF16)

    chunks = [slice(r0, r0 + q) for r0 in range(0, x_ref.shape[0], q)]
    preps = [prepare(rows) for rows in chunks]
    for rows, prep in zip(chunks, preps):
        advance(rows, prep)


def _ssd(proj, cols, conv_buf, s0, conv_w, conv_b, dt_bias, a_log, d_skip, norm_w, n_grp, n_dim, p_dim):
    bn, l, _ = proj.shape
    n_heads = dt_bias.shape[0]
    hpg = n_heads // n_grp
    d_ssd = n_heads * p_dim
    d_bc = 2 * n_grp * n_dim
    gw = hpg * p_dim
    q = 4 * CHUNK if l % (4 * CHUNK) == 0 else CHUNK
    assert l % q == 0 and n_heads <= LANES and LANES % p_dim == 0
    z_col, x_col, bc_col, dt_col = cols
    pad = lambda v: jnp.pad(v.astype(F32), (0, LANES - n_heads)).reshape(1, LANES)
    expand = (jnp.arange(LANES)[:, None] == (jnp.arange(d_ssd)[None, :] // p_dim)).astype(BF16)
    s0_t = s0.reshape(bn, n_grp, hpg, p_dim, n_dim).transpose(0, 1, 4, 2, 3).reshape(bn, n_grp, n_dim, gw)
    const2 = lambda b, t: (0, 0)
    kern = functools.partial(_ssd_kernel, n_grp=n_grp, hpg=hpg, p_dim=p_dim, n_dim=n_dim)
    y, s_last = pl.pallas_call(
        kern,
        grid=(bn, l // q),
        in_specs=[
            pl.BlockSpec((None, q, d_ssd), lambda b, t: (b, t, z_col)),
            pl.BlockSpec((None, q, d_ssd), lambda b, t: (b, t, x_col)),
            pl.BlockSpec((None, q, d_bc), lambda b, t: (b, t, bc_col)),
            pl.BlockSpec((None, q, LANES), lambda b, t: (b, t, dt_col)),
            pl.BlockSpec((None, CONV_W - 1, d_ssd), lambda b, t: (b, 0, 0)),
            pl.BlockSpec((None, CONV_W - 1, d_bc), lambda b, t: (b, 0, 0)),
            pl.BlockSpec((None, n_grp, n_dim, gw), lambda b, t: (b, 0, 0, 0)),
            pl.BlockSpec((CONV_W * SUBLANES, d_ssd), const2),
            pl.BlockSpec((SUBLANES, d_ssd), const2),
            pl.BlockSpec((CONV_W * SUBLANES, d_bc), const2),
            pl.BlockSpec((SUBLANES, d_bc), const2),
            pl.BlockSpec((1, LANES), const2),
            pl.BlockSpec((1, LANES), const2),
            pl.BlockSpec((1, d_ssd), const2),
            pl.BlockSpec((1, d_ssd), const2),
            pl.BlockSpec((LANES, d_ssd), const2),
        ],
        out_specs=[
            pl.BlockSpec((None, q, d_ssd), lambda b, t: (b, t, 0)),
            pl.BlockSpec((None, n_grp, n_dim, gw), lambda b, t: (b, 0, 0, 0)),
        ],
        out_shape=[
            jax.ShapeDtypeStruct((bn, l, d_ssd), BF16),
            jax.ShapeDtypeStruct((bn, n_grp, n_dim, gw), F32),
        ],
        scratch_shapes=[
            pltpu.VMEM((SUBLANES, d_ssd), F32),
            pltpu.VMEM((SUBLANES, d_bc), F32),
        ],
        compiler_params=_cparams("parallel", "arbitrary"),
        name="ssd",
    )(proj, proj, proj, proj, conv_buf[:, :, :d_ssd], conv_buf[:, :, d_ssd:], s0_t,
      *_conv_rep(conv_w[:, :d_ssd], conv_b[:d_ssd]), *_conv_rep(conv_w[:, d_ssd:], conv_b[d_ssd:]),
      pad(dt_bias), pad(a_log), jnp.repeat(d_skip.astype(F32), p_dim).reshape(1, d_ssd),
      norm_w.reshape(1, d_ssd), expand)
    s_last = s_last.reshape(bn, n_grp, n_dim, hpg, p_dim).transpose(0, 1, 3, 4, 2).reshape(bn, n_heads, p_dim, n_dim)
    return y, s_last


def _hgrn_kernel(q_ref, f_ref, i_ref, lb_ref, nw_ref, s0_ref, y_ref, slast_ref, *, n_heads):
    t = pl.program_id(1)
    c = CHUNK
    nsb = c // HG_SUB

    @pl.when(t == 0)
    def _():
        slast_ref[...] = s0_ref[...]

    lbp = lb_ref[...]
    e = jnp.exp(lbp - jnp.max(lbp, axis=0, keepdims=True))
    lb = e[1:2, :] / jnp.sum(e, axis=0, keepdims=True)
    zero_blk = jnp.zeros((HG_SUB, LANES), F32)
    r2 = lax.broadcasted_iota(jnp.int32, (c, c), 0)
    c2 = lax.broadcasted_iota(jnp.int32, (c, c), 1)
    causal = c2 <= r2
    nw = nw_ref[...]

    def prepare(rows):
        g = lb + (1.0 - lb) * _sigmoid(f_ref[rows, :])
        logg = jnp.log(g)
        kk = 1.0 - g
        qq = _silu(q_ref[rows, :])
        b = _cumsum_rows(logg)
        iv = i_ref[rows, :]
        b_last = b[c - 1:c, :]
        qe = (qq * jnp.exp(b)).astype(BF16)
        kd = (kk * jnp.exp(b_last - b)).astype(BF16)
        s_decay = jnp.exp(b_last)
        ivb = iv.astype(BF16)
        sb = lambda x, r: x[r * HG_SUB:(r + 1) * HG_SUB, :]
        m_off = [b[a * HG_SUB - 1:a * HG_SUB, :] for a in range(1, nsb)]
        m_mid = [b[r * HG_SUB + HG_SUB // 2 - 1:r * HG_SUB + HG_SUB // 2, :] for r in range(nsb)]
        q_off = [None] + [sb(qq, a) * jnp.exp(sb(b, a) - m_off[a - 1]) for a in range(1, nsb)]
        q_mid = [sb(qq, r) * jnp.exp(sb(b, r) - m_mid[r]) for r in range(nsb)]
        k_mid = [sb(kk, r) * jnp.exp(m_mid[r] - sb(b, r)) for r in range(nsb)]
        k_off = [[sb(kk, r) * jnp.exp(m_off[a - 1] - sb(b, r)) if a > r else None for a in range(1, nsb)]
                 for r in range(nsb)]
        atts = []
        for h in range(n_heads):
            sl = slice(h * LANES, (h + 1) * LANES)
            q_rows, k_rows = [], []
            for r in range(nsb):
                q_parts = [q_off[a][:, sl] if a == r else zero_blk for a in range(1, nsb)]
                k_parts = [k_off[r][a - 1][:, sl] if a > r else zero_blk for a in range(1, nsb)]
                q_parts += [q_mid[r][:, sl] if d == r else zero_blk for d in range(nsb)]
                k_parts += [k_mid[r][:, sl] if d == r else zero_blk for d in range(nsb)]
                q_rows.append(jnp.concatenate(q_parts, axis=1))
                k_rows.append(jnp.concatenate(k_parts, axis=1))
            qcat = jnp.concatenate(q_rows, axis=0).astype(BF16)
            kcat = jnp.concatenate(k_rows, axis=0).astype(BF16)
            att = lax.dot_general(qcat, kcat, _NT, preferred_element_type=F32)
            atts.append(jnp.where(causal, att, 0.0).astype(BF16))
        return atts, ivb, iv, qe, kd, s_decay

    def advance(rows, prep):
        atts, ivb, iv, qe, kd, s_decay = prep
        outs = []
        for h in range(n_heads):
            sl = slice(h * LANES, (h + 1) * LANES)
            s_t = slast_ref[h]
            o = _dot(atts[h], ivb[:, sl])
            o = o + lax.dot_general(qe[:, sl], s_t.astype(BF16), _NT, preferred_element_type=F32)
            slast_ref[h] = s_decay[:, sl] * s_t + _dot(jnp.transpose(iv[:, sl]).astype(BF16), kd[:, sl])
            outs.append(_rms(o, nw))
        y_ref[rows, :] = jnp.concatenate(outs, axis=1).astype(BF16)

    chunks = [slice(r0, r0 + c) for r0 in range(0, q_ref.shape[0], c)]
    preps = [prepare(rows) for rows in chunks]
    for rows, prep in zip(chunks, preps):
        advance(rows, prep)


def _hgrn(proj, cols, s0, lb_param, norm_w, n_heads):
    bn, l, _ = proj.shape
    d = n_heads * LANES
    c = 4 * CHUNK if l % (4 * CHUNK) == 0 else CHUNK
    assert l % c == 0
    q_col, f_col, i_col = cols
    depth = lb_param.shape[0]
    assert depth == 2
    s0_t = jnp.swapaxes(s0, -1, -2)
    kern = functools.partial(_hgrn_kernel, n_heads=n_heads)
    y, s_last = pl.pallas_call(
        kern,
        grid=(bn, l // c),
        in_specs=[
            pl.BlockSpec((None, c, d), lambda b, t: (b, t, q_col)),
            pl.BlockSpec((None, c, d), lambda b, t: (b, t, f_col)),
            pl.BlockSpec((None, c, d), lambda b, t: (b, t, i_col)),
            pl.BlockSpec((depth, d), lambda b, t: (0, 0)),
            pl.BlockSpec((1, LANES), lambda b, t: (0, 0)),
            pl.BlockSpec((None, n_heads, LANES, LANES), lambda b, t: (b, 0, 0, 0)),
        ],
        out_specs=[
            pl.BlockSpec((None, c, d), lambda b, t: (b, t, 0)),
            pl.BlockSpec((None, n_heads, LANES, LANES), lambda b, t: (b, 0, 0, 0)),
        ],
        out_shape=[
            jax.ShapeDtypeStruct((bn, l, d), BF16),
            jax.ShapeDtypeStruct((bn, n_heads, LANES, LANES), F32),
        ],
        compiler_params=_cparams("parallel", "arbitrary"),
        name="hgrn2",
    )(proj, proj, proj, lb_param, norm_w.reshape(1, LANES), s0_t)
    return y, jnp.swapaxes(s_last, -1, -2)


def kernel(x_prompt, x_sample, cache_diff_k, cache_diff_v, state_rglru_conv, state_rglru_h, state_ssd_conv,
           state_ssd, state_hgrn, norm_w, l0_w_in, l0_conv_w, l0_conv_b, l0_rg_w_r, l0_rg_b_r, l0_rg_w_i,
           l0_rg_b_i, l0_rg_lambda, l0_lq1, l0_lk1, l0_lq2, l0_lk2, l0_subln_w, l0_w_out, l1_w_in, l1_conv_w,
           l1_conv_b, l1_dt_bias, l1_a_log, l1_d_skip, l1_ssd_norm_w, l1_hg_lower_bound, l1_hg_norm_w, l1_w_out,
           ffn_w_gate, ffn_w_up, ffn_w_down):
    d_model = x_prompt.shape[-1]
    d_rnn = l0_conv_w.shape[1]
    n_heads_b, dv_b = cache_diff_k.shape[2], cache_diff_k.shape[3]
    d_att = n_heads_b * dv_b
    n_heads_c, p_c, n_c = state_ssd.shape[1:]
    d_ssd = n_heads_c * p_c
    d_xbc = l1_conv_w.shape[1]
    d_bc = d_xbc - d_ssd
    n_grp = d_bc // (2 * n_c)
    n_heads_d, dk_d, dv_d = state_hgrn.shape[1:]
    d_hg = n_heads_d * dk_d
    assert dv_b == LANES and dk_d == LANES and dv_d == LANES and d_rnn == d_att == d_ssd == d_hg

    w_in0_main = l0_w_in[:, :2 * d_rnn + d_att].astype(BF16)
    w_in0_kv = l0_w_in[:, 2 * d_rnn + d_att:].astype(BF16)
    o1 = d_ssd
    o2 = o1 + d_xbc
    o3 = o2 + n_heads_c
    w1 = l1_w_in
    w_in1 = jnp.concatenate(
        [w1[:, :o1], w1[:, o1:o1 + d_ssd], w1[:, o3:], w1[:, o1 + d_ssd:o2],
         jnp.pad(w1[:, o2:o3], ((0, 0), (0, LANES - n_heads_c)))], axis=1).astype(BF16)
    w_out0 = l0_w_out.astype(BF16)
    w_out1 = l1_w_out.astype(BF16)
    wg = ffn_w_gate.astype(BF16)
    wu = ffn_w_up.astype(BF16)
    wd = ffn_w_down.astype(BF16)
    w_r = l0_rg_w_r.astype(BF16)
    w_i = l0_rg_w_i.astype(BF16)
    tn0 = 1024
    tn1 = 1152
    assert w_in0_main.shape[1] % tn0 == 0 and w_in1.shape[1] % tn1 == 0
    ssd_cols = (0, 1, (2 * d_ssd + 3 * d_hg) // d_bc, (2 * d_ssd + 3 * d_hg + d_bc) // LANES)
    hg_cols = (2, 3, 4)

    def trunk(x, k_cache, v_cache, rg_conv, rg_h, ssd_conv, ssd_s, hg_s):
        bn, l, _ = x.shape
        m = bn * l
        x0 = x.reshape(m, d_model)
        proj0, xn0 = _norm_matmul(x0, norm_w[0, 0], w_in0_main, tn0)
        proj0 = proj0.reshape(bn, l, -1)
        k_new, v_new, k_bf, v_bf = _head_proj(xn0, w_in0_kv, n_heads_b)
        k_new = k_new.reshape(bn, l, n_heads_b, dv_b)
        v_new = v_new.reshape(bn, l, n_heads_b, dv_b)
        ya, rg_h_new = _rglru(proj0, rg_conv, rg_h, l0_conv_w, l0_conv_b, w_r, l0_rg_b_r, w_i, l0_rg_b_i,
                              l0_rg_lambda, d_rnn)
        yb = _diff_attention(proj0, 2 * d_rnn // LANES, k_bf.reshape(bn, l, d_att), v_bf.reshape(bn, l, d_att),
                             k_cache, v_cache, l0_lq1, l0_lk1, l0_lq2, l0_lk2, l0_subln_w, n_heads_b, 0)
        rg_conv_new = proj0[:, l - (CONV_W - 1):, :d_rnn]
        x1, hn1 = _out_proj(ya.reshape(m, d_rnn), yb.reshape(m, d_att), w_out0, x0, norm_w[0, 1], norm_w[0, 2])
        x2, xn2 = _ffn(x1, hn1, norm_w[0, 3], norm_w[1, 0], wg, wu, wd, 0)
        proj1 = _matmul(xn2, w_in1, tn1).reshape(bn, l, -1)
        ys, ssd_s_new = _ssd(proj1, ssd_cols, ssd_conv, ssd_s, l1_conv_w, l1_conv_b, l1_dt_bias, l1_a_log,
                             l1_d_skip, l1_ssd_norm_w, n_grp, n_c, p_c)
        yh, hg_s_new = _hgrn(proj1, hg_cols, hg_s, l1_hg_lower_bound, l1_hg_norm_w, n_heads_d)
        tail = proj1[:, l - (CONV_W - 1):, :]
        ssd_conv_new = jnp.concatenate(
            [tail[:, :, d_ssd:2 * d_ssd], tail[:, :, 2 * d_ssd + 3 * d_hg:2 * d_ssd + 3 * d_hg + d_bc]], axis=-1)
        x3, hn3 = _out_proj(ys.reshape(m, d_ssd), yh.reshape(m, d_hg), w_out1, x2, norm_w[1, 1], norm_w[1, 2])
        x4 = _ffn(x3, hn3, norm_w[1, 3], None, wg, wu, wd, 1)[0]
        return (x4.reshape(bn, l, d_model), k_new, v_new, rg_conv_new, rg_h_new, ssd_conv_new, ssd_s_new, hg_s_new)

    bp = x_prompt.shape[0]
    zeros = lambda *s: jnp.zeros(s, F32)
    outs_p = trunk(x_prompt, None, None, zeros(bp, CONV_W - 1, d_rnn), zeros(bp, d_rnn),
                   zeros(bp, CONV_W - 1, d_xbc), zeros(bp, n_heads_c, p_c, n_c), zeros(bp, n_heads_d, dk_d, dv_d))
    outs_s = trunk(x_sample, cache_diff_k, cache_diff_v, state_rglru_conv, state_rglru_h, state_ssd_conv,
                   state_ssd, state_hgrn)
    return (outs_p[0], outs_s[0]) + tuple(outs_p[1:]) + tuple(outs_s[1:])
```

```python
import functools
import math

import jax
import jax.numpy as jnp
from jax import lax
from jax.experimental import pallas as pl
from jax.experimental.pallas import tpu as pltpu

F32 = jnp.float32
BF16 = jnp.bfloat16

EPS = 1e-6
CHUNK = 64
CONV_W = 4
RG_C = 8.0
LANES = 128
SUBLANES = 8
HG_SUB = 16
NEG = -1e30
VMEM_LIMIT_BYTES = 56 * 1024 * 1024

PROJ_ROWS = 1024
EPILOGUE_ROWS = 512
FFN_COLS = 512
IN_PROJ_COLS = (1024, 1152)
ATTN_BLOCK = 512
CACHE_BLOCK = 1024
RGLRU_ROWS = 256
CHUNKS_PER_STEP = 4

_NT = (((1,), (1,)), ((), ()))


def _cparams(*sem):
    return pltpu.CompilerParams(dimension_semantics=sem, vmem_limit_bytes=VMEM_LIMIT_BYTES)


def _sigmoid(x):
    return 0.5 * jnp.tanh(0.5 * x) + 0.5


def _silu(x):
    h = 0.5 * x
    return h * jnp.tanh(h) + h


def _softplus(x):
    return jnp.maximum(x, 0.0) + jnp.log1p(jnp.exp(-jnp.abs(x)))


def _gelu_tanh(x):
    c = math.sqrt(2.0 / math.pi)
    return x * (0.5 * (1.0 + jnp.tanh(c * (x + 0.044715 * (x * x * x)))))


def _rms(x, w):
    return x * lax.rsqrt(jnp.mean(x * x, axis=-1, keepdims=True) + EPS) * w


def _dot(a, b):
    return jnp.dot(a, b, preferred_element_type=F32)


def _cumsum_rows(x):
    n = x.shape[0]
    row = lax.broadcasted_iota(jnp.int32, x.shape, 0)
    s = 1
    while s < n:
        x = x + jnp.where(row >= s, pltpu.roll(x, s, axis=0), 0.0)
        s *= 2
    return x


def _norm_matmul_kernel(x_ref, nw_ref, w_ref, o_ref, xn_ref):
    @pl.when(pl.program_id(1) == 0)
    def _():
        xn_ref[...] = _rms(x_ref[...], nw_ref[...]).astype(BF16)

    o_ref[...] = _dot(xn_ref[...], w_ref[...])


def _norm_matmul(x, nw, w, tn):
    m, d = x.shape
    n = w.shape[1]
    assert n % tn == 0
    tm = min(PROJ_ROWS, m)
    return pl.pallas_call(
        _norm_matmul_kernel,
        grid=(m // tm, n // tn),
        in_specs=[
            pl.BlockSpec((tm, d), lambda i, j: (i, 0)),
            pl.BlockSpec((1, d), lambda i, j: (0, 0)),
            pl.BlockSpec((d, tn), lambda i, j: (0, j)),
        ],
        out_specs=[pl.BlockSpec((tm, tn), lambda i, j: (i, j)), pl.BlockSpec((tm, d), lambda i, j: (i, 0))],
        out_shape=[jax.ShapeDtypeStruct((m, n), F32), jax.ShapeDtypeStruct((m, d), BF16)],
        compiler_params=_cparams("parallel", "arbitrary"),
        name="norm_in_proj",
    )(x, nw.reshape(1, d), w)


def _head_proj_kernel(xn_ref, w_ref, *o_refs):
    n_out = len(o_refs) // 2
    j = pl.program_id(1)
    for k in range(n_out):
        @pl.when(j == k)
        def _(k=k):
            o_ref, ob_ref = o_refs[k], o_refs[n_out + k]
            res = _dot(xn_ref[...], w_ref[...])
            for h in range(o_ref.shape[1]):
                o_ref[:, h, :] = res[:, h * LANES:(h + 1) * LANES]
            ob_ref[...] = res.astype(BF16)


def _head_proj(xn, w, n_heads):
    m, d = xn.shape
    width = n_heads * LANES
    n_out = w.shape[1] // width
    tm = min(PROJ_ROWS, m)
    return pl.pallas_call(
        _head_proj_kernel,
        grid=(m // tm, n_out),
        in_specs=[
            pl.BlockSpec((tm, d), lambda i, j: (i, 0)),
            pl.BlockSpec((d, width), lambda i, j: (0, j)),
        ],
        out_specs=([pl.BlockSpec((tm, n_heads, LANES), lambda i, j: (i, 0, 0)) for _ in range(n_out)]
                   + [pl.BlockSpec((tm, width), lambda i, j: (i, 0)) for _ in range(n_out)]),
        out_shape=([jax.ShapeDtypeStruct((m, n_heads, LANES), F32) for _ in range(n_out)]
                   + [jax.ShapeDtypeStruct((m, width), BF16) for _ in range(n_out)]),
        compiler_params=_cparams("parallel", "arbitrary"),
        name="head_proj",
    )(xn, w)


def _out_proj_kernel(ya_ref, yb_ref, wa_ref, wb_ref, x_ref, nw_ref, nwn_ref, o_ref, hn_ref):
    tm = x_ref.shape[0]
    grp = max(tm // 4, LANES)
    for r0 in range(0, tm, grp):
        rows = slice(r0, r0 + grp)
        m = _dot(ya_ref[rows, :], wa_ref[...]) + _dot(yb_ref[rows, :], wb_ref[...])
        o = x_ref[rows, :] + _rms(m, nw_ref[...])
        o_ref[rows, :] = o
        hn_ref[rows, :] = _rms(o, nwn_ref[...]).astype(BF16)


def _out_proj(ya, yb, w_out, x, nw, nw_next):
    m, d = x.shape
    da, db = ya.shape[1], yb.shape[1]
    assert da == db and w_out.shape[0] == da + db
    tm = min(EPILOGUE_ROWS, m)
    row_spec = pl.BlockSpec((tm, d), lambda i: (i, 0))
    vec_spec = pl.BlockSpec((1, d), lambda i: (0, 0))
    return pl.pallas_call(
        _out_proj_kernel,
        grid=(m // tm,),
        in_specs=[
            pl.BlockSpec((tm, da), lambda i: (i, 0)),
            pl.BlockSpec((tm, db), lambda i: (i, 0)),
            pl.BlockSpec((da, d), lambda i: (0, 0)),
            pl.BlockSpec((db, d), lambda i: (1, 0)),
            row_spec, vec_spec, vec_spec,
        ],
        out_specs=[row_spec, row_spec],
        out_shape=[jax.ShapeDtypeStruct((m, d), F32), jax.ShapeDtypeStruct((m, d), BF16)],
        compiler_params=_cparams("parallel"),
        name="out_proj",
    )(ya, yb, w_out, w_out, x, nw.reshape(1, d), nw_next.reshape(1, d))


def _ffn_kernel(x_ref, hn_ref, nwb_ref, nwn_ref, wg_ref, wu_ref, wd_ref, o_ref, *rest, emit_next):
    acc_ref = rest[-1]
    f = pl.program_id(1)

    @pl.when(f == 0)
    def _():
        acc_ref[...] = jnp.zeros_like(acc_ref)

    hn = hn_ref[...]
    g = _dot(hn, wg_ref[...])
    u = _dot(hn, wu_ref[...])
    a = (_silu(g) * u).astype(BF16)
    acc_ref[...] += _dot(a, wd_ref[...])

    @pl.when(f == pl.num_programs(1) - 1)
    def _():
        o = x_ref[...] + _rms(acc_ref[...], nwb_ref[...])
        o_ref[...] = o
        if emit_next:
            rest[0][...] = _rms(o, nwn_ref[...]).astype(BF16)


def _ffn(x, hn, nw_post, nw_next, wg, wu, wd, layer):
    emit_next = nw_next is not None
    nw_next = nw_post if nw_next is None else nw_next
    m, d = x.shape
    dff = wg.shape[2]
    tm = min(EPILOGUE_ROWS, m)
    tf = FFN_COLS
    row_spec = pl.BlockSpec((tm, d), lambda i, f: (i, 0))
    vec_spec = pl.BlockSpec((1, d), lambda i, f: (0, 0))
    out_specs = [row_spec]
    out_shape = [jax.ShapeDtypeStruct((m, d), F32)]
    if emit_next:
        out_specs.append(row_spec)
        out_shape.append(jax.ShapeDtypeStruct((m, d), BF16))
    return pl.pallas_call(
        functools.partial(_ffn_kernel, emit_next=emit_next),
        grid=(m // tm, dff // tf),
        in_specs=[
            row_spec, row_spec, vec_spec, vec_spec,
            pl.BlockSpec((None, d, tf), lambda i, f: (layer, 0, f)),
            pl.BlockSpec((None, d, tf), lambda i, f: (layer, 0, f)),
            pl.BlockSpec((None, tf, d), lambda i, f: (layer, f, 0)),
        ],
        out_specs=out_specs,
        out_shape=out_shape,
        scratch_shapes=[pltpu.VMEM((tm, d), F32)],
        compiler_params=_cparams("parallel", "arbitrary"),
        name="ffn",
    )(x, hn, nw_post.reshape(1, d), nw_next.reshape(1, d), wg, wu, wd)


def _matmul_kernel(x_ref, w_ref, o_ref):
    o_ref[...] = _dot(x_ref[...], w_ref[...])


def _matmul(x, w, tn):
    m, d = x.shape
    n = w.shape[1]
    assert n % tn == 0
    tm = min(PROJ_ROWS, m)
    return pl.pallas_call(
        _matmul_kernel,
        grid=(m // tm, n // tn),
        in_specs=[pl.BlockSpec((tm, d), lambda i, j: (i, 0)), pl.BlockSpec((d, tn), lambda i, j: (0, j))],
        out_specs=pl.BlockSpec((tm, tn), lambda i, j: (i, j)),
        out_shape=jax.ShapeDtypeStruct((m, n), F32),
        compiler_params=_cparams("parallel", "parallel"),
        name="in_proj",
    )(x, w)


def _conv_init(prev_ref, conv_buf):
    prev_ref[...] = jnp.zeros_like(prev_ref)
    prev_ref[SUBLANES - (CONV_W - 1):SUBLANES, :] = conv_buf


def _conv_rep(conv_w, conv_b):
    c = conv_w.shape[1]
    return jnp.repeat(conv_w, SUBLANES, axis=0), jnp.broadcast_to(conv_b.reshape(1, c), (SUBLANES, c))


def _conv_tile(prev_ref, x, cw_ref, cb_ref):
    tl, c = x.shape
    prev = prev_ref[...]
    row = lax.broadcasted_iota(jnp.int32, prev.shape, 0)
    tap = lambda i: cw_ref[i * SUBLANES:(i + 1) * SUBLANES, :][None]
    tiles = lambda v: v.reshape(tl // SUBLANES, SUBLANES, c)
    y = cb_ref[...][None] + tiles(x) * tap(CONV_W - 1)
    for k in range(1, CONV_W):
        xr = pltpu.roll(x, k, axis=0)
        head = jnp.where(row < k, pltpu.roll(prev, k, axis=0), xr[0:SUBLANES])
        xk = head if tl == SUBLANES else jnp.concatenate([head, xr[SUBLANES:]], axis=0)
        y = y + tiles(xk) * tap(CONV_W - 1 - k)
    prev_ref[...] = x[tl - SUBLANES:tl]
    return y.reshape(tl, c)


def _rglru_kernel(xa_ref, ga_ref, cbuf_ref, h0_ref, cw_ref, cb_ref, wr_ref, br_ref, wi_ref, bi_ref,
                  lam_ref, ya_ref, hlast_ref, xs_ref, a_ref, u_ref, h_ref, *, tl, n_blk):
    t = pl.program_id(1)

    @pl.when(t == 0)
    def _():
        _conv_init(xs_ref, cbuf_ref[...])
        h_ref[...] = h0_ref[...]

    xc = _conv_tile(xs_ref, xa_ref[...], cw_ref, cb_ref)
    xcb = xc.astype(BF16)
    r_pre = jnp.concatenate(
        [_dot(xcb[:, hb * LANES:(hb + 1) * LANES], wr_ref[hb]) for hb in range(n_blk)], axis=1)
    i_pre = jnp.concatenate(
        [_dot(xcb[:, hb * LANES:(hb + 1) * LANES], wi_ref[hb]) for hb in range(n_blk)], axis=1)
    r = _sigmoid(r_pre + br_ref[...])
    gi = _sigmoid(i_pre + bi_ref[...])
    nla = RG_C * r * _softplus(-lam_ref[...])
    a = jnp.exp(-nla)
    a_ref[...] = a
    u_ref[...] = jnp.sqrt(jnp.tanh(nla) * (a * a + 1.0)) * (gi * xc)

    row = lax.broadcasted_iota(jnp.int32, (SUBLANES, a_ref.shape[1]), 0)

    def group(gidx, h_prev):
        r0 = pl.multiple_of(gidx * SUBLANES, SUBLANES)
        ag = a_ref[pl.ds(r0, SUBLANES), :]
        ug = u_ref[pl.ds(r0, SUBLANES), :]
        s = 1
        while s < SUBLANES:
            a_sh = jnp.where(row >= s, pltpu.roll(ag, s, axis=0), 1.0)
            u_sh = jnp.where(row >= s, pltpu.roll(ug, s, axis=0), 0.0)
            ug = ag * u_sh + ug
            ag = ag * a_sh
            s *= 2
        hg = ag * h_prev + ug
        u_ref[pl.ds(r0, SUBLANES), :] = hg
        return hg[SUBLANES - 1:SUBLANES, :]

    h_last = lax.fori_loop(0, tl // SUBLANES, group, h_ref[...])
    h_ref[...] = h_last
    hlast_ref[...] = h_last
    ya_ref[...] = (u_ref[...] * _gelu_tanh(ga_ref[...])).astype(BF16)


def _rglru(proj, conv_buf, h0, conv_w, conv_b, w_r, b_r, w_i, b_i, lam, d_rnn):
    bn, l, _ = proj.shape
    tl = min(RGLRU_ROWS, l)
    n_blk = w_r.shape[0]
    vec = lambda v: v.reshape(1, d_rnn)
    const2 = lambda b, t: (0, 0)
    kern = functools.partial(_rglru_kernel, tl=tl, n_blk=n_blk)
    ya, h_last = pl.pallas_call(
        kern,
        grid=(bn, l // tl),
        in_specs=[
            pl.BlockSpec((None, tl, d_rnn), lambda b, t: (b, t, 0)),
            pl.BlockSpec((None, tl, d_rnn), lambda b, t: (b, t, 1)),
            pl.BlockSpec((None, CONV_W - 1, d_rnn), lambda b, t: (b, 0, 0)),
            pl.BlockSpec((None, 1, d_rnn), lambda b, t: (b, 0, 0)),
            pl.BlockSpec((CONV_W * SUBLANES, d_rnn), const2),
            pl.BlockSpec((SUBLANES, d_rnn), const2),
            pl.BlockSpec(w_r.shape, lambda b, t: (0, 0, 0)),
            pl.BlockSpec((1, d_rnn), const2),
            pl.BlockSpec(w_i.shape, lambda b, t: (0, 0, 0)),
            pl.BlockSpec((1, d_rnn), const2),
            pl.BlockSpec((1, d_rnn), const2),
        ],
        out_specs=[
            pl.BlockSpec((None, tl, d_rnn), lambda b, t: (b, t, 0)),
            pl.BlockSpec((None, 1, d_rnn), lambda b, t: (b, 0, 0)),
        ],
        out_shape=[
            jax.ShapeDtypeStruct((bn, l, d_rnn), BF16),
            jax.ShapeDtypeStruct((bn, 1, d_rnn), F32),
        ],
        scratch_shapes=[
            pltpu.VMEM((SUBLANES, d_rnn), F32),
            pltpu.VMEM((tl, d_rnn), F32),
            pltpu.VMEM((tl, d_rnn), F32),
            pltpu.VMEM((1, d_rnn), F32),
        ],
        compiler_params=_cparams("parallel", "arbitrary"),
        name="rglru",
    )(proj, proj, conv_buf, h0.reshape(bn, 1, d_rnn), *_conv_rep(conv_w, conv_b), w_r, vec(b_r), w_i, vec(b_i),
      vec(lam))
    return ya, h_last.reshape(bn, d_rnn)


def _chunk_mask(kpos, qpos):
    shift = CHUNK.bit_length() - 1
    return jnp.right_shift(kpos, shift) <= jnp.right_shift(qpos, shift)


def _diff_lambda(lq1_ref, lk1_ref, lq2_ref, lk2_ref, lam_init):
    return (jnp.exp(jnp.sum(lq1_ref[...] * lk1_ref[...], axis=-1, keepdims=True))
            - jnp.exp(jnp.sum(lq2_ref[...] * lk2_ref[...], axis=-1, keepdims=True)) + lam_init)


ONES_ROWS = 16


def _attn_prompt_kernel(q_ref, k_ref, v_ref, lq1_ref, lk1_ref, lq2_ref, lk2_ref, sw_ref, o_ref,
                        vt_ref, sa_ref, sb_ref, m_ref, acc_ref, *, t, nblk, dh, scale, lam_init):
    i = pl.program_id(2)
    dv = 2 * dh

    @pl.when(i == 0)
    def _():
        ones = jnp.ones((ONES_ROWS, t), BF16)
        for jb in range(nblk):
            vt_ref[jb, :dv, :] = jnp.transpose(v_ref[jb * t:(jb + 1) * t, :].astype(F32)).astype(BF16)
            vt_ref[jb, dv:, :] = ones

    qt = jnp.transpose(q_ref[...] * (scale * math.log2(math.e)))
    sub = lax.broadcasted_iota(jnp.int32, qt.shape, 0)
    qt2 = jnp.concatenate([jnp.where(sub < dh, qt, 0.0), jnp.where(sub >= dh, qt, 0.0)],
                          axis=1).astype(BF16)
    m_ref[...] = jnp.full_like(m_ref, NEG)
    acc_ref[...] = jnp.zeros_like(acc_ref)

    def scores(j, s_ref):
        s_ref[...] = _dot(k_ref[pl.ds(pl.multiple_of(j * t, t), t), :], qt2)

    def update(j, s_ref, diagonal):
        if diagonal:
            qoff = lax.broadcasted_iota(jnp.int32, (1, 2 * t), 1)
            qoff = jnp.where(qoff >= t, qoff - t, qoff)
            s = jnp.concatenate(
                [s_ref[c * CHUNK:(c + 1) * CHUNK, :] + jnp.where(qoff >= c * CHUNK, 0.0, NEG)
                 for c in range(t // CHUNK)], axis=0)
        else:
            s = s_ref[...]
        m_prev = m_ref[...]
        m_new = jnp.maximum(m_prev, jnp.max(s, axis=0, keepdims=True))
        alpha = jnp.exp2(m_prev - m_new)
        p = jnp.exp2(s - m_new).astype(BF16)
        acc_ref[...] = alpha * acc_ref[...] + _dot(vt_ref[j], p)
        m_ref[...] = m_new

    scores(0, sa_ref)

    def pair(pidx, carry):
        j = 2 * pidx
        scores(j + 1, sb_ref)
        update(j, sa_ref, False)
        scores(j + 2, sa_ref)
        update(j + 1, sb_ref, False)
        return carry

    lax.fori_loop(0, lax.shift_right_logical(i, 1), pair, 0)

    @pl.when((i & 1) == 1)
    def _():
        scores(i, sb_ref)
        update(i - 1, sa_ref, False)
        sa_ref[...] = sb_ref[...]

    update(i, sa_ref, True)

    lam = _diff_lambda(lq1_ref, lk1_ref, lq2_ref, lk2_ref, lam_init)
    acc = acc_ref[0:dv, :]
    l = acc_ref[dv:dv + 1, :]
    o_t = acc[:, :t] / l[:, :t] - lam * (acc[:, t:] / l[:, t:])
    o_ref[...] = (_rms(jnp.transpose(o_t), sw_ref[...]) * (1.0 - lam_init)).astype(BF16)


def _attn_cached_kernel(q_ref, kc_hbm, vc_hbm, kn_ref, vn_ref, lq1_ref, lk1_ref, lq2_ref, lk2_ref, sw_ref,
                        o_ref, qt_ref, m_ref, acc_ref, kbuf, vbuf, sem, *, n_heads, lq, n_cache, tkc, pos0,
                        dh, scale, lam_init):
    b = pl.program_id(0)
    j = pl.program_id(1)
    dv = 2 * dh
    hsl = lambda h: slice(h * LANES, (h + 1) * LANES)

    def cache_copies(bi, step, slot):
        rows = pl.ds(pl.multiple_of(step * tkc, tkc), tkc)
        return ([pltpu.make_async_copy(kc_hbm.at[bi, rows, h, :], kbuf.at[slot, h], sem.at[slot, 0])
                 for h in range(n_heads)]
                + [pltpu.make_async_copy(vc_hbm.at[bi, rows, h, :], vbuf.at[slot, h], sem.at[slot, 1])
                   for h in range(n_heads)])

    def fetch(bi, step, slot):
        for cp in cache_copies(bi, step, slot):
            cp.start()

    @pl.when(jnp.logical_and(b == 0, j == 0))
    def _():
        fetch(0, 0, 0)

    @pl.when(j == 0)
    def _():
        sub = lax.broadcasted_iota(jnp.int32, (dv, lq), 0)
        for h in range(n_heads):
            qt = jnp.transpose(q_ref[:, hsl(h)] * (scale * math.log2(math.e)))
            qt_ref[h] = jnp.concatenate([jnp.where(sub < dh, qt, 0.0), jnp.where(sub >= dh, qt, 0.0)],
                                        axis=1).astype(BF16)
        m_ref[...] = jnp.full_like(m_ref, NEG)
        acc_ref[...] = jnp.zeros_like(acc_ref)

    def update(k_of, v_of, tk, bias):
        ones = jnp.ones((ONES_ROWS, tk), BF16)
        scores = [_dot(k_of(h).astype(BF16), qt_ref[h]) for h in range(n_heads)]
        vts = [jnp.concatenate([jnp.transpose(v_of(h).astype(F32)).astype(BF16), ones], axis=0)
               for h in range(n_heads)]
        for h in range(n_heads):
            s = scores[h] if bias is None else scores[h] + bias
            m_prev = m_ref[h]
            m_new = jnp.maximum(m_prev, jnp.max(s, axis=0, keepdims=True))
            alpha = jnp.exp2(m_prev - m_new)
            p = jnp.exp2(s - m_new).astype(BF16)
            acc_ref[h] = alpha * acc_ref[h] + _dot(vts[h], p)
            m_ref[h] = m_new

    @pl.when(j < n_cache)
    def _():
        slot = j & 1

        @pl.when(j + 1 < n_cache)
        def _():
            fetch(b, j + 1, 1 - slot)

        for cp in cache_copies(b, j, slot):
            cp.wait()
        update(lambda h: kbuf[slot, h], lambda h: vbuf[slot, h], tkc, None)

    @pl.when(j == n_cache)
    def _():
        @pl.when(b + 1 < pl.num_programs(0))
        def _():
            fetch(b + 1, 0, 0)

        col = lax.broadcasted_iota(jnp.int32, (lq, 2 * lq), 1)
        qpos = pos0 + jnp.where(col >= lq, col - lq, col)
        kpos = pos0 + lax.broadcasted_iota(jnp.int32, (lq, 2 * lq), 0)
        update(lambda h: kn_ref[:, hsl(h)], lambda h: vn_ref[:, hsl(h)], lq,
               jnp.where(_chunk_mask(kpos, qpos), 0.0, NEG))
        lam = _diff_lambda(lq1_ref, lk1_ref, lq2_ref, lk2_ref, lam_init)
        outs = []
        for h in range(n_heads):
            acc = acc_ref[h, 0:dv, :]
            l = acc_ref[h, dv:dv + 1, :]
            o_t = acc[:, :lq] / l[:, :lq] - lam * (acc[:, lq:] / l[:, lq:])
            outs.append(_rms(jnp.transpose(o_t), sw_ref[...]) * (1.0 - lam_init))
        o_ref[...] = jnp.concatenate(outs, axis=1).astype(BF16)


def _diff_attention(qsrc, q_col, k, v, k_cache, v_cache, lq1, lk1, lq2, lk2, subln_w, n_heads, layer):
    bn, l = k.shape[:2]
    dv = LANES
    dh = dv // 2
    d_att = n_heads * dv
    lam_init = 0.8 - 0.6 * math.exp(-0.3 * layer)
    scale = dh ** -0.5
    params = [lq1.reshape(1, dh), lk1.reshape(1, dh), lq2.reshape(1, dh), lk2.reshape(1, dh),
              subln_w.reshape(1, dv)]
    if k_cache is None:
        t = min(ATTN_BLOCK, l)
        nblk = l // t
        assert t % CHUNK == 0 and l % t == 0
        small = lambda n: pl.BlockSpec((1, n), lambda b, h, i: (0, 0))
        kern = functools.partial(_attn_prompt_kernel, t=t, nblk=nblk, dh=dh, scale=scale, lam_init=lam_init)
        return pl.pallas_call(
            kern,
            grid=(bn, n_heads, nblk),
            in_specs=[
                pl.BlockSpec((None, t, dv), lambda b, h, i: (b, i, q_col + h)),
                pl.BlockSpec((None, l, dv), lambda b, h, i: (b, 0, h)),
                pl.BlockSpec((None, l, dv), lambda b, h, i: (b, 0, h)),
                small(dh), small(dh), small(dh), small(dh), small(dv),
            ],
            out_specs=pl.BlockSpec((None, t, dv), lambda b, h, i: (b, i, h)),
            out_shape=jax.ShapeDtypeStruct((bn, l, d_att), BF16),
            scratch_shapes=[
                pltpu.VMEM((nblk, dv + ONES_ROWS, t), BF16),
                pltpu.VMEM((t, 2 * t), F32),
                pltpu.VMEM((t, 2 * t), F32),
                pltpu.VMEM((1, 2 * t), F32),
                pltpu.VMEM((dv + ONES_ROWS, 2 * t), F32),
            ],
            compiler_params=_cparams("parallel", "parallel", "arbitrary"),
            name="diff_attention_prompt",
        )(qsrc, k, v, *params)

    pos0 = k_cache.shape[1]
    tkc = min(CACHE_BLOCK, pos0)
    assert pos0 % CHUNK == 0 and pos0 % tkc == 0 and l % SUBLANES == 0
    n_cache = pos0 // tkc
    small = lambda n: pl.BlockSpec((1, n), lambda b, j: (0, 0))
    kern = functools.partial(_attn_cached_kernel, n_heads=n_heads, lq=l, n_cache=n_cache, tkc=tkc, pos0=pos0,
                             dh=dh, scale=scale, lam_init=lam_init)
    cache_spec = pl.BlockSpec(memory_space=pl.ANY)
    new_spec = pl.BlockSpec((None, l, d_att), lambda b, j: (b, 0, 0))
    return pl.pallas_call(
        kern,
        grid=(bn, n_cache + 1),
        in_specs=[
            pl.BlockSpec((None, l, d_att), lambda b, j: (b, 0, q_col * dv // d_att)),
            cache_spec, cache_spec, new_spec, new_spec,
            small(dh), small(dh), small(dh), small(dh), small(dv),
        ],
        out_specs=pl.BlockSpec((None, l, d_att), lambda b, j: (b, 0, 0)),
        out_shape=jax.ShapeDtypeStruct((bn, l, d_att), BF16),
        scratch_shapes=[
            pltpu.VMEM((n_heads, dv, 2 * l), BF16),
            pltpu.VMEM((n_heads, 1, 2 * l), F32),
            pltpu.VMEM((n_heads, dv + ONES_ROWS, 2 * l), F32),
            pltpu.VMEM((2, n_heads, tkc, dv), F32),
            pltpu.VMEM((2, n_heads, tkc, dv), F32),
            pltpu.SemaphoreType.DMA((2, 2)),
        ],
        compiler_params=_cparams("arbitrary", "arbitrary"),
        name="diff_attention_cached",
    )(qsrc, k_cache, v_cache, k, v, *params)


def _ssd_kernel(z_ref, x_ref, bc_ref, dt_ref, cbx_ref, cbbc_ref, s0_ref, cwx_ref, cbiasx_ref, cwbc_ref,
                cbiasbc_ref, dtb_ref, alog_ref, dskip_ref, nw_ref, expand_ref,
                y_ref, s_ref, xsx_ref, xsbc_ref, *, n_grp, hpg, p_dim, n_dim):
    t = pl.program_id(1)
    q = CHUNK
    d_ssd = n_grp * hpg * p_dim
    gw = hpg * p_dim

    @pl.when(t == 0)
    def _():
        _conv_init(xsx_ref, cbx_ref[...])
        _conv_init(xsbc_ref, cbbc_ref[...])
        s_ref[...] = s0_ref[...]

    xs_all = _silu(_conv_tile(xsx_ref, x_ref[...], cwx_ref, cbiasx_ref))
    bcs_all = _silu(_conv_tile(xsbc_ref, bc_ref[...], cwbc_ref, cbiasbc_ref))

    a = -jnp.exp(alog_ref[...])
    expand = expand_ref[...]
    nw = nw_ref[...]
    row = lax.broadcasted_iota(jnp.int32, (q, q), 0)
    col = lax.broadcasted_iota(jnp.int32, (q, q), 1)
    causal = col <= row
    lane = lax.broadcasted_iota(jnp.int32, (q, LANES), 1)
    heads_per_tile = LANES // p_dim
    grp = lambda v, g: v[:, g * gw:(g + 1) * gw]

    def prepare(rows):
        xs, bcs = xs_all[rows], bcs_all[rows]
        dt = _softplus(dt_ref[rows, :] + dtb_ref[...])
        acum = _cumsum_rows(dt * a)
        a_last = acum[q - 1:q, :]
        wst = jnp.exp(a_last - acum) * dt
        eac = jnp.exp(acum)
        acum_t = jnp.transpose(acum)
        stacked = jnp.concatenate([dt, wst, eac], axis=0)
        hi = stacked.astype(BF16)
        rem = stacked - hi.astype(F32)
        mid = rem.astype(BF16)
        lo = (rem - mid.astype(F32)).astype(BF16)
        expd = _dot(hi, expand) + _dot(mid, expand) + _dot(lo, expand)
        dt_e, wst_e, eac_e = expd[0:q], expd[q:2 * q], expd[2 * q:3 * q]
        xdt = (xs * dt_e).astype(BF16)
        xw = (xs * wst_e).astype(BF16)
        bgs = [bcs[:, g * n_dim:(g + 1) * n_dim] for g in range(n_grp)]
        cgs = [bcs[:, (n_grp + g) * n_dim:(n_grp + g + 1) * n_dim].astype(BF16) for g in range(n_grp)]
        cbs = [lax.dot_general(cgs[g], bgs[g].astype(BF16), _NT, preferred_element_type=F32)
               for g in range(n_grp)]
        bg_ts = [jnp.transpose(bgs[g]).astype(BF16) for g in range(n_grp)]
        wdec = []
        for h in range(n_grp * hpg):
            dec = jnp.exp(jnp.where(causal, acum[:, h:h + 1] - acum_t[h:h + 1, :], -jnp.inf))
            wdec.append((cbs[h // hpg] * dec).astype(BF16))
        y_diag = []
        for tile in range(d_ssd // LANES):
            c0 = tile * LANES
            x_tile = xdt[:, c0:c0 + LANES]
            acc = None
            for hh in range(heads_per_tile):
                yh = _dot(wdec[c0 // p_dim + hh], x_tile)
                sel = jnp.logical_and(lane >= hh * p_dim, lane < (hh + 1) * p_dim)
                acc = jnp.where(sel, yh, 0.0) if acc is None else jnp.where(sel, yh, acc)
            y_diag.append(acc)
        y_diag = jnp.concatenate(y_diag, axis=1) + dskip_ref[...] * xs
        return y_diag, _silu(z_ref[rows, :]), cgs, bg_ts, xw, eac_e

    def advance(rows, prep):
        y_diag, gate, cgs, bg_ts, xw, eac_e = prep
        y_parts = []
        for g in range(n_grp):
            s_prev = s_ref[g]
            y_parts.append(_dot(cgs[g], s_prev.astype(BF16)) * grp(eac_e, g))
            s_ref[g] = grp(eac_e, g)[q - 1:q, :] * s_prev + _dot(bg_ts[g], grp(xw, g))
        y = (y_diag + jnp.concatenate(y_parts, axis=1)) * gate
        y_ref[rows, :] = jnp.concatenate([_rms(grp(y, g), grp(nw, g)) for g in range(n_grp)],
                                         axis=1).astype(BF16)

    chunks = [slice(r0, r0 + q) for r0 in range(0, x_ref.shape[0], q)]
    preps = [prepare(rows) for rows in chunks]
    for rows, prep in zip(chunks, preps):
        advance(rows, prep)


def _ssd(proj, cols, conv_buf, s0, conv_w, conv_b, dt_bias, a_log, d_skip, norm_w, n_grp, n_dim, p_dim):
    bn, l, _ = proj.shape
    n_heads = dt_bias.shape[0]
    hpg = n_heads // n_grp
    d_ssd = n_heads * p_dim
    d_bc = 2 * n_grp * n_dim
    gw = hpg * p_dim
    q = CHUNKS_PER_STEP * CHUNK if l % (CHUNKS_PER_STEP * CHUNK) == 0 else CHUNK
    assert l % q == 0 and n_heads <= LANES and LANES % p_dim == 0
    z_col, x_col, bc_col, dt_col = cols
    pad = lambda v: jnp.pad(v.astype(F32), (0, LANES - n_heads)).reshape(1, LANES)
    expand = (jnp.arange(LANES)[:, None] == (jnp.arange(d_ssd)[None, :] // p_dim)).astype(BF16)
    s0_t = s0.reshape(bn, n_grp, hpg, p_dim, n_dim).transpose(0, 1, 4, 2, 3).reshape(bn, n_grp, n_dim, gw)
    const2 = lambda b, t: (0, 0)
    kern = functools.partial(_ssd_kernel, n_grp=n_grp, hpg=hpg, p_dim=p_dim, n_dim=n_dim)
    y, s_last = pl.pallas_call(
        kern,
        grid=(bn, l // q),
        in_specs=[
            pl.BlockSpec((None, q, d_ssd), lambda b, t: (b, t, z_col)),
            pl.BlockSpec((None, q, d_ssd), lambda b, t: (b, t, x_col)),
            pl.BlockSpec((None, q, d_bc), lambda b, t: (b, t, bc_col)),
            pl.BlockSpec((None, q, LANES), lambda b, t: (b, t, dt_col)),
            pl.BlockSpec((None, CONV_W - 1, d_ssd), lambda b, t: (b, 0, 0)),
            pl.BlockSpec((None, CONV_W - 1, d_bc), lambda b, t: (b, 0, 0)),
            pl.BlockSpec((None, n_grp, n_dim, gw), lambda b, t: (b, 0, 0, 0)),
            pl.BlockSpec((CONV_W * SUBLANES, d_ssd), const2),
            pl.BlockSpec((SUBLANES, d_ssd), const2),
            pl.BlockSpec((CONV_W * SUBLANES, d_bc), const2),
            pl.BlockSpec((SUBLANES, d_bc), const2),
            pl.BlockSpec((1, LANES), const2),
            pl.BlockSpec((1, LANES), const2),
            pl.BlockSpec((1, d_ssd), const2),
            pl.BlockSpec((1, d_ssd), const2),
            pl.BlockSpec((LANES, d_ssd), const2),
        ],
        out_specs=[
            pl.BlockSpec((None, q, d_ssd), lambda b, t: (b, t, 0)),
            pl.BlockSpec((None, n_grp, n_dim, gw), lambda b, t: (b, 0, 0, 0)),
        ],
        out_shape=[
            jax.ShapeDtypeStruct((bn, l, d_ssd), BF16),
            jax.ShapeDtypeStruct((bn, n_grp, n_dim, gw), F32),
        ],
        scratch_shapes=[
            pltpu.VMEM((SUBLANES, d_ssd), F32),
            pltpu.VMEM((SUBLANES, d_bc), F32),
        ],
        compiler_params=_cparams("parallel", "arbitrary"),
        name="ssd",
    )(proj, proj, proj, proj, conv_buf[:, :, :d_ssd], conv_buf[:, :, d_ssd:], s0_t,
      *_conv_rep(conv_w[:, :d_ssd], conv_b[:d_ssd]), *_conv_rep(conv_w[:, d_ssd:], conv_b[d_ssd:]),
      pad(dt_bias), pad(a_log), jnp.repeat(d_skip.astype(F32), p_dim).reshape(1, d_ssd),
      norm_w.reshape(1, d_ssd), expand)
    s_last = s_last.reshape(bn, n_grp, n_dim, hpg, p_dim).transpose(0, 1, 3, 4, 2).reshape(bn, n_heads, p_dim, n_dim)
    return y, s_last


def _hgrn_kernel(q_ref, f_ref, i_ref, lb_ref, nw_ref, s0_ref, y_ref, slast_ref, *, n_heads):
    t = pl.program_id(1)
    c = CHUNK
    nsb = c // HG_SUB

    @pl.when(t == 0)
    def _():
        slast_ref[...] = s0_ref[...]

    lbp = lb_ref[...]
    e = jnp.exp(lbp - jnp.max(lbp, axis=0, keepdims=True))
    lb = e[1:2, :] / jnp.sum(e, axis=0, keepdims=True)
    zero_blk = jnp.zeros((HG_SUB, LANES), F32)
    r2 = lax.broadcasted_iota(jnp.int32, (c, c), 0)
    c2 = lax.broadcasted_iota(jnp.int32, (c, c), 1)
    causal = c2 <= r2
    nw = nw_ref[...]

    def prepare(rows):
        g = lb + (1.0 - lb) * _sigmoid(f_ref[rows, :])
        logg = jnp.log(g)
        kk = 1.0 - g
        qq = _silu(q_ref[rows, :])
        b = _cumsum_rows(logg)
        iv = i_ref[rows, :]
        b_last = b[c - 1:c, :]
        qe = (qq * jnp.exp(b)).astype(BF16)
        kd = (kk * jnp.exp(b_last - b)).astype(BF16)
        s_decay = jnp.exp(b_last)
        ivb = iv.astype(BF16)
        sb = lambda x, r: x[r * HG_SUB:(r + 1) * HG_SUB, :]
        m_off = [b[a * HG_SUB - 1:a * HG_SUB, :] for a in range(1, nsb)]
        m_mid = [b[r * HG_SUB + HG_SUB // 2 - 1:r * HG_SUB + HG_SUB // 2, :] for r in range(nsb)]
        q_off = [None] + [sb(qq, a) * jnp.exp(sb(b, a) - m_off[a - 1]) for a in range(1, nsb)]
        q_mid = [sb(qq, r) * jnp.exp(sb(b, r) - m_mid[r]) for r in range(nsb)]
        k_mid = [sb(kk, r) * jnp.exp(m_mid[r] - sb(b, r)) for r in range(nsb)]
        k_off = [[sb(kk, r) * jnp.exp(m_off[a - 1] - sb(b, r)) if a > r else None for a in range(1, nsb)]
                 for r in range(nsb)]
        atts = []
        for h in range(n_heads):
            sl = slice(h * LANES, (h + 1) * LANES)
            q_rows, k_rows = [], []
            for r in range(nsb):
                q_parts = [q_off[a][:, sl] if a == r else zero_blk for a in range(1, nsb)]
                k_parts = [k_off[r][a - 1][:, sl] if a > r else zero_blk for a in range(1, nsb)]
                q_parts += [q_mid[r][:, sl] if d == r else zero_blk for d in range(nsb)]
                k_parts += [k_mid[r][:, sl] if d == r else zero_blk for d in range(nsb)]
                q_rows.append(jnp.concatenate(q_parts, axis=1))
                k_rows.append(jnp.concatenate(k_parts, axis=1))
            qcat = jnp.concatenate(q_rows, axis=0).astype(BF16)
            kcat = jnp.concatenate(k_rows, axis=0).astype(BF16)
            att = lax.dot_general(qcat, kcat, _NT, preferred_element_type=F32)
            atts.append(jnp.where(causal, att, 0.0).astype(BF16))
        return atts, ivb, iv, qe, kd, s_decay

    def advance(rows, prep):
        atts, ivb, iv, qe, kd, s_decay = prep
        outs = []
        for h in range(n_heads):
            sl = slice(h * LANES, (h + 1) * LANES)
            s_t = slast_ref[h]
            o = _dot(atts[h], ivb[:, sl])
            o = o + lax.dot_general(qe[:, sl], s_t.astype(BF16), _NT, preferred_element_type=F32)
            slast_ref[h] = s_decay[:, sl] * s_t + _dot(jnp.transpose(iv[:, sl]).astype(BF16), kd[:, sl])
            outs.append(_rms(o, nw))
        y_ref[rows, :] = jnp.concatenate(outs, axis=1).astype(BF16)

    chunks = [slice(r0, r0 + c) for r0 in range(0, q_ref.shape[0], c)]
    preps = [prepare(rows) for rows in chunks]
    for rows, prep in zip(chunks, preps):
        advance(rows, prep)


def _hgrn(proj, cols, s0, lb_param, norm_w, n_heads):
    bn, l, _ = proj.shape
    d = n_heads * LANES
    c = CHUNKS_PER_STEP * CHUNK if l % (CHUNKS_PER_STEP * CHUNK) == 0 else CHUNK
    assert l % c == 0
    q_col, f_col, i_col = cols
    depth = lb_param.shape[0]
    assert depth == 2
    s0_t = jnp.swapaxes(s0, -1, -2)
    kern = functools.partial(_hgrn_kernel, n_heads=n_heads)
    y, s_last = pl.pallas_call(
        kern,
        grid=(bn, l // c),
        in_specs=[
            pl.BlockSpec((None, c, d), lambda b, t: (b, t, q_col)),
            pl.BlockSpec((None, c, d), lambda b, t: (b, t, f_col)),
            pl.BlockSpec((None, c, d), lambda b, t: (b, t, i_col)),
            pl.BlockSpec((depth, d), lambda b, t: (0, 0)),
            pl.BlockSpec((1, LANES), lambda b, t: (0, 0)),
            pl.BlockSpec((None, n_heads, LANES, LANES), lambda b, t: (b, 0, 0, 0)),
        ],
        out_specs=[
            pl.BlockSpec((None, c, d), lambda b, t: (b, t, 0)),
            pl.BlockSpec((None, n_heads, LANES, LANES), lambda b, t: (b, 0, 0, 0)),
        ],
        out_shape=[
            jax.ShapeDtypeStruct((bn, l, d), BF16),
            jax.ShapeDtypeStruct((bn, n_heads, LANES, LANES), F32),
        ],
        compiler_params=_cparams("parallel", "arbitrary"),
        name="hgrn2",
    )(proj, proj, proj, lb_param, norm_w.reshape(1, LANES), s0_t)
    return y, jnp.swapaxes(s_last, -1, -2)


def kernel(x_prompt, x_sample, cache_diff_k, cache_diff_v, state_rglru_conv, state_rglru_h, state_ssd_conv,
           state_ssd, state_hgrn, norm_w, l0_w_in, l0_conv_w, l0_conv_b, l0_rg_w_r, l0_rg_b_r, l0_rg_w_i,
           l0_rg_b_i, l0_rg_lambda, l0_lq1, l0_lk1, l0_lq2, l0_lk2, l0_subln_w, l0_w_out, l1_w_in, l1_conv_w,
           l1_conv_b, l1_dt_bias, l1_a_log, l1_d_skip, l1_ssd_norm_w, l1_hg_lower_bound, l1_hg_norm_w, l1_w_out,
           ffn_w_gate, ffn_w_up, ffn_w_down):
    d_model = x_prompt.shape[-1]
    d_rnn = l0_conv_w.shape[1]
    n_heads_b, dv_b = cache_diff_k.shape[2], cache_diff_k.shape[3]
    d_att = n_heads_b * dv_b
    n_heads_c, p_c, n_c = state_ssd.shape[1:]
    d_ssd = n_heads_c * p_c
    d_xbc = l1_conv_w.shape[1]
    d_bc = d_xbc - d_ssd
    n_grp = d_bc // (2 * n_c)
    n_heads_d, dk_d, dv_d = state_hgrn.shape[1:]
    d_hg = n_heads_d * dk_d
    assert dv_b == LANES and dk_d == LANES and dv_d == LANES and d_rnn == d_att == d_ssd == d_hg

    w_in0_main = l0_w_in[:, :2 * d_rnn + d_att].astype(BF16)
    w_in0_kv = l0_w_in[:, 2 * d_rnn + d_att:].astype(BF16)
    o1 = d_ssd
    o2 = o1 + d_xbc
    o3 = o2 + n_heads_c
    w1 = l1_w_in
    w_in1 = jnp.concatenate(
        [w1[:, :o1], w1[:, o1:o1 + d_ssd], w1[:, o3:], w1[:, o1 + d_ssd:o2],
         jnp.pad(w1[:, o2:o3], ((0, 0), (0, LANES - n_heads_c)))], axis=1).astype(BF16)
    w_out0 = l0_w_out.astype(BF16)
    w_out1 = l1_w_out.astype(BF16)
    wg = ffn_w_gate.astype(BF16)
    wu = ffn_w_up.astype(BF16)
    wd = ffn_w_down.astype(BF16)
    w_r = l0_rg_w_r.astype(BF16)
    w_i = l0_rg_w_i.astype(BF16)
    tn0, tn1 = IN_PROJ_COLS
    assert w_in0_main.shape[1] % tn0 == 0 and w_in1.shape[1] % tn1 == 0
    ssd_cols = (0, 1, (2 * d_ssd + 3 * d_hg) // d_bc, (2 * d_ssd + 3 * d_hg + d_bc) // LANES)
    hg_cols = (2, 3, 4)

    def trunk(x, k_cache, v_cache, rg_conv, rg_h, ssd_conv, ssd_s, hg_s):
        bn, l, _ = x.shape
        m = bn * l
        x0 = x.reshape(m, d_model)
        proj0, xn0 = _norm_matmul(x0, norm_w[0, 0], w_in0_main, tn0)
        proj0 = proj0.reshape(bn, l, -1)
        k_new, v_new, k_bf, v_bf = _head_proj(xn0, w_in0_kv, n_heads_b)
        k_new = k_new.reshape(bn, l, n_heads_b, dv_b)
        v_new = v_new.reshape(bn, l, n_heads_b, dv_b)
        ya, rg_h_new = _rglru(proj0, rg_conv, rg_h, l0_conv_w, l0_conv_b, w_r, l0_rg_b_r, w_i, l0_rg_b_i,
                              l0_rg_lambda, d_rnn)
        yb = _diff_attention(proj0, 2 * d_rnn // LANES, k_bf.reshape(bn, l, d_att), v_bf.reshape(bn, l, d_att),
                             k_cache, v_cache, l0_lq1, l0_lk1, l0_lq2, l0_lk2, l0_subln_w, n_heads_b, 0)
        rg_conv_new = proj0[:, l - (CONV_W - 1):, :d_rnn]
        x1, hn1 = _out_proj(ya.reshape(m, d_rnn), yb.reshape(m, d_att), w_out0, x0, norm_w[0, 1], norm_w[0, 2])
        x2, xn2 = _ffn(x1, hn1, norm_w[0, 3], norm_w[1, 0], wg, wu, wd, 0)
        proj1 = _matmul(xn2, w_in1, tn1).reshape(bn, l, -1)
        ys, ssd_s_new = _ssd(proj1, ssd_cols, ssd_conv, ssd_s, l1_conv_w, l1_conv_b, l1_dt_bias, l1_a_log,
                             l1_d_skip, l1_ssd_norm_w, n_grp, n_c, p_c)
        yh, hg_s_new = _hgrn(proj1, hg_cols, hg_s, l1_hg_lower_bound, l1_hg_norm_w, n_heads_d)
        tail = proj1[:, l - (CONV_W - 1):, :]
        ssd_conv_new = jnp.concatenate(
            [tail[:, :, d_ssd:2 * d_ssd], tail[:, :, 2 * d_ssd + 3 * d_hg:2 * d_ssd + 3 * d_hg + d_bc]], axis=-1)
        x3, hn3 = _out_proj(ys.reshape(m, d_ssd), yh.reshape(m, d_hg), w_out1, x2, norm_w[1, 1], norm_w[1, 2])
        x4 = _ffn(x3, hn3, norm_w[1, 3], None, wg, wu, wd, 1)[0]
        return (x4.reshape(bn, l, d_model), k_new, v_new, rg_conv_new, rg_h_new, ssd_conv_new, ssd_s_new, hg_s_new)

    bp = x_prompt.shape[0]
    zeros = lambda *s: jnp.zeros(s, F32)
    outs_p = trunk(x_prompt, None, None, zeros(bp, CONV_W - 1, d_rnn), zeros(bp, d_rnn),
                   zeros(bp, CONV_W - 1, d_xbc), zeros(bp, n_heads_c, p_c, n_c), zeros(bp, n_heads_d, dk_d, dv_d))
    outs_s = trunk(x_sample, cache_diff_k, cache_diff_v, state_rglru_conv, state_rglru_h, state_ssd_conv,
                   state_ssd, state_hgrn)
    return (outs_p[0], outs_s[0]) + tuple(outs_p[1:]) + tuple(outs_s[1:])
```
